```python
import math
import jax, jax.numpy as jnp
from jax import lax
import numpy as np

D_MODEL = 1024
BATCH = 8
SEQ = 2048
DEPTH = 4
DEC_BATCH = 128
DEC_SEQ = 1
PAST_LEN = 16384
PAGE_SIZE = 128

N_MIXERS = 3
N_A = (DEPTH + 2) // 3
N_B = (DEPTH + 1) // 3
N_C = DEPTH // 3
N_DENSE = (DEPTH + 1) // 2
N_MOE = DEPTH // 2
EPS = 1e-6

CHUNK = 128
GM_WIDTH = 2 * D_MODEL
GM_GROUPS = 8
GM_GROUP_DIM = GM_WIDTH // GM_GROUPS

SSM_INNER = 2 * D_MODEL
SSM_HEAD_DIM = 64
SSM_HEADS = SSM_INNER // SSM_HEAD_DIM
SSM_GROUPS = 8
SSM_HEADS_PER_GROUP = SSM_HEADS // SSM_GROUPS
SSM_STATE = 128
SSM_CONV = 4
SSM_CHUNK = 128
SSM_BC = SSM_GROUPS * SSM_STATE
SSM_CONV_DIM = SSM_INNER + 2 * SSM_BC
SSM_IN_DIM = SSM_INNER + SSM_CONV_DIM + SSM_HEADS

CFM_KERNEL = 31

D_FF = 2816
N_EXPERTS = 8
TOP_K = 2

kernel_name = 'hybrid_gmlp_ssd_conformer_moe_decode_step'


def rmsnorm(x, g):
    xf = x.astype(jnp.float32)
    y = xf * lax.rsqrt(jnp.mean(xf * xf, axis=-1, keepdims=True) + EPS)
    return (y * g).astype(x.dtype)


def layernorm(x, g, b):
    xf = x.astype(jnp.float32)
    xc = xf - jnp.mean(xf, axis=-1, keepdims=True)
    y = xc * lax.rsqrt(jnp.mean(xc * xc, axis=-1, keepdims=True) + EPS)
    return (y * g + b).astype(x.dtype)


def causal_dwconv(ctx, w, b):
    ch = ctx.shape[-1]
    y = lax.conv_general_dilated(ctx, w.astype(ctx.dtype)[:, None, :], window_strides=(1,), padding='VALID',
                                 dimension_numbers=('NWC', 'WIO', 'NWC'), feature_group_count=ch)
    return y + b


def swiglu(h, w_gate, w_up, w_down):
    return (jax.nn.silu(h @ w_gate) * (h @ w_up)) @ w_down


def moe_swiglu(h, w_router, w_gate, w_up, w_down):
    logits = (h @ w_router).astype(jnp.float32)
    top_v, top_i = lax.top_k(logits, TOP_K)
    top_w = jax.nn.softmax(top_v, axis=-1)
    gate = jnp.sum(jax.nn.one_hot(top_i, N_EXPERTS, dtype=jnp.float32) * top_w[..., None], axis=-2)
    y = jnp.zeros(h.shape, jnp.float32)
    for e in range(N_EXPERTS):
        y = y + gate[..., e:e + 1] * swiglu(h, w_gate[e], w_up[e], w_down[e]).astype(jnp.float32)
    return y.astype(h.dtype)


def chunk_gmlp(x, w_in, b_in, ln_g, ln_b, w_s, b_s, w_out):
    bsz, t_len, _ = x.shape
    hid = jax.nn.gelu(x @ w_in + b_in, approximate=False)
    u, v = hid[..., :GM_WIDTH], layernorm(hid[..., GM_WIDTH:], ln_g, ln_b)
    cl = CHUNK if t_len % CHUNK == 0 else t_len
    vg = v.reshape(bsz, t_len // cl, cl, GM_GROUPS, GM_GROUP_DIM)
    w_causal = jnp.where(jnp.tril(jnp.ones((cl, cl), dtype=bool)), w_s[:, :cl, :cl], 0)
    s = jnp.einsum('gts,bcsgd->bctgd', w_causal, vg) + b_s[:, :cl].T[None, None, :, :, None]
    out = (u * s.reshape(bsz, t_len, GM_WIDTH)) @ w_out
    return out, v[:, t_len - cl:]


def ssd_scan(x, dt, a, bm, cm, h0):
    bsz, t_len = x.shape[:2]
    cl = SSM_CHUNK if t_len % SSM_CHUNK == 0 else t_len
    nc = t_len // cl

    def to_chunks(arr):
        return jnp.moveaxis(arr.reshape(bsz, nc, cl, *arr.shape[2:]), 1, 0)

    mask = jnp.tril(jnp.ones((cl, cl), dtype=bool))[None, :, :, None, None]

    def step(h, inp):
        xc, dtc, bc, cc = inp
        cum = jnp.cumsum(dtc * a, axis=1)
        seg = cum[:, :, None] - cum[:, None, :]
        decay = jnp.where(mask, jnp.exp(jnp.where(mask, seg, 0.0)), 0.0)
        cb = jnp.einsum('btgn,bsgn->btsg', cc, bc)
        wts = cb[..., None] * decay * dtc[:, None]
        y = jnp.einsum('btsgr,bsgrp->btgrp', wts, xc)
        y = y + jnp.einsum('btgn,bgrpn->btgrp', cc, h) * jnp.exp(cum)[..., None]
        w_in = jnp.exp(cum[:, -1:] - cum) * dtc
        h = h * jnp.exp(cum[:, -1])[..., None, None] + jnp.einsum('bsgn,bsgrp->bgrpn', bc, xc * w_in[..., None])
        return h, y

    h_last, ys = lax.scan(step, h0, (to_chunks(x), to_chunks(dt), to_chunks(bm), to_chunks(cm)))
    y = jnp.moveaxis(ys, 0, 1).reshape(bsz, t_len, *x.shape[2:])
    return y, h_last


def mamba2_mixer(x, conv_buf, h0, w_in, conv_w, conv_b, dt_bias, a_log, d_skip, norm_g, w_out):
    bsz, t_len, _ = x.shape
    proj = x @ w_in
    z = proj[..., :SSM_INNER]
    xbc = proj[..., SSM_INNER:SSM_INNER + SSM_CONV_DIM]
    dt_raw = proj[..., SSM_INNER + SSM_CONV_DIM:]
    ctx = jnp.concatenate([conv_buf.astype(xbc.dtype), xbc], axis=1)
    new_buf = ctx[:, -(SSM_CONV - 1):]
    xbc = jax.nn.silu(causal_dwconv(ctx, conv_w, conv_b)).astype(jnp.float32)
    g, r = SSM_GROUPS, SSM_HEADS_PER_GROUP
    xs = xbc[..., :SSM_INNER].reshape(bsz, t_len, g, r, SSM_HEAD_DIM)
    bm = xbc[..., SSM_INNER:SSM_INNER + SSM_BC].reshape(bsz, t_len, g, SSM_STATE)
    cm = xbc[..., SSM_INNER + SSM_BC:].reshape(bsz, t_len, g, SSM_STATE)
    dt = jax.nn.softplus(dt_raw.astype(jnp.float32) + dt_bias.astype(jnp.float32)).reshape(bsz, t_len, g, r)
    a = -jnp.exp(a_log.astype(jnp.float32)).reshape(g, r)
    h0g = h0.astype(jnp.float32).reshape(bsz, g, r, SSM_HEAD_DIM, SSM_STATE)
    y, h_last = ssd_scan(xs, dt, a, bm, cm, h0g)
    y = y + d_skip.astype(jnp.float32).reshape(g, r)[:, :, None] * xs
    y = y.reshape(bsz, t_len, SSM_INNER) * jax.nn.silu(z.astype(jnp.float32))
    yg = y.reshape(bsz, t_len, SSM_GROUPS, SSM_INNER // SSM_GROUPS)
    yg = yg * lax.rsqrt(jnp.mean(yg * yg, axis=-1, keepdims=True) + EPS)
    y = (yg.reshape(bsz, t_len, SSM_INNER) * norm_g).astype(x.dtype)
    return y @ w_out, new_buf, h_last.reshape(bsz, SSM_HEADS, SSM_HEAD_DIM, SSM_STATE)


def conformer_conv(x, buf, w_pw1, b_pw1, dw_w, dw_b, ln_g, ln_b, w_pw2, b_pw2):
    a = x @ w_pw1 + b_pw1
    glu = a[..., :D_MODEL] * jax.nn.sigmoid(a[..., D_MODEL:])
    ctx = jnp.concatenate([buf.astype(glu.dtype), glu], axis=1)
    new_buf = ctx[:, -(CFM_KERNEL - 1):]
    hc = jax.nn.silu(layernorm(causal_dwconv(ctx, dw_w, dw_b), ln_g, ln_b))
    return hc @ w_pw2 + b_pw2, new_buf


def setup_inputs(seed: int = 0) -> dict:
    key = jax.random.key(seed)
    ks = iter(jax.random.split(key, 48))

    def nrm(shape, scale):
        return scale * jax.random.normal(next(ks), shape, jnp.float32)

    dt0 = jnp.exp(jax.random.uniform(next(ks), (N_B, SSM_HEADS), jnp.float32) * (math.log(0.1) - math.log(1e-3)) + math.log(1e-3))
    inp = {}
    inp['x_prompt'] = nrm((BATCH, SEQ, D_MODEL), 1.0)
    inp['x_sample'] = nrm((DEC_BATCH, DEC_SEQ, D_MODEL), 1.0)
    inp['state_ssm'] = nrm((N_B, DEC_BATCH, SSM_HEADS, SSM_HEAD_DIM, SSM_STATE), 0.1)
    inp['state_conv_ssm'] = nrm((N_B, DEC_BATCH, SSM_CONV - 1, SSM_CONV_DIM), 1.0)
    inp['state_conv_cfm'] = nrm((N_C, DEC_BATCH, CFM_KERNEL - 1, D_MODEL), 0.5)
    inp['norm_mix_g'] = 1.0 + nrm((DEPTH, D_MODEL), 0.02)
    inp['norm_ffn_g'] = 1.0 + nrm((DEPTH, D_MODEL), 0.02)
    inp['norm_final_g'] = 1.0 + nrm((D_MODEL,), 0.02)
    inp['a_w_in'] = nrm((N_A, D_MODEL, 2 * GM_WIDTH), D_MODEL ** -0.5)
    inp['a_b_in'] = nrm((N_A, 2 * GM_WIDTH), 0.02)
    inp['a_ln_g'] = 1.0 + nrm((N_A, GM_WIDTH), 0.02)
    inp['a_ln_b'] = nrm((N_A, GM_WIDTH), 0.02)
    inp['a_w_s'] = nrm((N_A, GM_GROUPS, CHUNK, CHUNK), CHUNK ** -0.5)
    inp['a_b_s'] = 1.0 + nrm((N_A, GM_GROUPS, CHUNK), 0.02)
    inp['a_w_out'] = nrm((N_A, GM_WIDTH, D_MODEL), GM_WIDTH ** -0.5)
    inp['b_w_in'] = nrm((N_B, D_MODEL, SSM_IN_DIM), D_MODEL ** -0.5)
    inp['b_conv_w'] = nrm((N_B, SSM_CONV, SSM_CONV_DIM), SSM_CONV ** -0.5)
    inp['b_conv_b'] = nrm((N_B, SSM_CONV_DIM), 0.02)
    inp['b_dt_bias'] = dt0 + jnp.log(-jnp.expm1(-dt0))
    inp['b_a_log'] = jnp.log(jax.random.uniform(next(ks), (N_B, SSM_HEADS), jnp.float32, 1.0, 16.0))
    inp['b_d'] = 1.0 + nrm((N_B, SSM_HEADS), 0.02)
    inp['b_norm_g'] = 1.0 + nrm((N_B, SSM_INNER), 0.02)
    inp['b_w_out'] = nrm((N_B, SSM_INNER, D_MODEL), SSM_INNER ** -0.5)
    inp['c_w_pw1'] = nrm((N_C, D_MODEL, 2 * D_MODEL), D_MODEL ** -0.5)
    inp['c_b_pw1'] = nrm((N_C, 2 * D_MODEL), 0.02)
    inp['c_dw_w'] = nrm((N_C, CFM_KERNEL, D_MODEL), CFM_KERNEL ** -0.5)
    inp['c_dw_b'] = nrm((N_C, D_MODEL), 0.02)
    inp['c_ln_g'] = 1.0 + nrm((N_C, D_MODEL), 0.02)
    inp['c_ln_b'] = nrm((N_C, D_MODEL), 0.02)
    inp['c_w_pw2'] = nrm((N_C, D_MODEL, D_MODEL), D_MODEL ** -0.5)
    inp['c_b_pw2'] = nrm((N_C, D_MODEL), 0.02)
    inp['f_w_gate'] = nrm((N_DENSE, D_MODEL, D_FF), D_MODEL ** -0.5)
    inp['f_w_up'] = nrm((N_DENSE, D_MODEL, D_FF), D_MODEL ** -0.5)
    inp['f_w_down'] = nrm((N_DENSE, D_FF, D_MODEL), D_FF ** -0.5)
    inp['e_w_router'] = nrm((N_MOE, D_MODEL, N_EXPERTS), D_MODEL ** -0.5)
    inp['e_w_gate'] = nrm((N_MOE, N_EXPERTS, D_MODEL, D_FF), D_MODEL ** -0.5)
    inp['e_w_up'] = nrm((N_MOE, N_EXPERTS, D_MODEL, D_FF), D_MODEL ** -0.5)
    inp['e_w_down'] = nrm((N_MOE, N_EXPERTS, D_FF, D_MODEL), D_FF ** -0.5)
    return inp


def reference(x_prompt, x_sample, state_ssm, state_conv_ssm, state_conv_cfm,
              norm_mix_g, norm_ffn_g, norm_final_g,
              a_w_in, a_b_in, a_ln_g, a_ln_b, a_w_s, a_b_s, a_w_out,
              b_w_in, b_conv_w, b_conv_b, b_dt_bias, b_a_log, b_d, b_norm_g, b_w_out,
              c_w_pw1, c_b_pw1, c_dw_w, c_dw_b, c_ln_g, c_ln_b, c_w_pw2, c_b_pw2,
              f_w_gate, f_w_up, f_w_down,
              e_w_router, e_w_gate, e_w_up, e_w_down):
    xp, xs = x_prompt, x_sample
    v_p, v_s, ssm_p, ssm_s, cs_p, cs_s, cc_p, cc_s = [], [], [], [], [], [], [], []
    for i in range(DEPTH):
        kind, j = i % N_MIXERS, i // N_MIXERS
        hp, hs = rmsnorm(xp, norm_mix_g[i]), rmsnorm(xs, norm_mix_g[i])
        if kind == 0:
            prm = (a_w_in[j], a_b_in[j], a_ln_g[j], a_ln_b[j], a_w_s[j], a_b_s[j], a_w_out[j])
            op, vp = chunk_gmlp(hp, *prm)
            os_, vs = chunk_gmlp(hs, *prm)
            v_p.append(vp)
            v_s.append(vs)
        elif kind == 1:
            prm = (b_w_in[j], b_conv_w[j], b_conv_b[j], b_dt_bias[j], b_a_log[j], b_d[j], b_norm_g[j], b_w_out[j])
            bp = xp.shape[0]
            zero_buf = jnp.zeros((bp, SSM_CONV - 1, SSM_CONV_DIM), hp.dtype)
            zero_h = jnp.zeros((bp, SSM_HEADS, SSM_HEAD_DIM, SSM_STATE), jnp.float32)
            op, cbp, hlp = mamba2_mixer(hp, zero_buf, zero_h, *prm)
            os_, cbs, hls = mamba2_mixer(hs, state_conv_ssm[j], state_ssm[j], *prm)
            ssm_p.append(hlp)
            ssm_s.append(hls)
            cs_p.append(cbp)
            cs_s.append(cbs)
        else:
            prm = (c_w_pw1[j], c_b_pw1[j], c_dw_w[j], c_dw_b[j], c_ln_g[j], c_ln_b[j], c_w_pw2[j], c_b_pw2[j])
            zero_buf = jnp.zeros((xp.shape[0], CFM_KERNEL - 1, D_MODEL), hp.dtype)
            op, cbp = conformer_conv(hp, zero_buf, *prm)
            os_, cbs = conformer_conv(hs, state_conv_cfm[j], *prm)
            cc_p.append(cbp)
            cc_s.append(cbs)
        xp = xp + op
        xs = xs + os_
        gp, gs = rmsnorm(xp, norm_ffn_g[i]), rmsnorm(xs, norm_ffn_g[i])
        k = i // 2
        if i % 2 == 0:
            xp = xp + swiglu(gp, f_w_gate[k], f_w_up[k], f_w_down[k])
            xs = xs + swiglu(gs, f_w_gate[k], f_w_up[k], f_w_down[k])
        else:
            xp = xp + moe_swiglu(gp, e_w_router[k], e_w_gate[k], e_w_up[k], e_w_down[k])
            xs = xs + moe_swiglu(gs, e_w_router[k], e_w_gate[k], e_w_up[k], e_w_down[k])
    y_prompt = rmsnorm(xp, norm_final_g)
    y_sample = rmsnorm(xs, norm_final_g)
    new_gmlp_v_prompt = jnp.stack(v_p)
    new_gmlp_v_sample = jnp.stack(v_s)
    new_ssm_prompt = jnp.stack(ssm_p)
    new_ssm_sample = jnp.stack(ssm_s)
    new_conv_ssm_prompt = jnp.stack(cs_p)
    new_conv_ssm_sample = jnp.stack(cs_s)
    new_conv_cfm_prompt = jnp.stack(cc_p)
    new_conv_cfm_sample = jnp.stack(cc_s)
    return (y_prompt, y_sample, new_gmlp_v_prompt, new_gmlp_v_sample, new_ssm_prompt, new_ssm_sample,
            new_conv_ssm_prompt, new_conv_ssm_sample, new_conv_cfm_prompt, new_conv_cfm_sample)
```

```python
import functools

import jax
import jax.numpy as jnp
from jax import lax
from jax.experimental import pallas as pl
from jax.experimental.pallas import tpu as pltpu

F32 = jnp.float32
BF16 = jnp.bfloat16
EPS = 1e-6
HIGHEST = lax.Precision.HIGHEST

LANES_V7X = 128
SUBLANES_V7X = 8
VMEM_LIMIT_V7X = 60 * 1024 * 1024

CHUNK = 128
GM_GROUPS = 8
SSM_GROUPS = 8
SSM_HEAD_DIM = 64
SSM_STATE = 128
SSM_CONV = 4
CFM_KERNEL = 31
N_EXPERTS = 8
CFM_PAD = 32
SSM_PAD = 8


def _cparams(sem):
    return pltpu.CompilerParams(dimension_semantics=sem, vmem_limit_bytes=VMEM_LIMIT_V7X)


def _const_spec(shape):
    nd = len(shape)
    return pl.BlockSpec(shape, lambda *_: (0,) * nd, pipeline_mode=pl.Buffered(1))


def _dot(a, b):
    return jnp.dot(a, b, preferred_element_type=F32)


def _dot_nt(a, b):
    return lax.dot_general(a, b, (((1,), (1,)), ((), ())), preferred_element_type=F32)


def _dot_tn(a, b):
    return lax.dot_general(a, b, (((0,), (0,)), ((), ())), preferred_element_type=F32)


def _dot_hi(a, b):
    return jnp.dot(a, b, precision=HIGHEST, preferred_element_type=F32)


def _rms(x, g):
    return x * lax.rsqrt(jnp.mean(x * x, axis=-1, keepdims=True) + EPS) * g


def _layernorm(x, g, b):
    xc = x - jnp.mean(x, axis=-1, keepdims=True)
    return xc * lax.rsqrt(jnp.mean(xc * xc, axis=-1, keepdims=True) + EPS) * g + b


def _sigmoid(x):
    return 1.0 / (1.0 + jnp.exp(-x))


def _silu(x):
    return x * _sigmoid(x)


def _gelu(x):
    return 0.5 * x * (1.0 + lax.erf(x * (2.0 ** -0.5)))


def _softplus(x):
    return jnp.maximum(x, 0.0) + jnp.log(1.0 + jnp.exp(-jnp.abs(x)))


def _gmlp_prompt_body(x_ref, g_ref, win_ref, bin_ref, lng_ref, lnb_ref, ws_ref, bs_ref, wout_ref,
                      o_ref, v_ref, gated_ref, *, tm, width):
    x = x_ref[...]
    h = _rms(x, g_ref[...]).astype(BF16)
    u = _gelu(_dot(h, win_ref[:, :width]) + bin_ref[:, :width])
    hv = _gelu(_dot(h, win_ref[:, width:]) + bin_ref[:, width:])
    v = _layernorm(hv, lng_ref[...], lnb_ref[...])
    v_ref[0] = v[tm - CHUNK:, :]
    vb = v.astype(BF16)
    gdim = width // GM_GROUPS
    causal = (lax.broadcasted_iota(jnp.int32, (CHUNK, CHUNK), 0)
              >= lax.broadcasted_iota(jnp.int32, (CHUNK, CHUNK), 1))
    for g in range(GM_GROUPS):
        wc = jnp.where(causal, ws_ref[g], 0.0).astype(BF16)
        bias = bs_ref[:, g:g + 1]
        for c in range(tm // CHUNK):
            rows = slice(c * CHUNK, (c + 1) * CHUNK)
            cols = slice(g * gdim, (g + 1) * gdim)
            s = _dot(wc, vb[rows, cols]) + bias
            gated_ref[rows, cols] = (u[rows, cols] * s).astype(BF16)
    o_ref[...] = x + _dot(gated_ref[...], wout_ref[...])


def _gmlp_sample_body(x_ref, g_ref, win_ref, bin_ref, lng_ref, lnb_ref, ws0_ref, bs0_ref, wout_ref,
                      o_ref, v_ref, *, width):
    x = x_ref[...]
    h = _rms(x, g_ref[...]).astype(BF16)
    u = _gelu(_dot(h, win_ref[:, :width]) + bin_ref[:, :width])
    hv = _gelu(_dot(h, win_ref[:, width:]) + bin_ref[:, width:])
    v = _layernorm(hv, lng_ref[...], lnb_ref[...])
    v_ref[...] = v
    s = v * ws0_ref[...] + bs0_ref[...]
    o_ref[...] = x + _dot((u * s).astype(BF16), wout_ref[...])


def _gmlp(X, n_prompt, n_batch, g, w_in, b_in, ln_g, ln_b, w_s, b_s, w_out, *, tm):
    M, D = X.shape
    width = w_out.shape[0]
    seq = n_prompt // n_batch
    tiles_per_seq = seq // tm
    n_sample = M - n_prompt
    g2 = g.reshape(1, D)
    b_in2 = b_in.reshape(1, 2 * width)
    ln_g2, ln_b2 = ln_g.reshape(1, width), ln_b.reshape(1, width)
    w_in_b, w_out_b = w_in.astype(BF16), w_out.astype(BF16)

    X, v_p = pl.pallas_call(
        functools.partial(_gmlp_prompt_body, tm=tm, width=width),
        grid=(n_prompt // tm,),
        in_specs=[
            pl.BlockSpec((tm, D), lambda i: (i, 0)),
            _const_spec((1, D)),
            _const_spec((D, 2 * width)),
            _const_spec((1, 2 * width)),
            _const_spec((1, width)),
            _const_spec((1, width)),
            _const_spec((GM_GROUPS, CHUNK, CHUNK)),
            _const_spec((CHUNK, GM_GROUPS)),
            _const_spec((width, D)),
        ],
        out_specs=[
            pl.BlockSpec((tm, D), lambda i: (i, 0)),
            pl.BlockSpec((1, CHUNK, width), lambda i: (i // tiles_per_seq, 0, 0)),
        ],
        out_shape=[jax.ShapeDtypeStruct((M, D), F32),
                   jax.ShapeDtypeStruct((n_batch, CHUNK, width), F32)],
        scratch_shapes=[pltpu.VMEM((tm, width), BF16)],
        input_output_aliases={0: 0},
        compiler_params=_cparams(("arbitrary",)),
        name="gmlp_prompt",
    )(X, g2, w_in_b, b_in2, ln_g2, ln_b2, w_s, b_s.T, w_out_b)

    gdim = width // GM_GROUPS
    ws0 = jnp.repeat(w_s[:, 0, 0], gdim).reshape(1, width)
    bs0 = jnp.repeat(b_s[:, 0], gdim).reshape(1, width)
    sblk = n_prompt // n_sample
    X, v_s = pl.pallas_call(
        functools.partial(_gmlp_sample_body, width=width),
        grid=(1,),
        in_specs=[
            pl.BlockSpec((n_sample, D), lambda i: (sblk, 0)),
            _const_spec((1, D)),
            _const_spec((D, 2 * width)),
            _const_spec((1, 2 * width)),
            _const_spec((1, width)),
            _const_spec((1, width)),
            _const_spec((1, width)),
            _const_spec((1, width)),
            _const_spec((width, D)),
        ],
        out_specs=[
            pl.BlockSpec((n_sample, D), lambda i: (sblk, 0)),
            pl.BlockSpec((n_sample, width), lambda i: (0, 0)),
        ],
        out_shape=[jax.ShapeDtypeStruct((M, D), F32),
                   jax.ShapeDtypeStruct((n_sample, width), F32)],
        input_output_aliases={0: 0},
        compiler_params=_cparams(("arbitrary",)),
        name="gmlp_sample",
    )(X, g2, w_in_b, b_in2, ln_g2, ln_b2, ws0, bs0, w_out_b)
    return X, v_p, v_s.reshape(n_sample, 1, width)


def _swiglu_tile(h, wg_ref, wu_ref, wd_ref, f_chunk):
    d_ff = wd_ref.shape[0]
    acc = None
    for f in range(d_ff // f_chunk):
        cols = slice(f * f_chunk, (f + 1) * f_chunk)
        a = _dot(h, wg_ref[:, cols])
        b = _dot(h, wu_ref[:, cols])
        part = _dot((_silu(a) * b).astype(BF16), wd_ref[cols, :])
        acc = part if acc is None else acc + part
    return acc


def _ffn_dense_body(x_ref, g_ref, wg_ref, wu_ref, wd_ref, o_ref, *, f_chunk):
    x = x_ref[...]
    h = _rms(x, g_ref[...]).astype(BF16)
    o_ref[...] = x + _swiglu_tile(h, wg_ref, wu_ref, wd_ref, f_chunk)


def _ffn_dense_rows(X, g2, wg, wu, wd, *, tm, first_block, n_tiles, f_chunk):
    M, D = X.shape
    d_ff = wd.shape[0]
    return pl.pallas_call(
        functools.partial(_ffn_dense_body, f_chunk=f_chunk),
        grid=(n_tiles,),
        in_specs=[
            pl.BlockSpec((tm, D), lambda i: (i + first_block, 0)),
            _const_spec((1, D)),
            _const_spec((D, d_ff)),
            _const_spec((D, d_ff)),
            _const_spec((d_ff, D)),
        ],
        out_specs=pl.BlockSpec((tm, D), lambda i: (i + first_block, 0)),
        out_shape=jax.ShapeDtypeStruct((M, D), F32),
        input_output_aliases={0: 0},
        compiler_params=_cparams(("arbitrary",)),
        name="ffn_dense",
    )(X, g2, wg, wu, wd)


def _ffn_dense(X, n_prompt, g, w_gate, w_up, w_down, *, tm, f_chunk):
    M, D = X.shape
    n_sample = M - n_prompt
    g2 = g.reshape(1, D)
    wg, wu, wd = w_gate.astype(BF16), w_up.astype(BF16), w_down.astype(BF16)
    X = _ffn_dense_rows(X, g2, wg, wu, wd, tm=tm, first_block=0, n_tiles=n_prompt // tm, f_chunk=f_chunk)
    X = _ffn_dense_rows(X, g2, wg, wu, wd, tm=n_sample, first_block=n_prompt // n_sample, n_tiles=1,
                        f_chunk=f_chunk)
    return X


def _router_body(x_ref, g_ref, wr_ref, gate_ref, idx_ref):
    h = _rms(x_ref[...], g_ref[...])
    logits = _dot_hi(h, wr_ref[...])
    lane = lax.broadcasted_iota(jnp.int32, logits.shape, 1)
    neg = jnp.float32(-jnp.inf)
    logits = jnp.where(lane < N_EXPERTS, logits, neg)
    m1 = jnp.max(logits, axis=-1, keepdims=True)
    i1 = jnp.min(jnp.where(logits == m1, lane, LANES_V7X), axis=-1, keepdims=True)
    rest = jnp.where(lane == i1, neg, logits)
    m2 = jnp.max(rest, axis=-1, keepdims=True)
    i2 = jnp.min(jnp.where(rest == m2, lane, LANES_V7X), axis=-1, keepdims=True)
    e = jnp.exp(m2 - m1)
    w1 = 1.0 / (1.0 + e)
    w2 = e / (1.0 + e)
    gate_ref[...] = jnp.where(lane == i1, w1, 0.0) + jnp.where(lane == i2, w2, 0.0)
    idx_ref[...] = jnp.where(lane == 0, i1, jnp.where(lane == 1, i2, 0))


def _issue_row_gather(src_hbm, idx_ref, base, buf, sem, n_rows):
    def body(r, carry):
        tok = idx_ref[base + r]
        pltpu.make_async_copy(src_hbm.at[pl.ds(tok, 1), :], buf.at[pl.ds(r, 1), :], sem).start()
        return carry
    lax.fori_loop(0, n_rows, body, 0, unroll=8)


def _wait_row_gather(src_hbm, buf, sem, n_rows):
    pltpu.make_async_copy(src_hbm.at[pl.ds(0, n_rows), :], buf, sem).wait()


def _moe_ffn_body(tile_expert_ref, n_tiles_ref, tok_ref, x_hbm, wslot_ref, g_ref, wg_ref, wu_ref, wd_ref,
                  y_ref, buf_ref, sem_ref, *, tm, f_chunk):
    t = pl.program_id(0)
    n_tiles = n_tiles_ref[0]
    slot = t % 2

    @pl.when(t == 0)
    def _():
        _issue_row_gather(x_hbm, tok_ref, 0, buf_ref.at[0], sem_ref.at[0], tm)

    @pl.when(t + 1 < n_tiles)
    def _():
        _issue_row_gather(x_hbm, tok_ref, (t + 1) * tm, buf_ref.at[1 - slot], sem_ref.at[1 - slot], tm)

    @pl.when(t < n_tiles)
    def _():
        _wait_row_gather(x_hbm, buf_ref.at[slot], sem_ref.at[slot], tm)
        h = _rms(buf_ref[slot], g_ref[...]).astype(BF16)
        y = _swiglu_tile(h, wg_ref.at[0], wu_ref.at[0], wd_ref.at[0], f_chunk)
        y_ref[...] = y * wslot_ref[...]

    @pl.when(t >= n_tiles)
    def _():
        y_ref[...] = jnp.zeros_like(y_ref)


def _moe_combine_body(p1_ref, p2_ref, x_ref, y_hbm, gf_ref, o_ref, buf1_ref, buf2_ref, sem_ref,
                      *, tm, n_steps, final_norm):
    i = pl.program_id(0)
    slot = i % 2

    def issue(step, s):
        _issue_row_gather(y_hbm, p1_ref, step * tm, buf1_ref.at[s], sem_ref.at[0, s], tm)
        _issue_row_gather(y_hbm, p2_ref, step * tm, buf2_ref.at[s], sem_ref.at[1, s], tm)

    @pl.when(i == 0)
    def _():
        issue(0, 0)

    @pl.when(i + 1 < n_steps)
    def _():
        issue(i + 1, 1 - slot)

    _wait_row_gather(y_hbm, buf1_ref.at[slot], sem_ref.at[0, slot], tm)
    _wait_row_gather(y_hbm, buf2_ref.at[slot], sem_ref.at[1, slot], tm)
    out = x_ref[...] + (buf1_ref[slot] + buf2_ref[slot])
    if final_norm:
        out = _rms(out, gf_ref[...])
    o_ref[...] = out


def _moe(X, g, w_router, w_gate, w_up, w_down, g_final, *, tm_tok, tm, f_chunk, final_norm):
    M, D = X.shape
    E = N_EXPERTS
    d_ff = w_down.shape[1]
    g2 = g.reshape(1, D)
    wr = jnp.zeros((D, LANES_V7X), F32).at[:, :E].set(w_router)

    gate, idx = pl.pallas_call(
        _router_body,
        grid=(M // tm_tok,),
        in_specs=[pl.BlockSpec((tm_tok, D), lambda i: (i, 0)), _const_spec((1, D)), _const_spec((D, LANES_V7X))],
        out_specs=[pl.BlockSpec((tm_tok, LANES_V7X), lambda i: (i, 0)),
                   pl.BlockSpec((tm_tok, LANES_V7X), lambda i: (i, 0))],
        out_shape=[jax.ShapeDtypeStruct((M, LANES_V7X), F32), jax.ShapeDtypeStruct((M, LANES_V7X), jnp.int32)],
        compiler_params=_cparams(("arbitrary",)),
        name="moe_router",
    )(X, g2, wr)

    i1, i2 = idx[:, 0], idx[:, 1]
    eids = jnp.arange(E, dtype=jnp.int32)
    sel = (i1[:, None] == eids) | (i2[:, None] == eids)
    seli = sel.astype(jnp.int32)
    count = jnp.sum(seli, axis=0)
    tiles_e = (count + tm - 1) // tm
    tiles_end = jnp.cumsum(tiles_e)
    n_tiles = tiles_end[-1]
    start_e = (tiles_end - tiles_e) * tm
    pos = start_e[None, :] + jnp.cumsum(seli, axis=0) - seli
    max_tiles = (2 * M + E * (tm - 1)) // tm
    n_slots = max_tiles * tm
    tok_ids = jnp.broadcast_to(jnp.arange(M, dtype=jnp.int32)[:, None], (M, E))
    scat = jnp.where(sel, pos, n_slots)
    tok_of_slot = jnp.zeros((n_slots,), jnp.int32).at[scat].set(tok_ids, mode="drop")
    w_slot = jnp.zeros((n_slots,), F32).at[scat].set(gate[:, :E], mode="drop").reshape(n_slots, 1)
    pos1 = jnp.take_along_axis(pos, i1[:, None], axis=1)[:, 0].astype(jnp.int32)
    pos2 = jnp.take_along_axis(pos, i2[:, None], axis=1)[:, 0].astype(jnp.int32)
    tile_ids = jnp.arange(max_tiles, dtype=jnp.int32)
    tile_expert = jnp.searchsorted(tiles_end, jnp.minimum(tile_ids, n_tiles - 1), side="right").astype(jnp.int32)
    tile_expert = jnp.minimum(tile_expert, E - 1)
    n_tiles_arr = n_tiles.astype(jnp.int32).reshape(1)

    wg, wu, wd = w_gate.astype(BF16), w_up.astype(BF16), w_down.astype(BF16)
    y_sorted = pl.pallas_call(
        functools.partial(_moe_ffn_body, tm=tm, f_chunk=f_chunk),
        grid_spec=pltpu.PrefetchScalarGridSpec(
            num_scalar_prefetch=3,
            grid=(max_tiles,),
            in_specs=[
                pl.BlockSpec(memory_space=pl.ANY),
                pl.BlockSpec((tm, 1), lambda t, te, nt, tok: (t, 0)),
                pl.BlockSpec((1, D), lambda t, te, nt, tok: (0, 0)),
                pl.BlockSpec((1, D, d_ff), lambda t, te, nt, tok: (te[t], 0, 0)),
                pl.BlockSpec((1, D, d_ff), lambda t, te, nt, tok: (te[t], 0, 0)),
                pl.BlockSpec((1, d_ff, D), lambda t, te, nt, tok: (te[t], 0, 0)),
            ],
            out_specs=pl.BlockSpec((tm, D), lambda t, te, nt, tok: (t, 0)),
            scratch_shapes=[pltpu.VMEM((2, tm, D), F32), pltpu.SemaphoreType.DMA((2,))],
        ),
        out_shape=jax.ShapeDtypeStruct((n_slots, D), F32),
        compiler_params=_cparams(("arbitrary",)),
        name="moe_ffn",
    )(tile_expert, n_tiles_arr, tok_of_slot, X, w_slot, g2, wg, wu, wd)

    n_steps = M // tm_tok
    X = pl.pallas_call(
        functools.partial(_moe_combine_body, tm=tm_tok, n_steps=n_steps, final_norm=final_norm),
        grid_spec=pltpu.PrefetchScalarGridSpec(
            num_scalar_prefetch=2,
            grid=(n_steps,),
            in_specs=[
                pl.BlockSpec((tm_tok, D), lambda i, p1, p2: (i, 0)),
                pl.BlockSpec(memory_space=pl.ANY),
                pl.BlockSpec((1, D), lambda i, p1, p2: (0, 0)),
            ],
            out_specs=pl.BlockSpec((tm_tok, D), lambda i, p1, p2: (i, 0)),
            scratch_shapes=[pltpu.VMEM((2, tm_tok, D), F32), pltpu.VMEM((2, tm_tok, D), F32),
                            pltpu.SemaphoreType.DMA((2, 2))],
        ),
        out_shape=jax.ShapeDtypeStruct((M, D), F32),
        input_output_aliases={2: 0},
        compiler_params=_cparams(("arbitrary",)),
        name="moe_combine",
    )(pos1, pos2, X, y_sorted, g_final.reshape(1, D))
    return X


def _ssd_post(y, xs, z, dsk, ng, wout_ref, ynorm_ref, inner):
    y = (y + dsk * xs) * _silu(z)
    gw = inner // SSM_GROUPS
    for g in range(SSM_GROUPS):
        cols = slice(g * gw, (g + 1) * gw)
        yg = y[:, cols]
        yg = yg * lax.rsqrt(jnp.mean(yg * yg, axis=-1, keepdims=True) + EPS) * ng[:, cols]
        ynorm_ref[:, cols] = yg.astype(BF16)
    return _dot(ynorm_ref[...], wout_ref[...])


def _mamba_prompt_body(x_ref, g_ref, win_ref, wdt_ref, cw_ref, cb_ref, dtb_ref, a_ref, dsk_ref, ng_ref,
                       wout_ref, r_ref, o_ref, conv_ref, ssm_ref, ctx_ref, s_ref, y_ref, ynorm_ref,
                       *, tm, tiles_per_seq, inner, bc):
    i = pl.program_id(0)

    @pl.when(i % tiles_per_seq == 0)
    def _():
        ctx_ref[0:SSM_PAD, :] = jnp.zeros((SSM_PAD, ctx_ref.shape[1]), F32)
        s_ref[...] = jnp.zeros_like(s_ref)

    x = x_ref[...]
    h = _rms(x, g_ref[...]).astype(BF16)
    proj = _dot(h, win_ref[...])
    z = proj[:, :inner]
    xbc = proj[:, inner:]
    ctx_ref[SSM_PAD:SSM_PAD + tm, :] = xbc
    conv_ref[0] = ctx_ref[SSM_PAD + tm - (SSM_CONV - 1):SSM_PAD + tm, :]
    conv = cb_ref[...] + xbc * cw_ref[SSM_CONV - 1:SSM_CONV, :]
    for k in range(SSM_CONV - 1):
        off = SSM_PAD - (SSM_CONV - 1) + k
        conv = conv + ctx_ref[off:off + tm, :] * cw_ref[k:k + 1, :]
    ctx_ref[0:SSM_PAD, :] = ctx_ref[tm:tm + SSM_PAD, :]
    act = _silu(conv)
    xs = act[:, :inner]
    bm = act[:, inner:inner + bc].astype(BF16)
    cm = act[:, inner + bc:].astype(BF16)
    dt = _softplus(_dot(h, wdt_ref[...]) + dtb_ref[...])
    da = dt * a_ref[...]

    rows_i = lax.broadcasted_iota(jnp.int32, (CHUNK, CHUNK), 0)
    cols_i = lax.broadcasted_iota(jnp.int32, (CHUNK, CHUNK), 1)
    causal = rows_i >= cols_i
    tril = causal.astype(F32)
    gw = inner // SSM_GROUPS
    hpg = gw // SSM_HEAD_DIM
    lane_head = lax.broadcasted_iota(jnp.int32, (CHUNK, gw), 1) // SSM_HEAD_DIM
    r_mat = r_ref[...]

    for c in range(tm // CHUNK):
        rows = slice(c * CHUNK, (c + 1) * CHUNK)
        dt_c = dt[rows, :]
        cum = _dot_hi(tril, da[rows, :])
        cum_t = cum.T
        dt_t = dt_c.T
        ecum = jnp.exp(cum)
        e_full = _dot_hi(ecum, r_mat)
        w_in = jnp.exp(cum[CHUNK - 1:CHUNK, :] - cum) * dt_c
        xs_c = xs[rows, :]
        xw = (xs_c * _dot_hi(w_in, r_mat)).astype(BF16)
        xs_b = xs_c.astype(BF16)
        for g in range(SSM_GROUPS):
            gcols = slice(g * gw, (g + 1) * gw)
            ncols = slice(g * SSM_STATE, (g + 1) * SSM_STATE)
            b_g = bm[rows, ncols]
            c_g = cm[rows, ncols]
            cb = _dot_nt(c_g, b_g)
            x_g = xs_b[:, gcols]
            s_g = s_ref[g]
            y_g = _dot_nt(c_g, s_g.astype(BF16)) * e_full[:, gcols]
            for r in range(hpg):
                hd = g * hpg + r
                seg = cum[:, hd:hd + 1] - cum_t[hd:hd + 1, :]
                decay = jnp.where(causal, jnp.exp(jnp.where(causal, seg, 0.0)), 0.0)
                wts = (cb * decay * dt_t[hd:hd + 1, :]).astype(BF16)
                y_g = y_g + _dot(wts, jnp.where(lane_head == r, x_g, jnp.zeros_like(x_g)))
            y_ref[rows, gcols] = y_g
            upd = _dot_tn(xw[:, gcols], b_g)
            for r in range(hpg):
                hd = g * hpg + r
                hrows = slice(r * SSM_HEAD_DIM, (r + 1) * SSM_HEAD_DIM)
                s_ref[g, hrows, :] = s_g[hrows, :] * ecum[CHUNK - 1:CHUNK, hd:hd + 1] + upd[hrows, :]

    out = _ssd_post(y_ref[...], xs, z, dsk_ref[...], ng_ref[...], wout_ref, ynorm_ref, inner)
    o_ref[...] = x + out
    ssm_ref[0] = s_ref[...]


def _mamba_sample_pre_body(x_ref, g_ref, win_ref, wdt_ref, cw_ref, cb_ref, dtb_ref, a_ref, r_ref, cs_ref,
                           z_ref, xs_ref, b_ref, c_ref, xdt_ref, e_ref, newconv_ref, *, inner, bc):
    h = _rms(x_ref[...], g_ref[...]).astype(BF16)
    proj = _dot(h, win_ref[...])
    z_ref[...] = proj[:, :inner]
    xbc = proj[:, inner:]
    conv = cb_ref[...] + xbc * cw_ref[SSM_CONV - 1:SSM_CONV, :]
    for k in range(SSM_CONV - 1):
        conv = conv + cs_ref[k] * cw_ref[k:k + 1, :]
    for k in range(SSM_CONV - 2):
        newconv_ref[k] = cs_ref[k + 1]
    newconv_ref[SSM_CONV - 2] = xbc
    act = _silu(conv)
    xs = act[:, :inner]
    xs_ref[...] = xs
    b_ref[...] = act[:, inner:inner + bc]
    c_ref[...] = act[:, inner + bc:]
    dt = _softplus(_dot(h, wdt_ref[...]) + dtb_ref[...])
    xdt_ref[...] = xs * _dot_hi(dt, r_ref[...])
    e_ref[...] = _dot_hi(jnp.exp(dt * a_ref[...]), r_ref[...])


def _mamba_sample_state_body(h0_ref, xdt_t_ref, e_t_ref, b_ref, c_ref, hn_ref, y_t_ref, *, bt, inner):
    gw = inner // SSM_GROUPS
    for j in range(bt):
        col_x = xdt_t_ref[0, :, j:j + 1]
        col_e = e_t_ref[0, :, j:j + 1]
        for g in range(SSM_GROUPS):
            rows = slice(g * gw, (g + 1) * gw)
            ncols = slice(g * SSM_STATE, (g + 1) * SSM_STATE)
            b_row = b_ref[0, j:j + 1, ncols]
            c_row = c_ref[0, j:j + 1, ncols]
            hn = h0_ref[j, rows, :] * col_e[rows, :] + col_x[rows, :] * b_row
            hn_ref[j, rows, :] = hn
            y_t_ref[0, rows, j:j + 1] = jnp.sum(hn * c_row, axis=-1, keepdims=True)


def _mamba_sample_post_body(x_ref, y_ref, xs_ref, z_ref, dsk_ref, ng_ref, wout_ref, o_ref, ynorm_ref, *, inner):
    out = _ssd_post(y_ref[...], xs_ref[...], z_ref[...], dsk_ref[...], ng_ref[...], wout_ref, ynorm_ref, inner)
    o_ref[...] = x_ref[...] + out


def _mamba(X, n_prompt, n_batch, state_ssm, state_conv, g, w_in, conv_w, conv_b, dt_bias, a_log, d_skip,
           norm_g, w_out, *, tm, bt):
    M, D = X.shape
    inner = w_out.shape[0]
    heads = a_log.shape[0]
    conv_dim = conv_w.shape[1]
    bc = (conv_dim - inner) // 2
    n_sample = M - n_prompt
    seq = n_prompt // n_batch
    tiles_per_seq = seq // tm
    gw = inner // SSM_GROUPS
    L = LANES_V7X

    g2 = g.reshape(1, D)
    w_main = w_in[:, :inner + conv_dim].astype(BF16)
    w_dt = jnp.zeros((D, L), F32).at[:, :heads].set(w_in[:, inner + conv_dim:]).astype(BF16)
    cb2 = conv_b.reshape(1, conv_dim)
    dtb = jnp.zeros((1, L), F32).at[0, :heads].set(dt_bias)
    a_neg = jnp.zeros((1, L), F32).at[0, :heads].set(-jnp.exp(a_log))
    dsk = jnp.repeat(d_skip, SSM_HEAD_DIM).reshape(1, inner)
    ng2 = norm_g.reshape(1, inner)
    w_out_b = w_out.astype(BF16)
    r_mat = (jnp.arange(L, dtype=jnp.int32)[:, None]
             == (jnp.arange(inner, dtype=jnp.int32) // SSM_HEAD_DIM)[None, :]).astype(F32)

    X, conv_p, ssm_p = pl.pallas_call(
        functools.partial(_mamba_prompt_body, tm=tm, tiles_per_seq=tiles_per_seq, inner=inner, bc=bc),
        grid=(n_prompt // tm,),
        in_specs=[
            pl.BlockSpec((tm, D), lambda i: (i, 0)),
            _const_spec((1, D)),
            _const_spec((D, inner + conv_dim)),
            _const_spec((D, L)),
            _const_spec((SSM_CONV, conv_dim)),
            _const_spec((1, conv_dim)),
            _const_spec((1, L)),
            _const_spec((1, L)),
            _const_spec((1, inner)),
            _const_spec((1, inner)),
            _const_spec((inner, D)),
            _const_spec((L, inner)),
        ],
        out_specs=[
            pl.BlockSpec((tm, D), lambda i: (i, 0)),
            pl.BlockSpec((1, SSM_CONV - 1, conv_dim), lambda i: (i // tiles_per_seq, 0, 0)),
            pl.BlockSpec((1, SSM_GROUPS, gw, SSM_STATE), lambda i: (i // tiles_per_seq, 0, 0, 0)),
        ],
        out_shape=[
            jax.ShapeDtypeStruct((M, D), F32),
            jax.ShapeDtypeStruct((n_batch, SSM_CONV - 1, conv_dim), F32),
            jax.ShapeDtypeStruct((n_batch, SSM_GROUPS, gw, SSM_STATE), F32),
        ],
        scratch_shapes=[
            pltpu.VMEM((SSM_PAD + tm, conv_dim), F32),
            pltpu.VMEM((SSM_GROUPS, gw, SSM_STATE), F32),
            pltpu.VMEM((tm, inner), F32),
            pltpu.VMEM((tm, inner), BF16),
        ],
        input_output_aliases={0: 0},
        compiler_params=_cparams(("arbitrary",)),
        name="mamba_prompt",
    )(X, g2, w_main, w_dt, conv_w, cb2, dtb, a_neg, dsk, ng2, w_out_b, r_mat)

    sblk = n_prompt // n_sample
    cs_t = jnp.transpose(state_conv, (1, 0, 2))
    z, xs, b_m, c_m, xdt, e_full, newconv_t = pl.pallas_call(
        functools.partial(_mamba_sample_pre_body, inner=inner, bc=bc),
        grid=(1,),
        in_specs=[
            pl.BlockSpec((n_sample, D), lambda i: (sblk, 0)),
            _const_spec((1, D)),
            _const_spec((D, inner + conv_dim)),
            _const_spec((D, L)),
            _const_spec((SSM_CONV, conv_dim)),
            _const_spec((1, conv_dim)),
            _const_spec((1, L)),
            _const_spec((1, L)),
            _const_spec((L, inner)),
            _const_spec((SSM_CONV - 1, n_sample, conv_dim)),
        ],
        out_specs=[
            pl.BlockSpec((n_sample, inner), lambda i: (0, 0)),
            pl.BlockSpec((n_sample, inner), lambda i: (0, 0)),
            pl.BlockSpec((n_sample, bc), lambda i: (0, 0)),
            pl.BlockSpec((n_sample, bc), lambda i: (0, 0)),
            pl.BlockSpec((n_sample, inner), lambda i: (0, 0)),
            pl.BlockSpec((n_sample, inner), lambda i: (0, 0)),
            pl.BlockSpec((SSM_CONV - 1, n_sample, conv_dim), lambda i: (0, 0, 0)),
        ],
        out_shape=[
            jax.ShapeDtypeStruct((n_sample, inner), F32),
            jax.ShapeDtypeStruct((n_sample, inner), F32),
            jax.ShapeDtypeStruct((n_sample, bc), F32),
            jax.ShapeDtypeStruct((n_sample, bc), F32),
            jax.ShapeDtypeStruct((n_sample, inner), F32),
            jax.ShapeDtypeStruct((n_sample, inner), F32),
            jax.ShapeDtypeStruct((SSM_CONV - 1, n_sample, conv_dim), F32),
        ],
        compiler_params=_cparams(("arbitrary",)),
        name="mamba_sample_pre",
    )(X, g2, w_main, w_dt, conv_w, cb2, dtb, a_neg, r_mat, cs_t)
    conv_s = jnp.transpose(newconv_t, (1, 0, 2))

    nblk = n_sample // bt

    def to_cols(arr):
        return jnp.transpose(arr.reshape(nblk, bt, inner), (0, 2, 1))

    h0 = state_ssm.reshape(n_sample, inner, SSM_STATE)
    h_new, y_t = pl.pallas_call(
        functools.partial(_mamba_sample_state_body, bt=bt, inner=inner),
        grid=(nblk,),
        in_specs=[
            pl.BlockSpec((bt, inner, SSM_STATE), lambda i: (i, 0, 0)),
            pl.BlockSpec((1, inner, bt), lambda i: (i, 0, 0)),
            pl.BlockSpec((1, inner, bt), lambda i: (i, 0, 0)),
            pl.BlockSpec((1, bt, bc), lambda i: (i, 0, 0)),
            pl.BlockSpec((1, bt, bc), lambda i: (i, 0, 0)),
        ],
        out_specs=[
            pl.BlockSpec((bt, inner, SSM_STATE), lambda i: (i, 0, 0)),
            pl.BlockSpec((1, inner, bt), lambda i: (i, 0, 0)),
        ],
        out_shape=[
            jax.ShapeDtypeStruct((n_sample, inner, SSM_STATE), F32),
            jax.ShapeDtypeStruct((nblk, inner, bt), F32),
        ],
        compiler_params=_cparams(("arbitrary",)),
        name="mamba_sample_state",
    )(h0, to_cols(xdt), to_cols(e_full), b_m.reshape(nblk, bt, bc), c_m.reshape(nblk, bt, bc))
    y_s = jnp.transpose(y_t, (0, 2, 1)).reshape(n_sample, inner)

    X = pl.pallas_call(
        functools.partial(_mamba_sample_post_body, inner=inner),
        grid=(1,),
        in_specs=[
            pl.BlockSpec((n_sample, D), lambda i: (sblk, 0)),
            pl.BlockSpec((n_sample, inner), lambda i: (0, 0)),
            pl.BlockSpec((n_sample, inner), lambda i: (0, 0)),
            pl.BlockSpec((n_sample, inner), lambda i: (0, 0)),
            _const_spec((1, inner)),
            _const_spec((1, inner)),
            _const_spec((inner, D)),
        ],
        out_specs=pl.BlockSpec((n_sample, D), lambda i: (sblk, 0)),
        out_shape=jax.ShapeDtypeStruct((M, D), F32),
        scratch_shapes=[pltpu.VMEM((n_sample, inner), BF16)],
        input_output_aliases={0: 0},
        compiler_params=_cparams(("arbitrary",)),
        name="mamba_sample_post",
    )(X, y_s, xs, z, dsk, ng2, w_out_b)

    ssm_p = ssm_p.reshape(n_batch, heads, SSM_HEAD_DIM, SSM_STATE)
    ssm_s = h_new.reshape(n_sample, heads, SSM_HEAD_DIM, SSM_STATE)
    return X, conv_p, conv_s, ssm_p, ssm_s


def _cfm_prompt_body(x_ref, g_ref, w1_ref, b1_ref, dw_ref, dwb_ref, lng_ref, lnb_ref, w2_ref, b2_ref,
                     o_ref, buf_ref, ctx_ref, *, tm, tiles_per_seq, d):
    i = pl.program_id(0)

    @pl.when(i % tiles_per_seq == 0)
    def _():
        ctx_ref[0:CFM_PAD, :] = jnp.zeros((CFM_PAD, d), F32)

    x = x_ref[...]
    h = _rms(x, g_ref[...]).astype(BF16)
    a = _dot(h, w1_ref[...]) + b1_ref[...]
    glu = a[:, :d] * _sigmoid(a[:, d:])
    ctx_ref[CFM_PAD:CFM_PAD + tm, :] = glu
    acc = dwb_ref[...] + glu * dw_ref[CFM_KERNEL - 1:CFM_KERNEL, :]
    for k in range(CFM_KERNEL - 1):
        off = CFM_PAD - (CFM_KERNEL - 1) + k
        acc = acc + ctx_ref[off:off + tm, :] * dw_ref[k:k + 1, :]
    buf_ref[0] = ctx_ref[CFM_PAD + tm - (CFM_KERNEL - 1):CFM_PAD + tm, :]
    ctx_ref[0:CFM_PAD, :] = ctx_ref[tm:tm + CFM_PAD, :]
    hc = _silu(_layernorm(acc, lng_ref[...], lnb_ref[...])).astype(BF16)
    o_ref[...] = x + _dot(hc, w2_ref[...]) + b2_ref[...]


def _cfm_sample_body(x_ref, g_ref, w1_ref, b1_ref, dw_ref, dwb_ref, lng_ref, lnb_ref, w2_ref, b2_ref, cs_ref,
                     o_ref, new_ref, *, d):
    x = x_ref[...]
    h = _rms(x, g_ref[...]).astype(BF16)
    a = _dot(h, w1_ref[...]) + b1_ref[...]
    glu = a[:, :d] * _sigmoid(a[:, d:])
    acc = dwb_ref[...] + glu * dw_ref[CFM_KERNEL - 1:CFM_KERNEL, :]
    for k in range(CFM_KERNEL - 1):
        acc = acc + cs_ref[k] * dw_ref[k:k + 1, :]
    for k in range(CFM_KERNEL - 2):
        new_ref[k] = cs_ref[k + 1]
    new_ref[CFM_KERNEL - 2] = glu
    hc = _silu(_layernorm(acc, lng_ref[...], lnb_ref[...])).astype(BF16)
    o_ref[...] = x + _dot(hc, w2_ref[...]) + b2_ref[...]


def _conformer(X, n_prompt, n_batch, state_conv, g, w_pw1, b_pw1, dw_w, dw_b, ln_g, ln_b, w_pw2, b_pw2,
               *, tm, bt):
    M, D = X.shape
    n_sample = M - n_prompt
    seq = n_prompt // n_batch
    tiles_per_seq = seq // tm
    K = CFM_KERNEL
    g2 = g.reshape(1, D)
    w1, w2 = w_pw1.astype(BF16), w_pw2.astype(BF16)
    b1, b2 = b_pw1.reshape(1, 2 * D), b_pw2.reshape(1, D)
    dwb, lng, lnb = dw_b.reshape(1, D), ln_g.reshape(1, D), ln_b.reshape(1, D)
    weight_specs = [
        _const_spec((1, D)), _const_spec((D, 2 * D)), _const_spec((1, 2 * D)), _const_spec((K, D)),
        _const_spec((1, D)), _const_spec((1, D)), _const_spec((1, D)), _const_spec((D, D)), _const_spec((1, D)),
    ]
    weights = (g2, w1, b1, dw_w, dwb, lng, lnb, w2, b2)

    X, buf_p = pl.pallas_call(
        functools.partial(_cfm_prompt_body, tm=tm, tiles_per_seq=tiles_per_seq, d=D),
        grid=(n_prompt // tm,),
        in_specs=[pl.BlockSpec((tm, D), lambda i: (i, 0))] + weight_specs,
        out_specs=[
            pl.BlockSpec((tm, D), lambda i: (i, 0)),
            pl.BlockSpec((1, K - 1, D), lambda i: (i // tiles_per_seq, 0, 0)),
        ],
        out_shape=[jax.ShapeDtypeStruct((M, D), F32), jax.ShapeDtypeStruct((n_batch, K - 1, D), F32)],
        scratch_shapes=[pltpu.VMEM((CFM_PAD + tm, D), F32)],
        input_output_aliases={0: 0},
        compiler_params=_cparams(("arbitrary",)),
        name="conformer_prompt",
    )(X, *weights)

    first = n_prompt // bt
    cs_t = jnp.transpose(state_conv, (1, 0, 2))
    X, new_t = pl.pallas_call(
        functools.partial(_cfm_sample_body, d=D),
        grid=(n_sample // bt,),
        in_specs=[pl.BlockSpec((bt, D), lambda i: (i + first, 0))] + weight_specs
        + [pl.BlockSpec((K - 1, bt, D), lambda i: (0, i, 0))],
        out_specs=[
            pl.BlockSpec((bt, D), lambda i: (i + first, 0)),
            pl.BlockSpec((K - 1, bt, D), lambda i: (0, i, 0)),
        ],
        out_shape=[jax.ShapeDtypeStruct((M, D), F32), jax.ShapeDtypeStruct((K - 1, n_sample, D), F32)],
        input_output_aliases={0: 0},
        compiler_params=_cparams(("arbitrary",)),
        name="conformer_sample",
    )(X, *weights, cs_t)
    return X, buf_p, jnp.transpose(new_t, (1, 0, 2))


def kernel(x_prompt, x_sample, state_ssm, state_conv_ssm, state_conv_cfm, norm_mix_g, norm_ffn_g, norm_final_g, a_w_in, a_b_in, a_ln_g, a_ln_b, a_w_s, a_b_s, a_w_out, b_w_in, b_conv_w, b_conv_b, b_dt_bias, b_a_log, b_d, b_norm_g, b_w_out, c_w_pw1, c_b_pw1, c_dw_w, c_dw_b, c_ln_g, c_ln_b, c_w_pw2, c_b_pw2, f_w_gate, f_w_up, f_w_down, e_w_router, e_w_gate, e_w_up, e_w_down):
    n_batch, seq, D = x_prompt.shape
    n_sample = x_sample.shape[0]
    n_prompt = n_batch * seq
    depth = norm_mix_g.shape[0]
    d_ff = f_w_gate.shape[2]
    assert x_sample.shape[1] == 1 and n_prompt % n_sample == 0 and seq % CHUNK == 0

    X = jnp.concatenate([x_prompt.reshape(n_prompt, D), x_sample.reshape(n_sample, D)], axis=0)
    v_p, v_s, ssm_p, ssm_s, cs_p, cs_s, cc_p, cc_s = [], [], [], [], [], [], [], []
    for i in range(depth):
        kind, j = i % 3, i // 3
        if kind == 0:
            X, vp, vs = _gmlp(X, n_prompt, n_batch, norm_mix_g[i], a_w_in[j], a_b_in[j], a_ln_g[j], a_ln_b[j],
                              a_w_s[j], a_b_s[j], a_w_out[j], tm=512)
            v_p.append(vp)
            v_s.append(vs)
        elif kind == 1:
            X, cbp, cbs, hlp, hls = _mamba(X, n_prompt, n_batch, state_ssm[j], state_conv_ssm[j], norm_mix_g[i],
                                           b_w_in[j], b_conv_w[j], b_conv_b[j], b_dt_bias[j], b_a_log[j], b_d[j],
                                           b_norm_g[j], b_w_out[j], tm=256, bt=4)
            cs_p.append(cbp)
            cs_s.append(cbs)
            ssm_p.append(hlp)
            ssm_s.append(hls)
        else:
            X, cbp, cbs = _conformer(X, n_prompt, n_batch, state_conv_cfm[j], norm_mix_g[i], c_w_pw1[j], c_b_pw1[j],
                                     c_dw_w[j], c_dw_b[j], c_ln_g[j], c_ln_b[j], c_w_pw2[j], c_b_pw2[j],
                                     tm=512, bt=32)
            cc_p.append(cbp)
            cc_s.append(cbs)
        k = i // 2
        if i % 2 == 0:
            X = _ffn_dense(X, n_prompt, norm_ffn_g[i], f_w_gate[k], f_w_up[k], f_w_down[k],
                           tm=512, f_chunk=d_ff // 2)
        else:
            X = _moe(X, norm_ffn_g[i], e_w_router[k], e_w_gate[k], e_w_up[k], e_w_down[k], norm_final_g,
                     tm_tok=384, tm=512, f_chunk=d_ff // 2, final_norm=(i == depth - 1))
    assert depth % 2 == 0
    y_prompt = X[:n_prompt].reshape(n_batch, seq, D)
    y_sample = X[n_prompt:].reshape(n_sample, 1, D)
    return (y_prompt, y_sample, jnp.stack(v_p), jnp.stack(v_s), jnp.stack(ssm_p), jnp.stack(ssm_s),
            jnp.stack(cs_p), jnp.stack(cs_s), jnp.stack(cc_p), jnp.stack(cc_s))
```

```python
import functools

import jax
import jax.numpy as jnp
from jax import lax
from jax.experimental import pallas as pl
from jax.experimental.pallas import tpu as pltpu

F32 = jnp.float32
BF16 = jnp.bfloat16
EPS = 1e-6
HIGHEST = lax.Precision.HIGHEST

LANES_V7X = 128
SUBLANES_V7X = 8
VMEM_LIMIT_V7X = 60 * 1024 * 1024

CHUNK = 128
GM_GROUPS = 8
SSM_GROUPS = 8
SSM_HEAD_DIM = 64
SSM_STATE = 128
SSM_CONV = 4
CFM_KERNEL = 31
N_EXPERTS = 8
CFM_PAD = 32
SSM_PAD = 8
CONV_ROW_BLOCK = 128
CONV_LANE_BLOCK = 256


def _cparams(sem):
    return pltpu.CompilerParams(dimension_semantics=sem, vmem_limit_bytes=VMEM_LIMIT_V7X)


def _const_spec(shape):
    nd = len(shape)
    return pl.BlockSpec(shape, lambda *_: (0,) * nd, pipeline_mode=pl.Buffered(1))


def _dot(a, b):
    return jnp.dot(a, b, preferred_element_type=F32)


def _dot_nt(a, b):
    return lax.dot_general(a, b, (((1,), (1,)), ((), ())), preferred_element_type=F32)


def _dot_tn(a, b):
    return lax.dot_general(a, b, (((0,), (0,)), ((), ())), preferred_element_type=F32)


def _dot_hi(a, b):
    return jnp.dot(a, b, precision=HIGHEST, preferred_element_type=F32)


def _rms(x, g):
    return x * lax.rsqrt(jnp.mean(x * x, axis=-1, keepdims=True) + EPS) * g


def _layernorm(x, g, b):
    xc = x - jnp.mean(x, axis=-1, keepdims=True)
    return xc * lax.rsqrt(jnp.mean(xc * xc, axis=-1, keepdims=True) + EPS) * g + b


def _sigmoid(x):
    return 1.0 / (1.0 + jnp.exp(-x))


def _silu(x):
    return x * _sigmoid(x)


def _gelu(x):
    return 0.5 * x * (1.0 + lax.erf(x * (2.0 ** -0.5)))


def _softplus(x):
    return jnp.maximum(x, 0.0) + jnp.log(1.0 + jnp.exp(-jnp.abs(x)))


def _gmlp_prompt_body(x_ref, g_ref, win_ref, bin_ref, lng_ref, lnb_ref, ws_ref, bs_ref, wout_ref,
                      o_ref, v_ref, gated_ref, *, tm, width):
    x = x_ref[...]
    h = _rms(x, g_ref[...]).astype(BF16)
    u = _gelu(_dot(h, win_ref[:, :width]) + bin_ref[:, :width])
    hv = _gelu(_dot(h, win_ref[:, width:]) + bin_ref[:, width:])
    v = _layernorm(hv, lng_ref[...], lnb_ref[...])
    v_ref[0] = v[tm - CHUNK:, :]
    vb = v.astype(BF16)
    gdim = width // GM_GROUPS
    causal = (lax.broadcasted_iota(jnp.int32, (CHUNK, CHUNK), 0)
              >= lax.broadcasted_iota(jnp.int32, (CHUNK, CHUNK), 1))
    for g in range(GM_GROUPS):
        wc = jnp.where(causal, ws_ref[g], 0.0).astype(BF16)
        bias = bs_ref[:, g:g + 1]
        for c in range(tm // CHUNK):
            rows = slice(c * CHUNK, (c + 1) * CHUNK)
            cols = slice(g * gdim, (g + 1) * gdim)
            s = _dot(wc, vb[rows, cols]) + bias
            gated_ref[rows, cols] = (u[rows, cols] * s).astype(BF16)
    o_ref[...] = x + _dot(gated_ref[...], wout_ref[...])


def _gmlp_sample_body(x_ref, g_ref, win_ref, bin_ref, lng_ref, lnb_ref, ws0_ref, bs0_ref, wout_ref,
                      o_ref, v_ref, *, width):
    x = x_ref[...]
    h = _rms(x, g_ref[...]).astype(BF16)
    u = _gelu(_dot(h, win_ref[:, :width]) + bin_ref[:, :width])
    hv = _gelu(_dot(h, win_ref[:, width:]) + bin_ref[:, width:])
    v = _layernorm(hv, lng_ref[...], lnb_ref[...])
    v_ref[...] = v
    s = v * ws0_ref[...] + bs0_ref[...]
    o_ref[...] = x + _dot((u * s).astype(BF16), wout_ref[...])


def _gmlp(X, n_prompt, n_batch, g, w_in, b_in, ln_g, ln_b, w_s, b_s, w_out, *, tm):
    M, D = X.shape
    width = w_out.shape[0]
    seq = n_prompt // n_batch
    tiles_per_seq = seq // tm
    n_sample = M - n_prompt
    g2 = g.reshape(1, D)
    b_in2 = b_in.reshape(1, 2 * width)
    ln_g2, ln_b2 = ln_g.reshape(1, width), ln_b.reshape(1, width)
    w_in_b, w_out_b = w_in.astype(BF16), w_out.astype(BF16)

    X, v_p = pl.pallas_call(
        functools.partial(_gmlp_prompt_body, tm=tm, width=width),
        grid=(n_prompt // tm,),
        in_specs=[
            pl.BlockSpec((tm, D), lambda i: (i, 0)),
            _const_spec((1, D)),
            _const_spec((D, 2 * width)),
            _const_spec((1, 2 * width)),
            _const_spec((1, width)),
            _const_spec((1, width)),
            _const_spec((GM_GROUPS, CHUNK, CHUNK)),
            _const_spec((CHUNK, GM_GROUPS)),
            _const_spec((width, D)),
        ],
        out_specs=[
            pl.BlockSpec((tm, D), lambda i: (i, 0)),
            pl.BlockSpec((1, CHUNK, width), lambda i: (i // tiles_per_seq, 0, 0)),
        ],
        out_shape=[jax.ShapeDtypeStruct((M, D), F32),
                   jax.ShapeDtypeStruct((n_batch, CHUNK, width), F32)],
        scratch_shapes=[pltpu.VMEM((tm, width), BF16)],
        input_output_aliases={0: 0},
        compiler_params=_cparams(("arbitrary",)),
        name="gmlp_prompt",
    )(X, g2, w_in_b, b_in2, ln_g2, ln_b2, w_s, b_s.T, w_out_b)

    gdim = width // GM_GROUPS
    ws0 = jnp.repeat(w_s[:, 0, 0], gdim).reshape(1, width)
    bs0 = jnp.repeat(b_s[:, 0], gdim).reshape(1, width)
    sblk = n_prompt // n_sample
    X, v_s = pl.pallas_call(
        functools.partial(_gmlp_sample_body, width=width),
        grid=(1,),
        in_specs=[
            pl.BlockSpec((n_sample, D), lambda i: (sblk, 0)),
            _const_spec((1, D)),
            _const_spec((D, 2 * width)),
            _const_spec((1, 2 * width)),
            _const_spec((1, width)),
            _const_spec((1, width)),
            _const_spec((1, width)),
            _const_spec((1, width)),
            _const_spec((width, D)),
        ],
        out_specs=[
            pl.BlockSpec((n_sample, D), lambda i: (sblk, 0)),
            pl.BlockSpec((n_sample, width), lambda i: (0, 0)),
        ],
        out_shape=[jax.ShapeDtypeStruct((M, D), F32),
                   jax.ShapeDtypeStruct((n_sample, width), F32)],
        input_output_aliases={0: 0},
        compiler_params=_cparams(("arbitrary",)),
        name="gmlp_sample",
    )(X, g2, w_in_b, b_in2, ln_g2, ln_b2, ws0, bs0, w_out_b)
    return X, v_p, v_s.reshape(n_sample, 1, width)


def _swiglu_tile(h, wg_ref, wu_ref, wd_ref, f_chunk):
    d_ff = wd_ref.shape[0]
    acc = None
    for f in range(d_ff // f_chunk):
        cols = slice(f * f_chunk, (f + 1) * f_chunk)
        a = _dot(h, wg_ref[:, cols])
        b = _dot(h, wu_ref[:, cols])
        part = _dot((_silu(a) * b).astype(BF16), wd_ref[cols, :])
        acc = part if acc is None else acc + part
    return acc


def _ffn_dense_body(x_ref, g_ref, wg_ref, wu_ref, wd_ref, o_ref, *, f_chunk):
    x = x_ref[...]
    h = _rms(x, g_ref[...]).astype(BF16)
    o_ref[...] = x + _swiglu_tile(h, wg_ref, wu_ref, wd_ref, f_chunk)


def _ffn_dense_rows(X, g2, wg, wu, wd, *, tm, first_block, n_tiles, f_chunk):
    M, D = X.shape
    d_ff = wd.shape[0]
    return pl.pallas_call(
        functools.partial(_ffn_dense_body, f_chunk=f_chunk),
        grid=(n_tiles,),
        in_specs=[
            pl.BlockSpec((tm, D), lambda i: (i + first_block, 0)),
            _const_spec((1, D)),
            _const_spec((D, d_ff)),
            _const_spec((D, d_ff)),
            _const_spec((d_ff, D)),
        ],
        out_specs=pl.BlockSpec((tm, D), lambda i: (i + first_block, 0)),
        out_shape=jax.ShapeDtypeStruct((M, D), F32),
        input_output_aliases={0: 0},
        compiler_params=_cparams(("arbitrary",)),
        name="ffn_dense",
    )(X, g2, wg, wu, wd)


def _ffn_dense(X, n_prompt, g, w_gate, w_up, w_down, *, tm, f_chunk):
    M, D = X.shape
    n_sample = M - n_prompt
    g2 = g.reshape(1, D)
    wg, wu, wd = w_gate.astype(BF16), w_up.astype(BF16), w_down.astype(BF16)
    X = _ffn_dense_rows(X, g2, wg, wu, wd, tm=tm, first_block=0, n_tiles=n_prompt // tm, f_chunk=f_chunk)
    X = _ffn_dense_rows(X, g2, wg, wu, wd, tm=n_sample, first_block=n_prompt // n_sample, n_tiles=1,
                        f_chunk=f_chunk)
    return X


def _router_body(x_ref, g_ref, wr_ref, idx_ref, wts_ref):
    h = _rms(x_ref[...], g_ref[...])
    logits = _dot_hi(h, wr_ref[...])
    lane = lax.broadcasted_iota(jnp.int32, logits.shape, 1)
    neg = jnp.float32(-jnp.inf)
    logits = jnp.where(lane < N_EXPERTS, logits, neg)
    m1 = jnp.max(logits, axis=-1, keepdims=True)
    i1 = jnp.min(jnp.where(logits == m1, lane, LANES_V7X), axis=-1, keepdims=True)
    rest = jnp.where(lane == i1, neg, logits)
    m2 = jnp.max(rest, axis=-1, keepdims=True)
    i2 = jnp.min(jnp.where(rest == m2, lane, LANES_V7X), axis=-1, keepdims=True)
    e = jnp.exp(m2 - m1)
    w1 = 1.0 / (1.0 + e)
    w2 = e / (1.0 + e)
    idx_ref[...] = jnp.where(lane == 0, i1, jnp.where(lane == 1, i2, 0))
    wts_ref[...] = jnp.where(lane == 0, w1, jnp.where(lane == 1, w2, 0.0))


def _slot_table_body(pos1_ref, pos2_ref, pad_lo_ref, pad_hi_ref, tok_ref, *, n_tokens, n_ranges):
    def pad(s, carry):
        tok_ref[s] = 0
        return carry

    for r in range(n_ranges):
        lax.fori_loop(pad_lo_ref[r], pad_hi_ref[r], pad, 0)

    def place(t, carry):
        tok_ref[pos1_ref[t]] = t
        tok_ref[pos2_ref[t]] = t
        return carry

    lax.fori_loop(0, n_tokens, place, 0, unroll=8)


def _row_copy(src_hbm, row, buf, r, sem):
    return pltpu.make_async_copy(src_hbm.at[pl.ds(row, 1), :], buf.at[pl.ds(r, 1), :], sem)


def _issue_row_gather(src_hbm, idx_ref, base, buf, sem, n_rows):
    def body(r, carry):
        _row_copy(src_hbm, idx_ref[base + r], buf, r, sem).start()
        return carry
    lax.fori_loop(0, n_rows, body, 0, unroll=8)


def _wait_row_gather(src_hbm, buf, sem, n_rows):
    pltpu.make_async_copy(src_hbm.at[pl.ds(0, n_rows), :], buf, sem).wait()


def _moe_ffn_body(tile_expert_ref, n_tiles_ref, tok_ref, x_hbm, g_ref, wg_ref, wu_ref, wd_ref,
                  y_ref, buf_ref, h_ref, sem_ref, *, tm, f_chunk):
    t = pl.program_id(0)
    n_tiles = n_tiles_ref[0]
    slot = t % 2

    @pl.when(t == 0)
    def _():
        _issue_row_gather(x_hbm, tok_ref, 0, buf_ref.at[0], sem_ref.at[0], tm)

    @pl.when(t < n_tiles)
    def _():
        _wait_row_gather(x_hbm, buf_ref.at[slot], sem_ref.at[slot], tm)
        h_ref[...] = _rms(buf_ref[slot], g_ref[...]).astype(BF16)
        base = jnp.minimum(t + 1, n_tiles - 1) * tm
        for r in range(tm):
            _row_copy(x_hbm, tok_ref[base + r], buf_ref.at[1 - slot], r, sem_ref.at[1 - slot]).start()
        y_ref[...] = _swiglu_tile(h_ref[...], wg_ref.at[0], wu_ref.at[0], wd_ref.at[0], f_chunk)

    @pl.when(t == n_tiles - 1)
    def _():
        _wait_row_gather(x_hbm, buf_ref.at[1 - slot], sem_ref.at[1 - slot], tm)

    @pl.when(t >= n_tiles)
    def _():
        y_ref[...] = jnp.zeros_like(y_ref)


def _moe_combine_body(p1_ref, p2_ref, x_ref, wts_ref, y_hbm, gf_ref, o_ref, buf1_ref, buf2_ref, sem_ref,
                      *, tm, n_steps, final_norm):
    i = pl.program_id(0)
    slot = i % 2

    def issue(step, s):
        _issue_row_gather(y_hbm, p1_ref, step * tm, buf1_ref.at[s], sem_ref.at[0, s], tm)
        _issue_row_gather(y_hbm, p2_ref, step * tm, buf2_ref.at[s], sem_ref.at[1, s], tm)

    @pl.when(i == 0)
    def _():
        issue(0, 0)

    @pl.when(i + 1 < n_steps)
    def _():
        issue(i + 1, 1 - slot)

    _wait_row_gather(y_hbm, buf1_ref.at[slot], sem_ref.at[0, slot], tm)
    _wait_row_gather(y_hbm, buf2_ref.at[slot], sem_ref.at[1, slot], tm)
    w1 = wts_ref[:, 0:1]
    w2 = wts_ref[:, 1:2]
    out = x_ref[...] + (w1 * buf1_ref[slot] + w2 * buf2_ref[slot])
    if final_norm:
        out = _rms(out, gf_ref[...])
    o_ref[...] = out


def _moe(X, g, w_router, w_gate, w_up, w_down, g_final, *, tm_tok, tm, f_chunk, final_norm):
    M, D = X.shape
    E = N_EXPERTS
    d_ff = w_down.shape[1]
    g2 = g.reshape(1, D)
    wr = jnp.zeros((D, LANES_V7X), F32).at[:, :E].set(w_router)

    L = LANES_V7X
    idx, wts = pl.pallas_call(
        _router_body,
        grid=(M // tm_tok,),
        in_specs=[pl.BlockSpec((tm_tok, D), lambda i: (i, 0)), _const_spec((1, D)), _const_spec((D, L))],
        out_specs=[pl.BlockSpec((tm_tok, L), lambda i: (i, 0)), pl.BlockSpec((tm_tok, L), lambda i: (i, 0))],
        out_shape=[jax.ShapeDtypeStruct((M, L), jnp.int32), jax.ShapeDtypeStruct((M, L), F32)],
        compiler_params=_cparams(("arbitrary",)),
        name="moe_router",
    )(X, g2, wr)

    i1, i2 = idx[:, 0], idx[:, 1]
    eids = jnp.arange(E, dtype=jnp.int32)
    sel = ((i1[:, None] == eids) | (i2[:, None] == eids)).astype(jnp.int32)
    count = jnp.sum(sel, axis=0)
    tiles_e = (count + tm - 1) // tm
    tiles_end = jnp.cumsum(tiles_e)
    n_tiles = tiles_end[-1]
    start_e = (tiles_end - tiles_e) * tm
    pos = start_e[None, :] + jnp.cumsum(sel, axis=0) - sel
    pos1 = jnp.sum(jnp.where(i1[:, None] == eids, pos, 0), axis=1).astype(jnp.int32)
    pos2 = jnp.sum(jnp.where(i2[:, None] == eids, pos, 0), axis=1).astype(jnp.int32)
    max_tiles = (2 * M + E * (tm - 1)) // tm
    n_slots = max_tiles * tm
    pad_lo = jnp.concatenate([start_e + count, (n_tiles * tm)[None]]).astype(jnp.int32)
    pad_hi = jnp.concatenate([tiles_end * tm, jnp.array([n_slots], jnp.int32)]).astype(jnp.int32)
    tile_ids = jnp.arange(max_tiles, dtype=jnp.int32)
    tile_expert = jnp.sum((jnp.minimum(tile_ids, n_tiles - 1)[:, None] >= tiles_end[None, :]).astype(jnp.int32),
                          axis=1)
    tile_expert = jnp.minimum(tile_expert, E - 1).astype(jnp.int32)
    n_tiles_arr = n_tiles.astype(jnp.int32).reshape(1)

    smem = pl.BlockSpec(memory_space=pltpu.SMEM)
    tok_of_slot = pl.pallas_call(
        functools.partial(_slot_table_body, n_tokens=M, n_ranges=E + 1),
        in_specs=[smem, smem, smem, smem],
        out_specs=smem,
        out_shape=jax.ShapeDtypeStruct((n_slots,), jnp.int32),
        name="moe_slot_table",
    )(pos1, pos2, pad_lo, pad_hi)

    wg, wu, wd = w_gate.astype(BF16), w_up.astype(BF16), w_down.astype(BF16)
    y_sorted = pl.pallas_call(
        functools.partial(_moe_ffn_body, tm=tm, f_chunk=f_chunk),
        grid_spec=pltpu.PrefetchScalarGridSpec(
            num_scalar_prefetch=3,
            grid=(max_tiles,),
            in_specs=[
                pl.BlockSpec(memory_space=pl.ANY),
                pl.BlockSpec((1, D), lambda t, te, nt, tok: (0, 0)),
                pl.BlockSpec((1, D, d_ff), lambda t, te, nt, tok: (te[t], 0, 0)),
                pl.BlockSpec((1, D, d_ff), lambda t, te, nt, tok: (te[t], 0, 0)),
                pl.BlockSpec((1, d_ff, D), lambda t, te, nt, tok: (te[t], 0, 0)),
            ],
            out_specs=pl.BlockSpec((tm, D), lambda t, te, nt, tok: (t, 0)),
            scratch_shapes=[pltpu.VMEM((2, tm, D), F32), pltpu.VMEM((tm, D), BF16),
                            pltpu.SemaphoreType.DMA((2,))],
        ),
        out_shape=jax.ShapeDtypeStruct((n_slots, D), F32),
        compiler_params=_cparams(("arbitrary",)),
        name="moe_ffn",
    )(tile_expert, n_tiles_arr, tok_of_slot, X, g2, wg, wu, wd)

    n_steps = M // tm_tok
    X = pl.pallas_call(
        functools.partial(_moe_combine_body, tm=tm_tok, n_steps=n_steps, final_norm=final_norm),
        grid_spec=pltpu.PrefetchScalarGridSpec(
            num_scalar_prefetch=2,
            grid=(n_steps,),
            in_specs=[
                pl.BlockSpec((tm_tok, D), lambda i, p1, p2: (i, 0)),
                pl.BlockSpec((tm_tok, L), lambda i, p1, p2: (i, 0)),
                pl.BlockSpec(memory_space=pl.ANY),
                pl.BlockSpec((1, D), lambda i, p1, p2: (0, 0)),
            ],
            out_specs=pl.BlockSpec((tm_tok, D), lambda i, p1, p2: (i, 0)),
            scratch_shapes=[pltpu.VMEM((2, tm_tok, D), F32), pltpu.VMEM((2, tm_tok, D), F32),
                            pltpu.SemaphoreType.DMA((2, 2))],
        ),
        out_shape=jax.ShapeDtypeStruct((M, D), F32),
        input_output_aliases={2: 0},
        compiler_params=_cparams(("arbitrary",)),
        name="moe_combine",
    )(pos1, pos2, X, wts, y_sorted, g_final.reshape(1, D))
    return X


def _ssd_post(y, xs, z, dsk, ng, wout_ref, ynorm_ref, inner):
    y = (y + dsk * xs) * _silu(z)
    gw = inner // SSM_GROUPS
    for g in range(SSM_GROUPS):
        cols = slice(g * gw, (g + 1) * gw)
        yg = y[:, cols]
        yg = yg * lax.rsqrt(jnp.mean(yg * yg, axis=-1, keepdims=True) + EPS) * ng[:, cols]
        ynorm_ref[:, cols] = yg.astype(BF16)
    return _dot(ynorm_ref[...], wout_ref[...])


def _mamba_prompt_body(x_ref, g_ref, win_ref, wdt_ref, cw_ref, cb_ref, dtb_ref, a_ref, dsk_ref, ng_ref,
                       wout_ref, r_ref, o_ref, conv_ref, ssm_ref, ctx_ref, s_ref, y_ref, ynorm_ref,
                       *, tm, tiles_per_seq, inner, bc):
    i = pl.program_id(0)

    @pl.when(i % tiles_per_seq == 0)
    def _():
        ctx_ref[0:SSM_PAD, :] = jnp.zeros((SSM_PAD, ctx_ref.shape[1]), F32)
        s_ref[...] = jnp.zeros_like(s_ref)

    x = x_ref[...]
    h = _rms(x, g_ref[...]).astype(BF16)
    proj = _dot(h, win_ref[...])
    z = proj[:, :inner]
    xbc = proj[:, inner:]
    ctx_ref[SSM_PAD:SSM_PAD + tm, :] = xbc
    conv_ref[0] = ctx_ref[SSM_PAD + tm - (SSM_CONV - 1):SSM_PAD + tm, :]
    conv = cb_ref[...] + xbc * cw_ref[SSM_CONV - 1:SSM_CONV, :]
    for k in range(SSM_CONV - 1):
        off = SSM_PAD - (SSM_CONV - 1) + k
        conv = conv + ctx_ref[off:off + tm, :] * cw_ref[k:k + 1, :]
    ctx_ref[0:SSM_PAD, :] = ctx_ref[tm:tm + SSM_PAD, :]
    act = _silu(conv)
    xs = act[:, :inner]
    bm = act[:, inner:inner + bc].astype(BF16)
    cm = act[:, inner + bc:].astype(BF16)
    dt = _softplus(_dot(h, wdt_ref[...]) + dtb_ref[...])
    da = dt * a_ref[...]

    rows_i = lax.broadcasted_iota(jnp.int32, (CHUNK, CHUNK), 0)
    cols_i = lax.broadcasted_iota(jnp.int32, (CHUNK, CHUNK), 1)
    causal = rows_i >= cols_i
    tril = causal.astype(F32)
    gw = inner // SSM_GROUPS
    hpg = gw // SSM_HEAD_DIM
    lane_head = lax.broadcasted_iota(jnp.int32, (CHUNK, gw), 1) // SSM_HEAD_DIM
    r_mat = r_ref[...]

    for c in range(tm // CHUNK):
        rows = slice(c * CHUNK, (c + 1) * CHUNK)
        dt_c = dt[rows, :]
        cum = _dot_hi(tril, da[rows, :])
        cum_t = cum.T
        dt_t = dt_c.T
        ecum = jnp.exp(cum)
        e_full = _dot_hi(ecum, r_mat)
        w_in = jnp.exp(cum[CHUNK - 1:CHUNK, :] - cum) * dt_c
        xs_c = xs[rows, :]
        xw = (xs_c * _dot_hi(w_in, r_mat)).astype(BF16)
        xs_b = xs_c.astype(BF16)
        for g in range(SSM_GROUPS):
            gcols = slice(g * gw, (g + 1) * gw)
            ncols = slice(g * SSM_STATE, (g + 1) * SSM_STATE)
            b_g = bm[rows, ncols]
            c_g = cm[rows, ncols]
            cb = _dot_nt(c_g, b_g)
            x_g = xs_b[:, gcols]
            s_g = s_ref[g]
            y_g = _dot_nt(c_g, s_g.astype(BF16)) * e_full[:, gcols]
            for r in range(hpg):
                hd = g * hpg + r
                seg = cum[:, hd:hd + 1] - cum_t[hd:hd + 1, :]
                decay = jnp.where(causal, jnp.exp(jnp.where(causal, seg, 0.0)), 0.0)
                wts = (cb * decay * dt_t[hd:hd + 1, :]).astype(BF16)
                y_g = y_g + _dot(wts, jnp.where(lane_head == r, x_g, jnp.zeros_like(x_g)))
            y_ref[rows, gcols] = y_g
            upd = _dot_tn(xw[:, gcols], b_g)
            for r in range(hpg):
                hd = g * hpg + r
                hrows = slice(r * SSM_HEAD_DIM, (r + 1) * SSM_HEAD_DIM)
                s_ref[g, hrows, :] = s_g[hrows, :] * ecum[CHUNK - 1:CHUNK, hd:hd + 1] + upd[hrows, :]

    out = _ssd_post(y_ref[...], xs, z, dsk_ref[...], ng_ref[...], wout_ref, ynorm_ref, inner)
    o_ref[...] = x + out
    ssm_ref[0] = s_ref[...]


def _mamba_sample_pre_body(x_ref, g_ref, win_ref, wdt_ref, cw_ref, cb_ref, dtb_ref, a_ref, r_ref, cs_ref,
                           z_ref, xs_ref, b_ref, c_ref, xdt_ref, e_ref, newconv_ref, *, inner, bc):
    h = _rms(x_ref[...], g_ref[...]).astype(BF16)
    proj = _dot(h, win_ref[...])
    z_ref[...] = proj[:, :inner]
    xbc = proj[:, inner:]
    conv = cb_ref[...] + xbc * cw_ref[SSM_CONV - 1:SSM_CONV, :]
    for k in range(SSM_CONV - 1):
        conv = conv + cs_ref[k] * cw_ref[k:k + 1, :]
    for k in range(SSM_CONV - 2):
        newconv_ref[k] = cs_ref[k + 1]
    newconv_ref[SSM_CONV - 2] = xbc
    act = _silu(conv)
    xs = act[:, :inner]
    xs_ref[...] = xs
    b_ref[...] = act[:, inner:inner + bc]
    c_ref[...] = act[:, inner + bc:]
    dt = _softplus(_dot(h, wdt_ref[...]) + dtb_ref[...])
    xdt_ref[...] = xs * _dot_hi(dt, r_ref[...])
    e_ref[...] = _dot_hi(jnp.exp(dt * a_ref[...]), r_ref[...])


def _mamba_sample_state_body(h0_ref, xdt_t_ref, e_t_ref, b_ref, c_ref, hn_ref, y_t_ref, *, bt, inner):
    gw = inner // SSM_GROUPS
    for j in range(bt):
        col_x = xdt_t_ref[0, :, j:j + 1]
        col_e = e_t_ref[0, :, j:j + 1]
        for g in range(SSM_GROUPS):
            rows = slice(g * gw, (g + 1) * gw)
            ncols = slice(g * SSM_STATE, (g + 1) * SSM_STATE)
            b_row = b_ref[0, j:j + 1, ncols]
            c_row = c_ref[0, j:j + 1, ncols]
            hn = h0_ref[j, rows, :] * col_e[rows, :] + col_x[rows, :] * b_row
            hn_ref[j, rows, :] = hn
            y_t_ref[0, rows, j:j + 1] = jnp.sum(hn * c_row, axis=-1, keepdims=True)


def _mamba_sample_post_body(x_ref, y_ref, xs_ref, z_ref, dsk_ref, ng_ref, wout_ref, o_ref, ynorm_ref, *, inner):
    out = _ssd_post(y_ref[...], xs_ref[...], z_ref[...], dsk_ref[...], ng_ref[...], wout_ref, ynorm_ref, inner)
    o_ref[...] = x_ref[...] + out


def _mamba(X, n_prompt, n_batch, state_ssm, state_conv, g, w_in, conv_w, conv_b, dt_bias, a_log, d_skip,
           norm_g, w_out, *, tm, bt):
    M, D = X.shape
    inner = w_out.shape[0]
    heads = a_log.shape[0]
    conv_dim = conv_w.shape[1]
    bc = (conv_dim - inner) // 2
    n_sample = M - n_prompt
    seq = n_prompt // n_batch
    tiles_per_seq = seq // tm
    gw = inner // SSM_GROUPS
    L = LANES_V7X

    g2 = g.reshape(1, D)
    w_main = w_in[:, :inner + conv_dim].astype(BF16)
    w_dt = jnp.zeros((D, L), F32).at[:, :heads].set(w_in[:, inner + conv_dim:]).astype(BF16)
    cb2 = conv_b.reshape(1, conv_dim)
    dtb = jnp.zeros((1, L), F32).at[0, :heads].set(dt_bias)
    a_neg = jnp.zeros((1, L), F32).at[0, :heads].set(-jnp.exp(a_log))
    dsk = jnp.repeat(d_skip, SSM_HEAD_DIM).reshape(1, inner)
    ng2 = norm_g.reshape(1, inner)
    w_out_b = w_out.astype(BF16)
    r_mat = (jnp.arange(L, dtype=jnp.int32)[:, None]
             == (jnp.arange(inner, dtype=jnp.int32) // SSM_HEAD_DIM)[None, :]).astype(F32)

    X, conv_p, ssm_p = pl.pallas_call(
        functools.partial(_mamba_prompt_body, tm=tm, tiles_per_seq=tiles_per_seq, inner=inner, bc=bc),
        grid=(n_prompt // tm,),
        in_specs=[
            pl.BlockSpec((tm, D), lambda i: (i, 0)),
            _const_spec((1, D)),
            _const_spec((D, inner + conv_dim)),
            _const_spec((D, L)),
            _const_spec((SSM_CONV, conv_dim)),
            _const_spec((1, conv_dim)),
            _const_spec((1, L)),
            _const_spec((1, L)),
            _const_spec((1, inner)),
            _const_spec((1, inner)),
            _const_spec((inner, D)),
            _const_spec((L, inner)),
        ],
        out_specs=[
            pl.BlockSpec((tm, D), lambda i: (i, 0)),
            pl.BlockSpec((1, SSM_CONV - 1, conv_dim), lambda i: (i // tiles_per_seq, 0, 0)),
            pl.BlockSpec((1, SSM_GROUPS, gw, SSM_STATE), lambda i: (i // tiles_per_seq, 0, 0, 0)),
        ],
        out_shape=[
            jax.ShapeDtypeStruct((M, D), F32),
            jax.ShapeDtypeStruct((n_batch, SSM_CONV - 1, conv_dim), F32),
            jax.ShapeDtypeStruct((n_batch, SSM_GROUPS, gw, SSM_STATE), F32),
        ],
        scratch_shapes=[
            pltpu.VMEM((SSM_PAD + tm, conv_dim), F32),
            pltpu.VMEM((SSM_GROUPS, gw, SSM_STATE), F32),
            pltpu.VMEM((tm, inner), F32),
            pltpu.VMEM((tm, inner), BF16),
        ],
        input_output_aliases={0: 0},
        compiler_params=_cparams(("arbitrary",)),
        name="mamba_prompt",
    )(X, g2, w_main, w_dt, conv_w, cb2, dtb, a_neg, dsk, ng2, w_out_b, r_mat)

    sblk = n_prompt // n_sample
    cs_t = jnp.transpose(state_conv, (1, 0, 2))
    z, xs, b_m, c_m, xdt, e_full, newconv_t = pl.pallas_call(
        functools.partial(_mamba_sample_pre_body, inner=inner, bc=bc),
        grid=(1,),
        in_specs=[
            pl.BlockSpec((n_sample, D), lambda i: (sblk, 0)),
            _const_spec((1, D)),
            _const_spec((D, inner + conv_dim)),
            _const_spec((D, L)),
            _const_spec((SSM_CONV, conv_dim)),
            _const_spec((1, conv_dim)),
            _const_spec((1, L)),
            _const_spec((1, L)),
            _const_spec((L, inner)),
            _const_spec((SSM_CONV - 1, n_sample, conv_dim)),
        ],
        out_specs=[
            pl.BlockSpec((n_sample, inner), lambda i: (0, 0)),
            pl.BlockSpec((n_sample, inner), lambda i: (0, 0)),
            pl.BlockSpec((n_sample, bc), lambda i: (0, 0)),
            pl.BlockSpec((n_sample, bc), lambda i: (0, 0)),
            pl.BlockSpec((n_sample, inner), lambda i: (0, 0)),
            pl.BlockSpec((n_sample, inner), lambda i: (0, 0)),
            pl.BlockSpec((SSM_CONV - 1, n_sample, conv_dim), lambda i: (0, 0, 0)),
        ],
        out_shape=[
            jax.ShapeDtypeStruct((n_sample, inner), F32),
            jax.ShapeDtypeStruct((n_sample, inner), F32),
            jax.ShapeDtypeStruct((n_sample, bc), F32),
            jax.ShapeDtypeStruct((n_sample, bc), F32),
            jax.ShapeDtypeStruct((n_sample, inner), F32),
            jax.ShapeDtypeStruct((n_sample, inner), F32),
            jax.ShapeDtypeStruct((SSM_CONV - 1, n_sample, conv_dim), F32),
        ],
        compiler_params=_cparams(("arbitrary",)),
        name="mamba_sample_pre",
    )(X, g2, w_main, w_dt, conv_w, cb2, dtb, a_neg, r_mat, cs_t)
    conv_s = jnp.transpose(newconv_t, (1, 0, 2))

    nblk = n_sample // bt

    def to_cols(arr):
        return jnp.transpose(arr.reshape(nblk, bt, inner), (0, 2, 1))

    h0 = state_ssm.reshape(n_sample, inner, SSM_STATE)
    h_new, y_t = pl.pallas_call(
        functools.partial(_mamba_sample_state_body, bt=bt, inner=inner),
        grid=(nblk,),
        in_specs=[
            pl.BlockSpec((bt, inner, SSM_STATE), lambda i: (i, 0, 0)),
            pl.BlockSpec((1, inner, bt), lambda i: (i, 0, 0)),
            pl.BlockSpec((1, inner, bt), lambda i: (i, 0, 0)),
            pl.BlockSpec((1, bt, bc), lambda i: (i, 0, 0)),
            pl.BlockSpec((1, bt, bc), lambda i: (i, 0, 0)),
        ],
        out_specs=[
            pl.BlockSpec((bt, inner, SSM_STATE), lambda i: (i, 0, 0)),
            pl.BlockSpec((1, inner, bt), lambda i: (i, 0, 0)),
        ],
        out_shape=[
            jax.ShapeDtypeStruct((n_sample, inner, SSM_STATE), F32),
            jax.ShapeDtypeStruct((nblk, inner, bt), F32),
        ],
        compiler_params=_cparams(("arbitrary",)),
        name="mamba_sample_state",
    )(h0, to_cols(xdt), to_cols(e_full), b_m.reshape(nblk, bt, bc), c_m.reshape(nblk, bt, bc))
    y_s = jnp.transpose(y_t, (0, 2, 1)).reshape(n_sample, inner)

    X = pl.pallas_call(
        functools.partial(_mamba_sample_post_body, inner=inner),
        grid=(1,),
        in_specs=[
            pl.BlockSpec((n_sample, D), lambda i: (sblk, 0)),
            pl.BlockSpec((n_sample, inner), lambda i: (0, 0)),
            pl.BlockSpec((n_sample, inner), lambda i: (0, 0)),
            pl.BlockSpec((n_sample, inner), lambda i: (0, 0)),
            _const_spec((1, inner)),
            _const_spec((1, inner)),
            _const_spec((inner, D)),
        ],
        out_specs=pl.BlockSpec((n_sample, D), lambda i: (sblk, 0)),
        out_shape=jax.ShapeDtypeStruct((M, D), F32),
        scratch_shapes=[pltpu.VMEM((n_sample, inner), BF16)],
        input_output_aliases={0: 0},
        compiler_params=_cparams(("arbitrary",)),
        name="mamba_sample_post",
    )(X, y_s, xs, z, dsk, ng2, w_out_b)

    ssm_p = ssm_p.reshape(n_batch, heads, SSM_HEAD_DIM, SSM_STATE)
    ssm_s = h_new.reshape(n_sample, heads, SSM_HEAD_DIM, SSM_STATE)
    return X, conv_p, conv_s, ssm_p, ssm_s


def _causal_dwconv_tile(ctx_ref, dw_ref, acc_ref, tm, d):
    S = SUBLANES_V7X
    first = CFM_PAD - (CFM_KERNEL - 1)
    rb, lb = CONV_ROW_BLOCK, CONV_LANE_BLOCK
    for r0 in range(0, tm, rb):
        for l0 in range(0, d, lb):
            lanes = slice(l0, l0 + lb)
            y = None
            for b in range(S):
                pb = None
                for a in range((first + CFM_KERNEL - 1) // S + 1):
                    k = S * a + b - first
                    if 0 <= k < CFM_KERNEL:
                        term = ctx_ref[r0 + S * a:r0 + S * a + rb + S, lanes] * dw_ref[k:k + 1, lanes]
                        pb = term if pb is None else pb + term
                part = pb[b:b + rb, :]
                y = part if y is None else y + part
            acc_ref[r0:r0 + rb, lanes] = y


def _cfm_prompt_body(x_ref, g_ref, w1_ref, b1_ref, dw_ref, dwb_ref, lng_ref, lnb_ref, w2_ref, b2_ref,
                     o_ref, buf_ref, ctx_ref, acc_ref, *, tm, tiles_per_seq, d):
    i = pl.program_id(0)

    @pl.when(i % tiles_per_seq == 0)
    def _():
        ctx_ref[0:CFM_PAD, :] = jnp.zeros((CFM_PAD, d), F32)
        ctx_ref[CFM_PAD + tm:CFM_PAD + tm + SUBLANES_V7X, :] = jnp.zeros((SUBLANES_V7X, d), F32)

    x = x_ref[...]
    h = _rms(x, g_ref[...]).astype(BF16)
    a = _dot(h, w1_ref[...]) + b1_ref[...]
    ctx_ref[CFM_PAD:CFM_PAD + tm, :] = a[:, :d] * _sigmoid(a[:, d:])
    _causal_dwconv_tile(ctx_ref, dw_ref, acc_ref, tm, d)
    buf_ref[0] = ctx_ref[CFM_PAD + tm - (CFM_KERNEL - 1):CFM_PAD + tm, :]
    ctx_ref[0:CFM_PAD, :] = ctx_ref[tm:tm + CFM_PAD, :]
    acc = acc_ref[...] + dwb_ref[...]
    hc = _silu(_layernorm(acc, lng_ref[...], lnb_ref[...])).astype(BF16)
    o_ref[...] = x + _dot(hc, w2_ref[...]) + b2_ref[...]


def _cfm_sample_body(x_ref, g_ref, w1_ref, b1_ref, dw_ref, dwb_ref, lng_ref, lnb_ref, w2_ref, b2_ref, cs_ref,
                     o_ref, new_ref, *, d):
    x = x_ref[...]
    h = _rms(x, g_ref[...]).astype(BF16)
    a = _dot(h, w1_ref[...]) + b1_ref[...]
    glu = a[:, :d] * _sigmoid(a[:, d:])
    acc = dwb_ref[...] + glu * dw_ref[CFM_KERNEL - 1:CFM_KERNEL, :]
    for k in range(CFM_KERNEL - 1):
        acc = acc + cs_ref[k] * dw_ref[k:k + 1, :]
    for k in range(CFM_KERNEL - 2):
        new_ref[k] = cs_ref[k + 1]
    new_ref[CFM_KERNEL - 2] = glu
    hc = _silu(_layernorm(acc, lng_ref[...], lnb_ref[...])).astype(BF16)
    o_ref[...] = x + _dot(hc, w2_ref[...]) + b2_ref[...]


def _conformer(X, n_prompt, n_batch, state_conv, g, w_pw1, b_pw1, dw_w, dw_b, ln_g, ln_b, w_pw2, b_pw2,
               *, tm, bt):
    M, D = X.shape
    n_sample = M - n_prompt
    seq = n_prompt // n_batch
    tiles_per_seq = seq // tm
    K = CFM_KERNEL
    g2 = g.reshape(1, D)
    w1, w2 = w_pw1.astype(BF16), w_pw2.astype(BF16)
    b1, b2 = b_pw1.reshape(1, 2 * D), b_pw2.reshape(1, D)
    dwb, lng, lnb = dw_b.reshape(1, D), ln_g.reshape(1, D), ln_b.reshape(1, D)
    weight_specs = [
        _const_spec((1, D)), _const_spec((D, 2 * D)), _const_spec((1, 2 * D)), _const_spec((K, D)),
        _const_spec((1, D)), _const_spec((1, D)), _const_spec((1, D)), _const_spec((D, D)), _const_spec((1, D)),
    ]
    weights = (g2, w1, b1, dw_w, dwb, lng, lnb, w2, b2)

    X, buf_p = pl.pallas_call(
        functools.partial(_cfm_prompt_body, tm=tm, tiles_per_seq=tiles_per_seq, d=D),
        grid=(n_prompt // tm,),
        in_specs=[pl.BlockSpec((tm, D), lambda i: (i, 0))] + weight_specs,
        out_specs=[
            pl.BlockSpec((tm, D), lambda i: (i, 0)),
            pl.BlockSpec((1, K - 1, D), lambda i: (i // tiles_per_seq, 0, 0)),
        ],
        out_shape=[jax.ShapeDtypeStruct((M, D), F32), jax.ShapeDtypeStruct((n_batch, K - 1, D), F32)],
        scratch_shapes=[pltpu.VMEM((CFM_PAD + tm + SUBLANES_V7X, D), F32), pltpu.VMEM((tm, D), F32)],
        input_output_aliases={0: 0},
        compiler_params=_cparams(("arbitrary",)),
        name="conformer_prompt",
    )(X, *weights)

    first = n_prompt // bt
    cs_t = jnp.transpose(state_conv, (1, 0, 2))
    X, new_t = pl.pallas_call(
        functools.partial(_cfm_sample_body, d=D),
        grid=(n_sample // bt,),
        in_specs=[pl.BlockSpec((bt, D), lambda i: (i + first, 0))] + weight_specs
        + [pl.BlockSpec((K - 1, bt, D), lambda i: (0, i, 0))],
        out_specs=[
            pl.BlockSpec((bt, D), lambda i: (i + first, 0)),
            pl.BlockSpec((K - 1, bt, D), lambda i: (0, i, 0)),
        ],
        out_shape=[jax.ShapeDtypeStruct((M, D), F32), jax.ShapeDtypeStruct((K - 1, n_sample, D), F32)],
        input_output_aliases={0: 0},
        compiler_params=_cparams(("arbitrary",)),
        name="conformer_sample",
    )(X, *weights, cs_t)
    return X, buf_p, jnp.transpose(new_t, (1, 0, 2))


def kernel(x_prompt, x_sample, state_ssm, state_conv_ssm, state_conv_cfm, norm_mix_g, norm_ffn_g, norm_final_g, a_w_in, a_b_in, a_ln_g, a_ln_b, a_w_s, a_b_s, a_w_out, b_w_in, b_conv_w, b_conv_b, b_dt_bias, b_a_log, b_d, b_norm_g, b_w_out, c_w_pw1, c_b_pw1, c_dw_w, c_dw_b, c_ln_g, c_ln_b, c_w_pw2, c_b_pw2, f_w_gate, f_w_up, f_w_down, e_w_router, e_w_gate, e_w_up, e_w_down):
    n_batch, seq, D = x_prompt.shape
    n_sample = x_sample.shape[0]
    n_prompt = n_batch * seq
    depth = norm_mix_g.shape[0]
    d_ff = f_w_gate.shape[2]
    assert x_sample.shape[1] == 1 and n_prompt % n_sample == 0 and seq % CHUNK == 0

    X = jnp.concatenate([x_prompt.reshape(n_prompt, D), x_sample.reshape(n_sample, D)], axis=0)
    v_p, v_s, ssm_p, ssm_s, cs_p, cs_s, cc_p, cc_s = [], [], [], [], [], [], [], []
    for i in range(depth):
        kind, j = i % 3, i // 3
        if kind == 0:
            X, vp, vs = _gmlp(X, n_prompt, n_batch, norm_mix_g[i], a_w_in[j], a_b_in[j], a_ln_g[j], a_ln_b[j],
                              a_w_s[j], a_b_s[j], a_w_out[j], tm=512)
            v_p.append(vp)
            v_s.append(vs)
        elif kind == 1:
            X, cbp, cbs, hlp, hls = _mamba(X, n_prompt, n_batch, state_ssm[j], state_conv_ssm[j], norm_mix_g[i],
                                           b_w_in[j], b_conv_w[j], b_conv_b[j], b_dt_bias[j], b_a_log[j], b_d[j],
                                           b_norm_g[j], b_w_out[j], tm=256, bt=4)
            cs_p.append(cbp)
            cs_s.append(cbs)
            ssm_p.append(hlp)
            ssm_s.append(hls)
        else:
            X, cbp, cbs = _conformer(X, n_prompt, n_batch, state_conv_cfm[j], norm_mix_g[i], c_w_pw1[j], c_b_pw1[j],
                                     c_dw_w[j], c_dw_b[j], c_ln_g[j], c_ln_b[j], c_w_pw2[j], c_b_pw2[j],
                                     tm=512, bt=32)
            cc_p.append(cbp)
            cc_s.append(cbs)
        k = i // 2
        if i % 2 == 0:
            X = _ffn_dense(X, n_prompt, norm_ffn_g[i], f_w_gate[k], f_w_up[k], f_w_down[k],
                           tm=512, f_chunk=d_ff // 2)
        else:
            X = _moe(X, norm_ffn_g[i], e_w_router[k], e_w_gate[k], e_w_up[k], e_w_down[k], norm_final_g,
                     tm_tok=384, tm=512, f_chunk=d_ff // 2, final_norm=(i == depth - 1))
    assert depth % 2 == 0
    y_prompt = X[:n_prompt].reshape(n_batch, seq, D)
    y_sample = X[n_prompt:].reshape(n_sample, 1, D)
    return (y_prompt, y_sample, jnp.stack(v_p), jnp.stack(v_s), jnp.stack(ssm_p), jnp.stack(ssm_s),
            jnp.stack(cs_p), jnp.stack(cs_s), jnp.stack(cc_p), jnp.stack(cc_s))
```

```python
import functools
import math

import jax
import jax.numpy as jnp
from jax import lax
from jax.experimental import pallas as pl
from jax.experimental.pallas import tpu as pltpu

F32 = jnp.float32
BF16 = jnp.bfloat16
EPS = 1e-6
HIGHEST = lax.Precision.HIGHEST

LANES_V7X = 128
SUBLANES_V7X = 8
VMEM_LIMIT_V7X = 60 * 1024 * 1024

CHUNK = 128
GM_GROUPS = 8
SSM_GROUPS = 8
SSM_HEAD_DIM = 64
SSM_STATE = 128
SSM_CONV = 4
CFM_KERNEL = 31
N_EXPERTS = 8
CFM_PAD = 32
SSM_PAD = 8
CONV_ROW_BLOCK = 128
CONV_LANE_BLOCK = 256


def _cparams(sem):
    return pltpu.CompilerParams(dimension_semantics=sem, vmem_limit_bytes=VMEM_LIMIT_V7X)


def _const_spec(shape):
    nd = len(shape)
    return pl.BlockSpec(shape, lambda *_: (0,) * nd, pipeline_mode=pl.Buffered(1))


def _dot(a, b):
    return jnp.dot(a, b, preferred_element_type=F32)


def _dot_nt(a, b):
    return lax.dot_general(a, b, (((1,), (1,)), ((), ())), preferred_element_type=F32)


def _dot_tn(a, b):
    return lax.dot_general(a, b, (((0,), (0,)), ((), ())), preferred_element_type=F32)


def _dot_hi(a, b):
    return jnp.dot(a, b, precision=HIGHEST, preferred_element_type=F32)


def _rms(x, g):
    return x * lax.rsqrt(jnp.mean(x * x, axis=-1, keepdims=True) + EPS) * g


def _layernorm(x, g, b):
    xc = x - jnp.mean(x, axis=-1, keepdims=True)
    return xc * lax.rsqrt(jnp.mean(xc * xc, axis=-1, keepdims=True) + EPS) * g + b


def _sigmoid(x):
    return 1.0 / (1.0 + jnp.exp(-x))


def _silu(x):
    return x * _sigmoid(x)


def _gelu(x):
    return 0.5 * x * (1.0 + lax.erf(x * (2.0 ** -0.5)))


def _softplus(x):
    return jnp.maximum(x, 0.0) + jnp.log(1.0 + jnp.exp(-jnp.abs(x)))


def _gmlp_prompt_body(x_ref, g_ref, win_ref, bin_ref, lng_ref, lnb_ref, ws_ref, bs_ref, wout_ref,
                      o_ref, v_ref, gated_ref, *, tm, width):
    x = x_ref[...]
    h = _rms(x, g_ref[...]).astype(BF16)
    u = _gelu(_dot(h, win_ref[:, :width]) + bin_ref[:, :width])
    hv = _gelu(_dot(h, win_ref[:, width:]) + bin_ref[:, width:])
    v = _layernorm(hv, lng_ref[...], lnb_ref[...])
    v_ref[0] = v[tm - CHUNK:, :]
    vb = v.astype(BF16)
    gdim = width // GM_GROUPS
    causal = (lax.broadcasted_iota(jnp.int32, (CHUNK, CHUNK), 0)
              >= lax.broadcasted_iota(jnp.int32, (CHUNK, CHUNK), 1))
    for g in range(GM_GROUPS):
        wc = jnp.where(causal, ws_ref[g], 0.0).astype(BF16)
        bias = bs_ref[:, g:g + 1]
        for c in range(tm // CHUNK):
            rows = slice(c * CHUNK, (c + 1) * CHUNK)
            cols = slice(g * gdim, (g + 1) * gdim)
            s = _dot(wc, vb[rows, cols]) + bias
            gated_ref[rows, cols] = (u[rows, cols] * s).astype(BF16)
    o_ref[...] = x + _dot(gated_ref[...], wout_ref[...])


def _gmlp_sample_body(x_ref, g_ref, win_ref, bin_ref, lng_ref, lnb_ref, ws0_ref, bs0_ref, wout_ref,
                      o_ref, v_ref, *, width):
    x = x_ref[...]
    h = _rms(x, g_ref[...]).astype(BF16)
    u = _gelu(_dot(h, win_ref[:, :width]) + bin_ref[:, :width])
    hv = _gelu(_dot(h, win_ref[:, width:]) + bin_ref[:, width:])
    v = _layernorm(hv, lng_ref[...], lnb_ref[...])
    v_ref[...] = v
    s = v * ws0_ref[...] + bs0_ref[...]
    o_ref[...] = x + _dot((u * s).astype(BF16), wout_ref[...])


def _gmlp(X, n_prompt, n_batch, g, w_in, b_in, ln_g, ln_b, w_s, b_s, w_out, *, tm):
    M, D = X.shape
    width = w_out.shape[0]
    seq = n_prompt // n_batch
    tiles_per_seq = seq // tm
    n_sample = M - n_prompt
    g2 = g.reshape(1, D)
    b_in2 = b_in.reshape(1, 2 * width)
    ln_g2, ln_b2 = ln_g.reshape(1, width), ln_b.reshape(1, width)
    w_in_b, w_out_b = w_in.astype(BF16), w_out.astype(BF16)

    X, v_p = pl.pallas_call(
        functools.partial(_gmlp_prompt_body, tm=tm, width=width),
        grid=(n_prompt // tm,),
        in_specs=[
            pl.BlockSpec((tm, D), lambda i: (i, 0)),
            _const_spec((1, D)),
            _const_spec((D, 2 * width)),
            _const_spec((1, 2 * width)),
            _const_spec((1, width)),
            _const_spec((1, width)),
            _const_spec((GM_GROUPS, CHUNK, CHUNK)),
            _const_spec((CHUNK, GM_GROUPS)),
            _const_spec((width, D)),
        ],
        out_specs=[
            pl.BlockSpec((tm, D), lambda i: (i, 0)),
            pl.BlockSpec((1, CHUNK, width), lambda i: (i // tiles_per_seq, 0, 0)),
        ],
        out_shape=[jax.ShapeDtypeStruct((M, D), F32),
                   jax.ShapeDtypeStruct((n_batch, CHUNK, width), F32)],
        scratch_shapes=[pltpu.VMEM((tm, width), BF16)],
        input_output_aliases={0: 0},
        compiler_params=_cparams(("arbitrary",)),
        name="gmlp_prompt",
    )(X, g2, w_in_b, b_in2, ln_g2, ln_b2, w_s, b_s.T, w_out_b)

    gdim = width // GM_GROUPS
    ws0 = jnp.repeat(w_s[:, 0, 0], gdim).reshape(1, width)
    bs0 = jnp.repeat(b_s[:, 0], gdim).reshape(1, width)
    sblk = n_prompt // n_sample
    X, v_s = pl.pallas_call(
        functools.partial(_gmlp_sample_body, width=width),
        grid=(1,),
        in_specs=[
            pl.BlockSpec((n_sample, D), lambda i: (sblk, 0)),
            _const_spec((1, D)),
            _const_spec((D, 2 * width)),
            _const_spec((1, 2 * width)),
            _const_spec((1, width)),
            _const_spec((1, width)),
            _const_spec((1, width)),
            _const_spec((1, width)),
            _const_spec((width, D)),
        ],
        out_specs=[
            pl.BlockSpec((n_sample, D), lambda i: (sblk, 0)),
            pl.BlockSpec((n_sample, width), lambda i: (0, 0)),
        ],
        out_shape=[jax.ShapeDtypeStruct((M, D), F32),
                   jax.ShapeDtypeStruct((n_sample, width), F32)],
        input_output_aliases={0: 0},
        compiler_params=_cparams(("arbitrary",)),
        name="gmlp_sample",
    )(X, g2, w_in_b, b_in2, ln_g2, ln_b2, ws0, bs0, w_out_b)
    return X, v_p, v_s.reshape(n_sample, 1, width)


def _swiglu_tile(h, wg_ref, wu_ref, wd_ref, f_chunk):
    d_ff = wd_ref.shape[0]
    acc = None
    for f in range(d_ff // f_chunk):
        cols = slice(f * f_chunk, (f + 1) * f_chunk)
        a = _dot(h, wg_ref[:, cols])
        b = _dot(h, wu_ref[:, cols])
        part = _dot((_silu(a) * b).astype(BF16), wd_ref[cols, :])
        acc = part if acc is None else acc + part
    return acc


def _ffn_dense_body(x_ref, g_ref, wg_ref, wu_ref, wd_ref, o_ref, *, f_chunk):
    x = x_ref[...]
    h = _rms(x, g_ref[...]).astype(BF16)
    o_ref[...] = x + _swiglu_tile(h, wg_ref, wu_ref, wd_ref, f_chunk)


def _ffn_dense_rows(X, g2, wg, wu, wd, *, tm, first_block, n_tiles, f_chunk):
    M, D = X.shape
    d_ff = wd.shape[0]
    return pl.pallas_call(
        functools.partial(_ffn_dense_body, f_chunk=f_chunk),
        grid=(n_tiles,),
        in_specs=[
            pl.BlockSpec((tm, D), lambda i: (i + first_block, 0)),
            _const_spec((1, D)),
            _const_spec((D, d_ff)),
            _const_spec((D, d_ff)),
            _const_spec((d_ff, D)),
        ],
        out_specs=pl.BlockSpec((tm, D), lambda i: (i + first_block, 0)),
        out_shape=jax.ShapeDtypeStruct((M, D), F32),
        input_output_aliases={0: 0},
        compiler_params=_cparams(("arbitrary",)),
        name="ffn_dense",
    )(X, g2, wg, wu, wd)


def _ffn_dense(X, n_prompt, g, w_gate, w_up, w_down, *, tm, f_chunk):
    M, D = X.shape
    n_sample = M - n_prompt
    g2 = g.reshape(1, D)
    wg, wu, wd = w_gate.astype(BF16), w_up.astype(BF16), w_down.astype(BF16)
    X = _ffn_dense_rows(X, g2, wg, wu, wd, tm=tm, first_block=0, n_tiles=n_prompt // tm, f_chunk=f_chunk)
    X = _ffn_dense_rows(X, g2, wg, wu, wd, tm=n_sample, first_block=n_prompt // n_sample, n_tiles=1,
                        f_chunk=f_chunk)
    return X


def _router_body(x_ref, g_ref, wr_ref, idx_ref, wts_ref):
    h = _rms(x_ref[...], g_ref[...])
    logits = _dot_hi(h, wr_ref[...])
    lane = lax.broadcasted_iota(jnp.int32, logits.shape, 1)
    neg = jnp.float32(-jnp.inf)
    logits = jnp.where(lane < N_EXPERTS, logits, neg)
    m1 = jnp.max(logits, axis=-1, keepdims=True)
    i1 = jnp.min(jnp.where(logits == m1, lane, LANES_V7X), axis=-1, keepdims=True)
    rest = jnp.where(lane == i1, neg, logits)
    m2 = jnp.max(rest, axis=-1, keepdims=True)
    i2 = jnp.min(jnp.where(rest == m2, lane, LANES_V7X), axis=-1, keepdims=True)
    e = jnp.exp(m2 - m1)
    w1 = 1.0 / (1.0 + e)
    w2 = e / (1.0 + e)
    idx_ref[...] = jnp.where(lane == 0, i1, jnp.where(lane == 1, i2, 0))
    wts_ref[...] = jnp.where(lane == 0, w1, jnp.where(lane == 1, w2, 0.0))


def _slot_table_body(pos1_ref, pos2_ref, pad_lo_ref, pad_hi_ref, dst_ref, *, n_tokens, region, tm, n_ranges):
    def pad(s, carry):
        dst_ref[s] = 2 * region + s % tm
        return carry

    for r in range(n_ranges):
        lax.fori_loop(pad_lo_ref[r], pad_hi_ref[r], pad, 0)

    def place(t, carry):
        dst_ref[pos1_ref[t]] = t
        dst_ref[pos2_ref[t]] = region + t
        return carry

    lax.fori_loop(0, n_tokens, place, 0, unroll=8)


def _slot_token(d, region):
    return jnp.where(d >= 2 * region, 0, jnp.where(d >= region, d - region, d))


def _gather_row_copy(x_hbm, tok, buf, r, sem):
    return pltpu.make_async_copy(x_hbm.at[pl.ds(tok, 1), :], buf.at[pl.ds(r, 1), :], sem)


def _scatter_row_copy(buf, r, y_hbm, row, sem):
    return pltpu.make_async_copy(buf.at[pl.ds(r, 1), :], y_hbm.at[pl.ds(row, 1), :], sem)


def _wait_rows(hbm, buf, sem, n_rows):
    pltpu.make_async_copy(hbm.at[pl.ds(0, n_rows), :], buf, sem).wait()


def _moe_ffn_body(tile_expert_ref, n_tiles_ref, dst_ref, x_hbm, g_ref, wg_ref, wu_ref, wd_ref, y_hbm,
                  xbuf_ref, ybuf_ref, h_ref, gsem_ref, ssem_ref, *, tm, region, gap):
    t = pl.program_id(0)
    n_tiles = n_tiles_ref[0]
    slot = t % 2
    n_chunks = wd_ref.shape[1]
    rows_per_chunk = tm // n_chunks

    @pl.when(t == 0)
    def _():
        def first(r, carry):
            tok = _slot_token(dst_ref[r], region)
            _gather_row_copy(x_hbm, tok, xbuf_ref.at[0], r, gsem_ref.at[0]).start()
            return carry
        lax.fori_loop(0, tm, first, 0, unroll=8)
        ybuf_ref[1] = jnp.zeros(ybuf_ref.shape[1:], F32)
        if gap:
            fills = [pltpu.make_async_copy(ybuf_ref.at[1, pl.ds(0, gap), :], y_hbm.at[pl.ds(base, gap), :],
                                           ssem_ref.at[2]) for base in (region - gap, 2 * region - gap)]
            for c in fills:
                c.start()
            for c in fills:
                c.wait()

    @pl.when(t < n_tiles)
    def _():
        @pl.when(t > 0)
        def _():
            _wait_rows(y_hbm, ybuf_ref.at[slot], ssem_ref.at[slot], tm)
        _wait_rows(x_hbm, xbuf_ref.at[slot], gsem_ref.at[slot], tm)
        h_ref[...] = _rms(xbuf_ref[slot], g_ref[...]).astype(BF16)
        ybuf_ref[slot] = jnp.zeros(ybuf_ref.shape[1:], F32)
        next_base = jnp.minimum(t + 1, n_tiles - 1) * tm
        prev_base = jnp.maximum(t - 1, 0) * tm
        has_prev = t > 0

        def chunk(f, carry):
            for r in range(rows_per_chunk):
                row = f * rows_per_chunk + r
                tok = _slot_token(dst_ref[next_base + row], region)
                _gather_row_copy(x_hbm, tok, xbuf_ref.at[1 - slot], row, gsem_ref.at[1 - slot]).start()
                d = jnp.where(has_prev, dst_ref[prev_base + row], 2 * region + row)
                _scatter_row_copy(ybuf_ref.at[1 - slot], row, y_hbm, d, ssem_ref.at[1 - slot]).start()
            h = h_ref[...]
            a = _dot(h, wg_ref[0, f])
            b = _dot(h, wu_ref[0, f])
            ybuf_ref[slot] += _dot((_silu(a) * b).astype(BF16), wd_ref[0, f])
            return carry

        lax.fori_loop(0, n_chunks, chunk, 0)

        @pl.when(t == n_tiles - 1)
        def _():
            _wait_rows(y_hbm, ybuf_ref.at[1 - slot], ssem_ref.at[1 - slot], tm)

            def last(r, carry):
                _scatter_row_copy(ybuf_ref.at[slot], r, y_hbm, dst_ref[t * tm + r], ssem_ref.at[slot]).start()
                return carry
            lax.fori_loop(0, tm, last, 0, unroll=8)
            _wait_rows(x_hbm, xbuf_ref.at[1 - slot], gsem_ref.at[1 - slot], tm)
            _wait_rows(y_hbm, ybuf_ref.at[slot], ssem_ref.at[slot], tm)


def _moe_combine_body(x_ref, wts_ref, y1_ref, y2_ref, gf_ref, o_ref, *, final_norm):
    out = x_ref[...] + (wts_ref[:, 0:1] * y1_ref[...] + wts_ref[:, 1:2] * y2_ref[...])
    if final_norm:
        out = _rms(out, gf_ref[...])
    o_ref[...] = out


def _moe(X, n_prompt, g, w_router, w_gate, w_up, w_down, g_final, *, tm_tok, tm_out, tm, f_chunk, final_norm):
    M, D = X.shape
    E = N_EXPERTS
    unit = math.lcm(tm_tok, tm_out, M - n_prompt)
    region = -(-M // unit) * unit
    d_ff = w_down.shape[1]
    g2 = g.reshape(1, D)
    wr = jnp.zeros((D, LANES_V7X), F32).at[:, :E].set(w_router)

    L = LANES_V7X
    idx, wts = pl.pallas_call(
        _router_body,
        grid=(M // tm_tok,),
        in_specs=[pl.BlockSpec((tm_tok, D), lambda i: (i, 0)), _const_spec((1, D)), _const_spec((D, L))],
        out_specs=[pl.BlockSpec((tm_tok, L), lambda i: (i, 0)), pl.BlockSpec((tm_tok, L), lambda i: (i, 0))],
        out_shape=[jax.ShapeDtypeStruct((M, L), jnp.int32), jax.ShapeDtypeStruct((M, L), F32)],
        compiler_params=_cparams(("arbitrary",)),
        name="moe_router",
    )(X, g2, wr)

    i1, i2 = idx[:, 0], idx[:, 1]
    eids = jnp.arange(E, dtype=jnp.int32)
    sel = ((i1[:, None] == eids) | (i2[:, None] == eids)).astype(jnp.int32)
    count = jnp.sum(sel, axis=0)
    tiles_e = (count + tm - 1) // tm
    tiles_end = jnp.cumsum(tiles_e)
    n_tiles = tiles_end[-1]
    start_e = (tiles_end - tiles_e) * tm
    pos = start_e[None, :] + jnp.cumsum(sel, axis=0) - sel
    pos1 = jnp.sum(jnp.where(i1[:, None] == eids, pos, 0), axis=1).astype(jnp.int32)
    pos2 = jnp.sum(jnp.where(i2[:, None] == eids, pos, 0), axis=1).astype(jnp.int32)
    max_tiles = (2 * M + E * (tm - 1)) // tm
    n_slots = max_tiles * tm
    pad_lo = jnp.concatenate([start_e + count, (n_tiles * tm)[None]]).astype(jnp.int32)
    pad_hi = jnp.concatenate([tiles_end * tm, jnp.array([n_slots], jnp.int32)]).astype(jnp.int32)
    tile_ids = jnp.arange(max_tiles, dtype=jnp.int32)
    tile_expert = jnp.sum((jnp.minimum(tile_ids, n_tiles - 1)[:, None] >= tiles_end[None, :]).astype(jnp.int32),
                          axis=1)
    tile_expert = jnp.minimum(tile_expert, E - 1).astype(jnp.int32)
    n_tiles_arr = n_tiles.astype(jnp.int32).reshape(1)

    smem = pl.BlockSpec(memory_space=pltpu.SMEM)
    dst_of_slot = pl.pallas_call(
        functools.partial(_slot_table_body, n_tokens=M, region=region, tm=tm, n_ranges=E + 1),
        in_specs=[smem, smem, smem, smem],
        out_specs=smem,
        out_shape=jax.ShapeDtypeStruct((n_slots,), jnp.int32),
        name="moe_slot_table",
    )(pos1, pos2, pad_lo, pad_hi)

    n_chunks = d_ff // f_chunk
    wg = w_gate.astype(BF16).reshape(E, D, n_chunks, f_chunk).transpose(0, 2, 1, 3)
    wu = w_up.astype(BF16).reshape(E, D, n_chunks, f_chunk).transpose(0, 2, 1, 3)
    wd = w_down.astype(BF16).reshape(E, n_chunks, f_chunk, D)
    assert region - M <= tm
    y_rows = pl.pallas_call(
        functools.partial(_moe_ffn_body, tm=tm, region=region, gap=region - M),
        grid_spec=pltpu.PrefetchScalarGridSpec(
            num_scalar_prefetch=3,
            grid=(max_tiles,),
            in_specs=[
                pl.BlockSpec(memory_space=pl.ANY),
                pl.BlockSpec((1, D), lambda t, te, nt, dst: (0, 0)),
                pl.BlockSpec((1, n_chunks, D, f_chunk), lambda t, te, nt, dst: (te[t], 0, 0, 0)),
                pl.BlockSpec((1, n_chunks, D, f_chunk), lambda t, te, nt, dst: (te[t], 0, 0, 0)),
                pl.BlockSpec((1, n_chunks, f_chunk, D), lambda t, te, nt, dst: (te[t], 0, 0, 0)),
            ],
            out_specs=pl.BlockSpec(memory_space=pl.ANY),
            scratch_shapes=[pltpu.VMEM((2, tm, D), F32), pltpu.VMEM((2, tm, D), F32), pltpu.VMEM((tm, D), BF16),
                            pltpu.SemaphoreType.DMA((2,)), pltpu.SemaphoreType.DMA((3,))],
        ),
        out_shape=jax.ShapeDtypeStruct((2 * region + tm, D), F32),
        compiler_params=_cparams(("arbitrary",)),
        name="moe_ffn",
    )(tile_expert, n_tiles_arr, dst_of_slot, X, g2, wg, wu, wd)

    def combine_rows(tm_c, first_block, n_tiles_c, in_place):
        second = region // tm_c
        out_first = first_block if in_place else 0
        return pl.pallas_call(
            functools.partial(_moe_combine_body, final_norm=final_norm),
            grid=(n_tiles_c,),
            in_specs=[
                pl.BlockSpec((tm_c, D), lambda i: (i + first_block, 0)),
                pl.BlockSpec((tm_c, L), lambda i: (i + first_block, 0)),
                pl.BlockSpec((tm_c, D), lambda i: (i + first_block, 0)),
                pl.BlockSpec((tm_c, D), lambda i: (i + first_block + second, 0)),
                _const_spec((1, D)),
            ],
            out_specs=pl.BlockSpec((tm_c, D), lambda i: (i + out_first, 0)),
            out_shape=jax.ShapeDtypeStruct((M if in_place else n_tiles_c * tm_c, D), F32),
            input_output_aliases={0: 0} if in_place else {},
            compiler_params=_cparams(("arbitrary",)),
            name="moe_combine",
        )(X, wts, y_rows, y_rows, g_final.reshape(1, D))

    if not final_norm:
        return combine_rows(tm_tok, 0, M // tm_tok, True)
    n_sample = M - n_prompt
    y_p = combine_rows(tm_out, 0, n_prompt // tm_out, False)
    y_s = combine_rows(n_sample, n_prompt // n_sample, 1, False)
    return y_p, y_s


def _ssd_post(y, xs, z, dsk, ng, wout_ref, ynorm_ref, inner):
    y = (y + dsk * xs) * _silu(z)
    gw = inner // SSM_GROUPS
    for g in range(SSM_GROUPS):
        cols = slice(g * gw, (g + 1) * gw)
        yg = y[:, cols]
        yg = yg * lax.rsqrt(jnp.mean(yg * yg, axis=-1, keepdims=True) + EPS) * ng[:, cols]
        ynorm_ref[:, cols] = yg.astype(BF16)
    return _dot(ynorm_ref[...], wout_ref[...])


def _mamba_prompt_body(x_ref, g_ref, win_ref, wdt_ref, cw_ref, cb_ref, dtb_ref, a_ref, dsk_ref, ng_ref,
                       wout_ref, r_ref, o_ref, conv_ref, ssm_ref, ctx_ref, s_ref, y_ref, ynorm_ref,
                       *, tm, tiles_per_seq, inner, bc):
    i = pl.program_id(0)

    @pl.when(i % tiles_per_seq == 0)
    def _():
        ctx_ref[0:SSM_PAD, :] = jnp.zeros((SSM_PAD, ctx_ref.shape[1]), F32)
        s_ref[...] = jnp.zeros_like(s_ref)

    x = x_ref[...]
    h = _rms(x, g_ref[...]).astype(BF16)
    proj = _dot(h, win_ref[...])
    z = proj[:, :inner]
    xbc = proj[:, inner:]
    ctx_ref[SSM_PAD:SSM_PAD + tm, :] = xbc
    conv_ref[0] = ctx_ref[SSM_PAD + tm - (SSM_CONV - 1):SSM_PAD + tm, :]
    conv = cb_ref[...] + xbc * cw_ref[SSM_CONV - 1:SSM_CONV, :]
    for k in range(SSM_CONV - 1):
        off = SSM_PAD - (SSM_CONV - 1) + k
        conv = conv + ctx_ref[off:off + tm, :] * cw_ref[k:k + 1, :]
    ctx_ref[0:SSM_PAD, :] = ctx_ref[tm:tm + SSM_PAD, :]
    act = _silu(conv)
    xs = act[:, :inner]
    bm = act[:, inner:inner + bc].astype(BF16)
    cm = act[:, inner + bc:].astype(BF16)
    dt = _softplus(_dot(h, wdt_ref[...]) + dtb_ref[...])
    da = dt * a_ref[...]

    rows_i = lax.broadcasted_iota(jnp.int32, (CHUNK, CHUNK), 0)
    cols_i = lax.broadcasted_iota(jnp.int32, (CHUNK, CHUNK), 1)
    causal = rows_i >= cols_i
    tril = causal.astype(F32)
    gw = inner // SSM_GROUPS
    hpg = gw // SSM_HEAD_DIM
    lane_head = lax.broadcasted_iota(jnp.int32, (CHUNK, gw), 1) // SSM_HEAD_DIM
    r_mat = r_ref[...]

    for c in range(tm // CHUNK):
        rows = slice(c * CHUNK, (c + 1) * CHUNK)
        dt_c = dt[rows, :]
        cum = _dot_hi(tril, da[rows, :])
        cum_t = cum.T
        dt_t = dt_c.T
        ecum = jnp.exp(cum)
        e_full = _dot_hi(ecum, r_mat)
        w_in = jnp.exp(cum[CHUNK - 1:CHUNK, :] - cum) * dt_c
        xs_c = xs[rows, :]
        xw = (xs_c * _dot_hi(w_in, r_mat)).astype(BF16)
        xs_b = xs_c.astype(BF16)
        for g in range(SSM_GROUPS):
            gcols = slice(g * gw, (g + 1) * gw)
            ncols = slice(g * SSM_STATE, (g + 1) * SSM_STATE)
            b_g = bm[rows, ncols]
            c_g = cm[rows, ncols]
            cb = _dot_nt(c_g, b_g)
            x_g = xs_b[:, gcols]
            s_g = s_ref[g]
            y_g = _dot_nt(c_g, s_g.astype(BF16)) * e_full[:, gcols]
            for r in range(hpg):
                hd = g * hpg + r
                seg = cum[:, hd:hd + 1] - cum_t[hd:hd + 1, :]
                decay = jnp.where(causal, jnp.exp(jnp.where(causal, seg, 0.0)), 0.0)
                wts = (cb * decay * dt_t[hd:hd + 1, :]).astype(BF16)
                y_g = y_g + _dot(wts, jnp.where(lane_head == r, x_g, jnp.zeros_like(x_g)))
            y_ref[rows, gcols] = y_g
            upd = _dot_tn(xw[:, gcols], b_g)
            for r in range(hpg):
                hd = g * hpg + r
                hrows = slice(r * SSM_HEAD_DIM, (r + 1) * SSM_HEAD_DIM)
                s_ref[g, hrows, :] = s_g[hrows, :] * ecum[CHUNK - 1:CHUNK, hd:hd + 1] + upd[hrows, :]

    out = _ssd_post(y_ref[...], xs, z, dsk_ref[...], ng_ref[...], wout_ref, ynorm_ref, inner)
    o_ref[...] = x + out
    ssm_ref[0] = s_ref[...]


def _mamba_sample_pre_body(x_ref, g_ref, win_ref, wdt_ref, cw_ref, cb_ref, dtb_ref, a_ref, r_ref, cs_ref,
                           z_ref, xs_ref, b_ref, c_ref, xdt_ref, e_ref, newconv_ref, *, inner, bc):
    h = _rms(x_ref[...], g_ref[...]).astype(BF16)
    proj = _dot(h, win_ref[...])
    z_ref[...] = proj[:, :inner]
    xbc = proj[:, inner:]
    conv = cb_ref[...] + xbc * cw_ref[SSM_CONV - 1:SSM_CONV, :]
    for k in range(SSM_CONV - 1):
        conv = conv + cs_ref[k] * cw_ref[k:k + 1, :]
    for k in range(SSM_CONV - 2):
        newconv_ref[k] = cs_ref[k + 1]
    newconv_ref[SSM_CONV - 2] = xbc
    act = _silu(conv)
    xs = act[:, :inner]
    xs_ref[...] = xs
    b_ref[...] = act[:, inner:inner + bc]
    c_ref[...] = act[:, inner + bc:]
    dt = _softplus(_dot(h, wdt_ref[...]) + dtb_ref[...])
    xdt_ref[...] = xs * _dot_hi(dt, r_ref[...])
    e_ref[...] = jnp.exp(dt * a_ref[...])


def _mamba_sample_state_body(e_ref, h0_ref, xdt_t_ref, b_ref, c_ref, hn_ref, y_ref, *, bt, inner, heads):
    i = pl.program_id(0)
    gw = inner // SSM_GROUPS
    hpg = gw // SSM_HEAD_DIM
    row_id = lax.broadcasted_iota(jnp.int32, (bt, bt * SSM_STATE), 0)
    lane_tok = lax.broadcasted_iota(jnp.int32, (bt, bt * SSM_STATE), 1) // SSM_STATE
    for g in range(SSM_GROUPS):
        rows = slice(g * gw, (g + 1) * gw)
        ncols = slice(g * SSM_STATE, (g + 1) * SSM_STATE)
        states = []
        for j in range(bt):
            b_row = b_ref[0, j:j + 1, ncols]
            for r in range(hpg):
                hrows = slice(g * gw + r * SSM_HEAD_DIM, g * gw + (r + 1) * SSM_HEAD_DIM)
                e = e_ref[(i * bt + j) * heads + g * hpg + r]
                hn_ref[j, hrows, :] = h0_ref[j, hrows, :] * e + xdt_t_ref[0, hrows, j:j + 1] * b_row
            states.append(hn_ref[j, rows, :].astype(BF16))
        c_g = c_ref[0, :, ncols]
        c_diag = jnp.where(row_id == lane_tok, jnp.concatenate([c_g] * bt, axis=1), 0.0).astype(BF16)
        y_ref[:, rows] = _dot_nt(c_diag, jnp.concatenate(states, axis=1))


def _mamba_sample_post_body(x_ref, y_ref, xs_ref, z_ref, dsk_ref, ng_ref, wout_ref, o_ref, ynorm_ref, *, inner):
    out = _ssd_post(y_ref[...], xs_ref[...], z_ref[...], dsk_ref[...], ng_ref[...], wout_ref, ynorm_ref, inner)
    o_ref[...] = x_ref[...] + out


def _mamba(X, n_prompt, n_batch, state_ssm_all, layer, state_conv, g, w_in, conv_w, conv_b, dt_bias, a_log, d_skip,
           norm_g, w_out, *, tm, bt):
    M, D = X.shape
    inner = w_out.shape[0]
    heads = a_log.shape[0]
    conv_dim = conv_w.shape[1]
    bc = (conv_dim - inner) // 2
    n_sample = M - n_prompt
    seq = n_prompt // n_batch
    tiles_per_seq = seq // tm
    gw = inner // SSM_GROUPS
    L = LANES_V7X

    g2 = g.reshape(1, D)
    w_main = w_in[:, :inner + conv_dim].astype(BF16)
    w_dt = jnp.zeros((D, L), F32).at[:, :heads].set(w_in[:, inner + conv_dim:]).astype(BF16)
    cb2 = conv_b.reshape(1, conv_dim)
    dtb = jnp.zeros((1, L), F32).at[0, :heads].set(dt_bias)
    a_neg = jnp.zeros((1, L), F32).at[0, :heads].set(-jnp.exp(a_log))
    dsk = jnp.repeat(d_skip, SSM_HEAD_DIM).reshape(1, inner)
    ng2 = norm_g.reshape(1, inner)
    w_out_b = w_out.astype(BF16)
    r_mat = (jnp.arange(L, dtype=jnp.int32)[:, None]
             == (jnp.arange(inner, dtype=jnp.int32) // SSM_HEAD_DIM)[None, :]).astype(F32)

    X, conv_p, ssm_p = pl.pallas_call(
        functools.partial(_mamba_prompt_body, tm=tm, tiles_per_seq=tiles_per_seq, inner=inner, bc=bc),
        grid=(n_prompt // tm,),
        in_specs=[
            pl.BlockSpec((tm, D), lambda i: (i, 0)),
            _const_spec((1, D)),
            _const_spec((D, inner + conv_dim)),
            _const_spec((D, L)),
            _const_spec((SSM_CONV, conv_dim)),
            _const_spec((1, conv_dim)),
            _const_spec((1, L)),
            _const_spec((1, L)),
            _const_spec((1, inner)),
            _const_spec((1, inner)),
            _const_spec((inner, D)),
            _const_spec((L, inner)),
        ],
        out_specs=[
            pl.BlockSpec((tm, D), lambda i: (i, 0)),
            pl.BlockSpec((1, SSM_CONV - 1, conv_dim), lambda i: (i // tiles_per_seq, 0, 0)),
            pl.BlockSpec((1, SSM_GROUPS, gw, SSM_STATE), lambda i: (i // tiles_per_seq, 0, 0, 0)),
        ],
        out_shape=[
            jax.ShapeDtypeStruct((M, D), F32),
            jax.ShapeDtypeStruct((n_batch, SSM_CONV - 1, conv_dim), F32),
            jax.ShapeDtypeStruct((n_batch, SSM_GROUPS, gw, SSM_STATE), F32),
        ],
        scratch_shapes=[
            pltpu.VMEM((SSM_PAD + tm, conv_dim), F32),
            pltpu.VMEM((SSM_GROUPS, gw, SSM_STATE), F32),
            pltpu.VMEM((tm, inner), F32),
            pltpu.VMEM((tm, inner), BF16),
        ],
        input_output_aliases={0: 0},
        compiler_params=_cparams(("arbitrary",)),
        name="mamba_prompt",
    )(X, g2, w_main, w_dt, conv_w, cb2, dtb, a_neg, dsk, ng2, w_out_b, r_mat)

    sblk = n_prompt // n_sample
    cs_t = jnp.transpose(state_conv, (1, 0, 2))
    z, xs, b_m, c_m, xdt, e_tok, newconv_t = pl.pallas_call(
        functools.partial(_mamba_sample_pre_body, inner=inner, bc=bc),
        grid=(1,),
        in_specs=[
            pl.BlockSpec((n_sample, D), lambda i: (sblk, 0)),
            _const_spec((1, D)),
            _const_spec((D, inner + conv_dim)),
            _const_spec((D, L)),
            _const_spec((SSM_CONV, conv_dim)),
            _const_spec((1, conv_dim)),
            _const_spec((1, L)),
            _const_spec((1, L)),
            _const_spec((L, inner)),
            _const_spec((SSM_CONV - 1, n_sample, conv_dim)),
        ],
        out_specs=[
            pl.BlockSpec((n_sample, inner), lambda i: (0, 0)),
            pl.BlockSpec((n_sample, inner), lambda i: (0, 0)),
            pl.BlockSpec((n_sample, bc), lambda i: (0, 0)),
            pl.BlockSpec((n_sample, bc), lambda i: (0, 0)),
            pl.BlockSpec((n_sample, inner), lambda i: (0, 0)),
            pl.BlockSpec((n_sample, L), lambda i: (0, 0)),
            pl.BlockSpec((SSM_CONV - 1, n_sample, conv_dim), lambda i: (0, 0, 0)),
        ],
        out_shape=[
            jax.ShapeDtypeStruct((n_sample, inner), F32),
            jax.ShapeDtypeStruct((n_sample, inner), F32),
            jax.ShapeDtypeStruct((n_sample, bc), F32),
            jax.ShapeDtypeStruct((n_sample, bc), F32),
            jax.ShapeDtypeStruct((n_sample, inner), F32),
            jax.ShapeDtypeStruct((n_sample, L), F32),
            jax.ShapeDtypeStruct((SSM_CONV - 1, n_sample, conv_dim), F32),
        ],
        compiler_params=_cparams(("arbitrary",)),
        name="mamba_sample_pre",
    )(X, g2, w_main, w_dt, conv_w, cb2, dtb, a_neg, r_mat, cs_t)
    conv_s = jnp.transpose(newconv_t, (1, 0, 2))

    nblk = n_sample // bt

    def to_cols(arr):
        return jnp.transpose(arr.reshape(nblk, bt, inner), (0, 2, 1))

    h0 = state_ssm_all.reshape(state_ssm_all.shape[0], n_sample, inner, SSM_STATE)
    h_new, y_s = pl.pallas_call(
        functools.partial(_mamba_sample_state_body, bt=bt, inner=inner, heads=heads),
        grid_spec=pltpu.PrefetchScalarGridSpec(
            num_scalar_prefetch=1,
            grid=(nblk,),
            in_specs=[
                pl.BlockSpec((None, bt, inner, SSM_STATE), lambda i, e: (layer, i, 0, 0)),
                pl.BlockSpec((1, inner, bt), lambda i, e: (i, 0, 0)),
                pl.BlockSpec((1, bt, bc), lambda i, e: (i, 0, 0)),
                pl.BlockSpec((1, bt, bc), lambda i, e: (i, 0, 0)),
            ],
            out_specs=[
                pl.BlockSpec((bt, inner, SSM_STATE), lambda i, e: (i, 0, 0)),
                pl.BlockSpec((bt, inner), lambda i, e: (i, 0)),
            ],
        ),
        out_shape=[
            jax.ShapeDtypeStruct((n_sample, inner, SSM_STATE), F32),
            jax.ShapeDtypeStruct((n_sample, inner), F32),
        ],
        compiler_params=_cparams(("arbitrary",)),
        name="mamba_sample_state",
    )(e_tok[:, :heads].reshape(n_sample * heads), h0, to_cols(xdt), b_m.reshape(nblk, bt, bc),
      c_m.reshape(nblk, bt, bc))

    X = pl.pallas_call(
        functools.partial(_mamba_sample_post_body, inner=inner),
        grid=(1,),
        in_specs=[
            pl.BlockSpec((n_sample, D), lambda i: (sblk, 0)),
            pl.BlockSpec((n_sample, inner), lambda i: (0, 0)),
            pl.BlockSpec((n_sample, inner), lambda i: (0, 0)),
            pl.BlockSpec((n_sample, inner), lambda i: (0, 0)),
            _const_spec((1, inner)),
            _const_spec((1, inner)),
            _const_spec((inner, D)),
        ],
        out_specs=pl.BlockSpec((n_sample, D), lambda i: (sblk, 0)),
        out_shape=jax.ShapeDtypeStruct((M, D), F32),
        scratch_shapes=[pltpu.VMEM((n_sample, inner), BF16)],
        input_output_aliases={0: 0},
        compiler_params=_cparams(("arbitrary",)),
        name="mamba_sample_post",
    )(X, y_s, xs, z, dsk, ng2, w_out_b)

    ssm_p = ssm_p.reshape(n_batch, heads, SSM_HEAD_DIM, SSM_STATE)
    ssm_s = h_new.reshape(n_sample, heads, SSM_HEAD_DIM, SSM_STATE)
    return X, conv_p, conv_s, ssm_p, ssm_s


def _causal_dwconv_tile(ctx_ref, dw_ref, acc_ref, tm, d):
    S = SUBLANES_V7X
    first = CFM_PAD - (CFM_KERNEL - 1)
    rb, lb = CONV_ROW_BLOCK, CONV_LANE_BLOCK
    for r0 in range(0, tm, rb):
        for l0 in range(0, d, lb):
            lanes = slice(l0, l0 + lb)
            y = None
            for b in range(S):
                pb = None
                for a in range((first + CFM_KERNEL - 1) // S + 1):
                    k = S * a + b - first
                    if 0 <= k < CFM_KERNEL:
                        term = ctx_ref[r0 + S * a:r0 + S * a + rb + S, lanes] * dw_ref[k:k + 1, lanes]
                        pb = term if pb is None else pb + term
                part = pb[b:b + rb, :]
                y = part if y is None else y + part
            acc_ref[r0:r0 + rb, lanes] = y


def _cfm_prompt_body(x_ref, g_ref, w1_ref, b1_ref, dw_ref, dwb_ref, lng_ref, lnb_ref, w2_ref, b2_ref,
                     o_ref, buf_ref, ctx_ref, acc_ref, *, tm, tiles_per_seq, d):
    i = pl.program_id(0)

    @pl.when(i % tiles_per_seq == 0)
    def _():
        ctx_ref[0:CFM_PAD, :] = jnp.zeros((CFM_PAD, d), F32)
        ctx_ref[CFM_PAD + tm:CFM_PAD + tm + SUBLANES_V7X, :] = jnp.zeros((SUBLANES_V7X, d), F32)

    x = x_ref[...]
    h = _rms(x, g_ref[...]).astype(BF16)
    a = _dot(h, w1_ref[...]) + b1_ref[...]
    ctx_ref[CFM_PAD:CFM_PAD + tm, :] = a[:, :d] * _sigmoid(a[:, d:])
    _causal_dwconv_tile(ctx_ref, dw_ref, acc_ref, tm, d)
    buf_ref[0] = ctx_ref[CFM_PAD + tm - (CFM_KERNEL - 1):CFM_PAD + tm, :]
    ctx_ref[0:CFM_PAD, :] = ctx_ref[tm:tm + CFM_PAD, :]
    acc = acc_ref[...] + dwb_ref[...]
    hc = _silu(_layernorm(acc, lng_ref[...], lnb_ref[...])).astype(BF16)
    o_ref[...] = x + _dot(hc, w2_ref[...]) + b2_ref[...]


def _cfm_sample_body(x_ref, g_ref, w1_ref, b1_ref, dw_ref, dwb_ref, lng_ref, lnb_ref, w2_ref, b2_ref, cs_ref,
                     o_ref, new_ref, *, d):
    x = x_ref[...]
    h = _rms(x, g_ref[...]).astype(BF16)
    a = _dot(h, w1_ref[...]) + b1_ref[...]
    glu = a[:, :d] * _sigmoid(a[:, d:])
    acc = dwb_ref[...] + glu * dw_ref[CFM_KERNEL - 1:CFM_KERNEL, :]
    for k in range(CFM_KERNEL - 1):
        acc = acc + cs_ref[k] * dw_ref[k:k + 1, :]
    for k in range(CFM_KERNEL - 2):
        new_ref[k] = cs_ref[k + 1]
    new_ref[CFM_KERNEL - 2] = glu
    hc = _silu(_layernorm(acc, lng_ref[...], lnb_ref[...])).astype(BF16)
    o_ref[...] = x + _dot(hc, w2_ref[...]) + b2_ref[...]


def _conformer(X, n_prompt, n_batch, state_conv, g, w_pw1, b_pw1, dw_w, dw_b, ln_g, ln_b, w_pw2, b_pw2,
               *, tm, bt):
    M, D = X.shape
    n_sample = M - n_prompt
    seq = n_prompt // n_batch
    tiles_per_seq = seq // tm
    K = CFM_KERNEL
    g2 = g.reshape(1, D)
    w1, w2 = w_pw1.astype(BF16), w_pw2.astype(BF16)
    b1, b2 = b_pw1.reshape(1, 2 * D), b_pw2.reshape(1, D)
    dwb, lng, lnb = dw_b.reshape(1, D), ln_g.reshape(1, D), ln_b.reshape(1, D)
    weight_specs = [
        _const_spec((1, D)), _const_spec((D, 2 * D)), _const_spec((1, 2 * D)), _const_spec((K, D)),
        _const_spec((1, D)), _const_spec((1, D)), _const_spec((1, D)), _const_spec((D, D)), _const_spec((1, D)),
    ]
    weights = (g2, w1, b1, dw_w, dwb, lng, lnb, w2, b2)

    X, buf_p = pl.pallas_call(
        functools.partial(_cfm_prompt_body, tm=tm, tiles_per_seq=tiles_per_seq, d=D),
        grid=(n_prompt // tm,),
        in_specs=[pl.BlockSpec((tm, D), lambda i: (i, 0))] + weight_specs,
        out_specs=[
            pl.BlockSpec((tm, D), lambda i: (i, 0)),
            pl.BlockSpec((1, K - 1, D), lambda i: (i // tiles_per_seq, 0, 0)),
        ],
        out_shape=[jax.ShapeDtypeStruct((M, D), F32), jax.ShapeDtypeStruct((n_batch, K - 1, D), F32)],
        scratch_shapes=[pltpu.VMEM((CFM_PAD + tm + SUBLANES_V7X, D), F32), pltpu.VMEM((tm, D), F32)],
        input_output_aliases={0: 0},
        compiler_params=_cparams(("arbitrary",)),
        name="conformer_prompt",
    )(X, *weights)

    first = n_prompt // bt
    cs_t = jnp.transpose(state_conv, (1, 0, 2))
    X, new_t = pl.pallas_call(
        functools.partial(_cfm_sample_body, d=D),
        grid=(n_sample // bt,),
        in_specs=[pl.BlockSpec((bt, D), lambda i: (i + first, 0))] + weight_specs
        + [pl.BlockSpec((K - 1, bt, D), lambda i: (0, i, 0))],
        out_specs=[
            pl.BlockSpec((bt, D), lambda i: (i + first, 0)),
            pl.BlockSpec((K - 1, bt, D), lambda i: (0, i, 0)),
        ],
        out_shape=[jax.ShapeDtypeStruct((M, D), F32), jax.ShapeDtypeStruct((K - 1, n_sample, D), F32)],
        input_output_aliases={0: 0},
        compiler_params=_cparams(("arbitrary",)),
        name="conformer_sample",
    )(X, *weights, cs_t)
    return X, buf_p, jnp.transpose(new_t, (1, 0, 2))


def _stack(parts):
    return parts[0][None] if len(parts) == 1 else jnp.stack(parts)


def kernel(x_prompt, x_sample, state_ssm, state_conv_ssm, state_conv_cfm, norm_mix_g, norm_ffn_g, norm_final_g, a_w_in, a_b_in, a_ln_g, a_ln_b, a_w_s, a_b_s, a_w_out, b_w_in, b_conv_w, b_conv_b, b_dt_bias, b_a_log, b_d, b_norm_g, b_w_out, c_w_pw1, c_b_pw1, c_dw_w, c_dw_b, c_ln_g, c_ln_b, c_w_pw2, c_b_pw2, f_w_gate, f_w_up, f_w_down, e_w_router, e_w_gate, e_w_up, e_w_down):
    n_batch, seq, D = x_prompt.shape
    n_sample = x_sample.shape[0]
    n_prompt = n_batch * seq
    depth = norm_mix_g.shape[0]
    d_ff = f_w_gate.shape[2]
    assert x_sample.shape[1] == 1 and n_prompt % n_sample == 0 and seq % CHUNK == 0

    X = jnp.concatenate([x_prompt.reshape(n_prompt, D), x_sample.reshape(n_sample, D)], axis=0)
    v_p, v_s, ssm_p, ssm_s, cs_p, cs_s, cc_p, cc_s = [], [], [], [], [], [], [], []
    for i in range(depth):
        kind, j = i % 3, i // 3
        if kind == 0:
            X, vp, vs = _gmlp(X, n_prompt, n_batch, norm_mix_g[i], a_w_in[j], a_b_in[j], a_ln_g[j], a_ln_b[j],
                              a_w_s[j], a_b_s[j], a_w_out[j], tm=512)
            v_p.append(vp)
            v_s.append(vs)
        elif kind == 1:
            X, cbp, cbs, hlp, hls = _mamba(X, n_prompt, n_batch, state_ssm, j, state_conv_ssm[j], norm_mix_g[i],
                                           b_w_in[j], b_conv_w[j], b_conv_b[j], b_dt_bias[j], b_a_log[j], b_d[j],
                                           b_norm_g[j], b_w_out[j], tm=256, bt=8)
            cs_p.append(cbp)
            cs_s.append(cbs)
            ssm_p.append(hlp)
            ssm_s.append(hls)
        else:
            X, cbp, cbs = _conformer(X, n_prompt, n_batch, state_conv_cfm[j], norm_mix_g[i], c_w_pw1[j], c_b_pw1[j],
                                     c_dw_w[j], c_dw_b[j], c_ln_g[j], c_ln_b[j], c_w_pw2[j], c_b_pw2[j],
                                     tm=512, bt=32)
            cc_p.append(cbp)
            cc_s.append(cbs)
        k = i // 2
        if i % 2 == 0:
            X = _ffn_dense(X, n_prompt, norm_ffn_g[i], f_w_gate[k], f_w_up[k], f_w_down[k],
                           tm=512, f_chunk=d_ff // 2)
        else:
            X = _moe(X, n_prompt, norm_ffn_g[i], e_w_router[k], e_w_gate[k], e_w_up[k], e_w_down[k], norm_final_g,
                     tm_tok=384, tm_out=512, tm=528, f_chunk=256, final_norm=(i == depth - 1))
    assert depth % 2 == 0
    y_prompt, y_sample = X
    y_prompt = y_prompt.reshape(n_batch, seq, D)
    y_sample = y_sample.reshape(n_sample, 1, D)
    return (y_prompt, y_sample, _stack(v_p), _stack(v_s), _stack(ssm_p), _stack(ssm_s),
            _stack(cs_p), _stack(cs_s), _stack(cc_p), _stack(cc_s))
```

```python
import functools

import jax
import jax.numpy as jnp
from jax import lax
from jax.experimental import pallas as pl
from jax.experimental.pallas import tpu as pltpu

F32 = jnp.float32
BF16 = jnp.bfloat16
EPS = 1e-6
HIGHEST = lax.Precision.HIGHEST

LANES_V7X = 128
SUBLANES_V7X = 8
VMEM_LIMIT_V7X = 60 * 1024 * 1024

CHUNK = 128
GM_GROUPS = 8
SSM_GROUPS = 8
SSM_HEAD_DIM = 64
SSM_STATE = 128
SSM_CONV = 4
CFM_KERNEL = 31
N_EXPERTS = 8
CFM_PAD = 32
SSM_PAD = 8
CONV_ROW_BLOCK = 128
CONV_LANE_BLOCK = 256


def _cparams(sem):
    return pltpu.CompilerParams(dimension_semantics=sem, vmem_limit_bytes=VMEM_LIMIT_V7X)


def _const_spec(shape):
    nd = len(shape)
    return pl.BlockSpec(shape, lambda *_: (0,) * nd, pipeline_mode=pl.Buffered(1))


def _dot(a, b):
    return jnp.dot(a, b, preferred_element_type=F32)


def _dot_nt(a, b):
    return lax.dot_general(a, b, (((1,), (1,)), ((), ())), preferred_element_type=F32)


def _dot_tn(a, b):
    return lax.dot_general(a, b, (((0,), (0,)), ((), ())), preferred_element_type=F32)


def _dot_hi(a, b):
    return jnp.dot(a, b, precision=HIGHEST, preferred_element_type=F32)


def _split_bf16(a):
    hi = a.astype(BF16)
    return hi, (a - hi.astype(F32)).astype(BF16)


def _dot_split_lhs(a, b_bf16):
    hi, lo = _split_bf16(a)
    return _dot(hi, b_bf16) + _dot(lo, b_bf16)


def _rms(x, g):
    return x * lax.rsqrt(jnp.mean(x * x, axis=-1, keepdims=True) + EPS) * g


def _layernorm(x, g, b):
    xc = x - jnp.mean(x, axis=-1, keepdims=True)
    return xc * lax.rsqrt(jnp.mean(xc * xc, axis=-1, keepdims=True) + EPS) * g + b


def _sigmoid(x):
    return 1.0 / (1.0 + jnp.exp(-x))


def _silu(x):
    return x * _sigmoid(x)


def _gelu(x):
    return 0.5 * x * (1.0 + lax.erf(x * (2.0 ** -0.5)))


def _softplus(x):
    return jnp.maximum(x, 0.0) + jnp.log(1.0 + jnp.exp(-jnp.abs(x)))


def _gmlp_prompt_body(x_ref, g_ref, win_ref, bin_ref, lng_ref, lnb_ref, ws_ref, bs_ref, wout_ref,
                      o_ref, v_ref, gated_ref, *, tm, width):
    x = x_ref[...]
    h = _rms(x, g_ref[...]).astype(BF16)
    u = _gelu(_dot(h, win_ref[:, :width]) + bin_ref[:, :width])
    hv = _gelu(_dot(h, win_ref[:, width:]) + bin_ref[:, width:])
    v = _layernorm(hv, lng_ref[...], lnb_ref[...])
    v_ref[0] = v[tm - CHUNK:, :]
    vb = v.astype(BF16)
    gdim = width // GM_GROUPS
    causal = (lax.broadcasted_iota(jnp.int32, (CHUNK, CHUNK), 0)
              >= lax.broadcasted_iota(jnp.int32, (CHUNK, CHUNK), 1))
    for g in range(GM_GROUPS):
        wc = jnp.where(causal, ws_ref[g], 0.0).astype(BF16)
        bias = bs_ref[:, g:g + 1]
        for c in range(tm // CHUNK):
            rows = slice(c * CHUNK, (c + 1) * CHUNK)
            cols = slice(g * gdim, (g + 1) * gdim)
            s = _dot(wc, vb[rows, cols]) + bias
            gated_ref[rows, cols] = (u[rows, cols] * s).astype(BF16)
    o_ref[...] = x + _dot(gated_ref[...], wout_ref[...])


def _gmlp_sample_body(x_ref, g_ref, win_ref, bin_ref, lng_ref, lnb_ref, ws0_ref, bs0_ref, wout_ref,
                      o_ref, v_ref, *, width):
    x = x_ref[...]
    h = _rms(x, g_ref[...]).astype(BF16)
    u = _gelu(_dot(h, win_ref[:, :width]) + bin_ref[:, :width])
    hv = _gelu(_dot(h, win_ref[:, width:]) + bin_ref[:, width:])
    v = _layernorm(hv, lng_ref[...], lnb_ref[...])
    v_ref[...] = v
    s = v * ws0_ref[...] + bs0_ref[...]
    o_ref[...] = x + _dot((u * s).astype(BF16), wout_ref[...])


def _gmlp(X, n_prompt, n_batch, g, w_in, b_in, ln_g, ln_b, w_s, b_s, w_out, *, tm):
    M, D = X.shape
    width = w_out.shape[0]
    seq = n_prompt // n_batch
    tiles_per_seq = seq // tm
    n_sample = M - n_prompt
    g2 = g.reshape(1, D)
    b_in2 = b_in.reshape(1, 2 * width)
    ln_g2, ln_b2 = ln_g.reshape(1, width), ln_b.reshape(1, width)
    w_in_b, w_out_b = w_in.astype(BF16), w_out.astype(BF16)

    X, v_p = pl.pallas_call(
        functools.partial(_gmlp_prompt_body, tm=tm, width=width),
        grid=(n_prompt // tm,),
        in_specs=[
            pl.BlockSpec((tm, D), lambda i: (i, 0)),
            _const_spec((1, D)),
            _const_spec((D, 2 * width)),
            _const_spec((1, 2 * width)),
            _const_spec((1, width)),
            _const_spec((1, width)),
            _const_spec((GM_GROUPS, CHUNK, CHUNK)),
            _const_spec((CHUNK, GM_GROUPS)),
            _const_spec((width, D)),
        ],
        out_specs=[
            pl.BlockSpec((tm, D), lambda i: (i, 0)),
            pl.BlockSpec((1, CHUNK, width), lambda i: (i // tiles_per_seq, 0, 0)),
        ],
        out_shape=[jax.ShapeDtypeStruct((M, D), F32),
                   jax.ShapeDtypeStruct((n_batch, CHUNK, width), F32)],
        scratch_shapes=[pltpu.VMEM((tm, width), BF16)],
        input_output_aliases={0: 0},
        compiler_params=_cparams(("arbitrary",)),
        name="gmlp_prompt",
    )(X, g2, w_in_b, b_in2, ln_g2, ln_b2, w_s, b_s.T, w_out_b)

    gdim = width // GM_GROUPS
    ws0 = jnp.repeat(w_s[:, 0, 0], gdim).reshape(1, width)
    bs0 = jnp.repeat(b_s[:, 0], gdim).reshape(1, width)
    sblk = n_prompt // n_sample
    X, v_s = pl.pallas_call(
        functools.partial(_gmlp_sample_body, width=width),
        grid=(1,),
        in_specs=[
            pl.BlockSpec((n_sample, D), lambda i: (sblk, 0)),
            _const_spec((1, D)),
            _const_spec((D, 2 * width)),
            _const_spec((1, 2 * width)),
            _const_spec((1, width)),
            _const_spec((1, width)),
            _const_spec((1, width)),
            _const_spec((1, width)),
            _const_spec((width, D)),
        ],
        out_specs=[
            pl.BlockSpec((n_sample, D), lambda i: (sblk, 0)),
            pl.BlockSpec((n_sample, width), lambda i: (0, 0)),
        ],
        out_shape=[jax.ShapeDtypeStruct((M, D), F32),
                   jax.ShapeDtypeStruct((n_sample, width), F32)],
        input_output_aliases={0: 0},
        compiler_params=_cparams(("arbitrary",)),
        name="gmlp_sample",
    )(X, g2, w_in_b, b_in2, ln_g2, ln_b2, ws0, bs0, w_out_b)
    return X, v_p, v_s.reshape(n_sample, 1, width)


def _swiglu_tile(h, wg_ref, wu_ref, wd_ref, f_chunk):
    d_ff = wd_ref.shape[0]
    acc = None
    for f in range(d_ff // f_chunk):
        cols = slice(f * f_chunk, (f + 1) * f_chunk)
        a = _dot(h, wg_ref[:, cols])
        b = _dot(h, wu_ref[:, cols])
        part = _dot((_silu(a) * b).astype(BF16), wd_ref[cols, :])
        acc = part if acc is None else acc + part
    return acc


def _ffn_dense_body(x_ref, g_ref, wg_ref, wu_ref, wd_ref, o_ref, *, f_chunk):
    x = x_ref[...]
    h = _rms(x, g_ref[...]).astype(BF16)
    o_ref[...] = x + _swiglu_tile(h, wg_ref, wu_ref, wd_ref, f_chunk)


def _ffn_dense_rows(X, g2, wg, wu, wd, *, tm, first_block, n_tiles, f_chunk):
    M, D = X.shape
    d_ff = wd.shape[0]
    return pl.pallas_call(
        functools.partial(_ffn_dense_body, f_chunk=f_chunk),
        grid=(n_tiles,),
        in_specs=[
            pl.BlockSpec((tm, D), lambda i: (i + first_block, 0)),
            _const_spec((1, D)),
            _const_spec((D, d_ff)),
            _const_spec((D, d_ff)),
            _const_spec((d_ff, D)),
        ],
        out_specs=pl.BlockSpec((tm, D), lambda i: (i + first_block, 0)),
        out_shape=jax.ShapeDtypeStruct((M, D), F32),
        input_output_aliases={0: 0},
        compiler_params=_cparams(("arbitrary",)),
        name="ffn_dense",
    )(X, g2, wg, wu, wd)


def _ffn_dense(X, n_prompt, g, w_gate, w_up, w_down, *, tm, f_chunk):
    M, D = X.shape
    n_sample = M - n_prompt
    g2 = g.reshape(1, D)
    wg, wu, wd = w_gate.astype(BF16), w_up.astype(BF16), w_down.astype(BF16)
    X = _ffn_dense_rows(X, g2, wg, wu, wd, tm=tm, first_block=0, n_tiles=n_prompt // tm, f_chunk=f_chunk)
    X = _ffn_dense_rows(X, g2, wg, wu, wd, tm=n_sample, first_block=n_prompt // n_sample, n_tiles=1,
                        f_chunk=f_chunk)
    return X


def _router_body(x_ref, g_ref, wr_hi_ref, wr_lo_ref, idx_ref, wts_ref):
    h_hi, h_lo = _split_bf16(_rms(x_ref[...], g_ref[...]))
    logits = _dot(h_hi, wr_hi_ref[...]) + (_dot(h_lo, wr_hi_ref[...]) + _dot(h_hi, wr_lo_ref[...]))
    lane = lax.broadcasted_iota(jnp.int32, logits.shape, 1)
    neg = jnp.float32(-jnp.inf)
    logits = jnp.where(lane < N_EXPERTS, logits, neg)
    m1 = jnp.max(logits, axis=-1, keepdims=True)
    i1 = jnp.min(jnp.where(logits == m1, lane, LANES_V7X), axis=-1, keepdims=True)
    rest = jnp.where(lane == i1, neg, logits)
    m2 = jnp.max(rest, axis=-1, keepdims=True)
    i2 = jnp.min(jnp.where(rest == m2, lane, LANES_V7X), axis=-1, keepdims=True)
    e = jnp.exp(m2 - m1)
    w1 = 1.0 / (1.0 + e)
    w2 = e / (1.0 + e)
    idx_ref[...] = jnp.where(lane == 0, i1, jnp.where(lane == 1, i2, 0))
    wts_ref[...] = jnp.where(lane == 0, w1, jnp.where(lane == 1, w2, 0.0))


def _slot_table_body(pos1_ref, pos2_ref, pad_lo_ref, pad_hi_ref, tok_ref, *, n_tokens, n_ranges):
    def pad(s, carry):
        tok_ref[s] = 0
        return carry

    for r in range(n_ranges):
        lax.fori_loop(pad_lo_ref[r], pad_hi_ref[r], pad, 0)

    def place(t, carry):
        tok_ref[pos1_ref[t]] = t
        tok_ref[pos2_ref[t]] = t
        return carry

    lax.fori_loop(0, n_tokens, place, 0, unroll=8)


def _issue_row_gather(src_hbm, idx_ref, base, buf, sem, n_rows):
    def body(r, carry):
        pltpu.make_async_copy(src_hbm.at[pl.ds(idx_ref[base + r], 1), :], buf.at[pl.ds(r, 1), :], sem).start()
        return carry
    lax.fori_loop(0, n_rows, body, 0, unroll=8)


def _wait_row_gather(src_hbm, buf, sem, n_rows):
    pltpu.make_async_copy(src_hbm.at[pl.ds(0, n_rows), :], buf, sem).wait()


def _moe_ffn_body(tile_expert_ref, n_tiles_ref, tok_ref, x_hbm, g_ref, wg_ref, wu_ref, wd_ref,
                  y_ref, buf_ref, sem_ref, *, tm, f_chunk):
    t = pl.program_id(0)
    n_tiles = n_tiles_ref[0]
    slot = t % 2

    @pl.when(t == 0)
    def _():
        _issue_row_gather(x_hbm, tok_ref, 0, buf_ref.at[0], sem_ref.at[0], tm)

    @pl.when(t + 1 < n_tiles)
    def _():
        _issue_row_gather(x_hbm, tok_ref, (t + 1) * tm, buf_ref.at[1 - slot], sem_ref.at[1 - slot], tm)

    @pl.when(t < n_tiles)
    def _():
        _wait_row_gather(x_hbm, buf_ref.at[slot], sem_ref.at[slot], tm)
        h = _rms(buf_ref[slot], g_ref[...]).astype(BF16)
        y_ref[...] = _swiglu_tile(h, wg_ref.at[0], wu_ref.at[0], wd_ref.at[0], f_chunk)

    @pl.when(t >= n_tiles)
    def _():
        y_ref[...] = jnp.zeros_like(y_ref)


def _moe_combine_body(p1_ref, p2_ref, x_ref, wts_ref, y_hbm, gf_ref, o_ref, buf1_ref, buf2_ref, sem_ref,
                      *, tm, row0, n_steps, final_norm):
    i = pl.program_id(0)
    slot = i % 2

    def issue(step, s):
        _issue_row_gather(y_hbm, p1_ref, row0 + step * tm, buf1_ref.at[s], sem_ref.at[0, s], tm)
        _issue_row_gather(y_hbm, p2_ref, row0 + step * tm, buf2_ref.at[s], sem_ref.at[1, s], tm)

    @pl.when(i == 0)
    def _():
        issue(0, 0)

    @pl.when(i + 1 < n_steps)
    def _():
        issue(i + 1, 1 - slot)

    _wait_row_gather(y_hbm, buf1_ref.at[slot], sem_ref.at[0, slot], tm)
    _wait_row_gather(y_hbm, buf2_ref.at[slot], sem_ref.at[1, slot], tm)
    out = x_ref[...] + (wts_ref[:, 0:1] * buf1_ref[slot] + wts_ref[:, 1:2] * buf2_ref[slot])
    if final_norm:
        out = _rms(out, gf_ref[...])
    o_ref[...] = out


def _moe(X, n_prompt, g, w_router, w_gate, w_up, w_down, g_final, *, tm_tok, tm_out, tm, f_chunk, final_norm):
    M, D = X.shape
    E = N_EXPERTS
    d_ff = w_down.shape[1]
    g2 = g.reshape(1, D)
    L = LANES_V7X
    wr = jnp.zeros((D, L), F32).at[:, :E].set(w_router)
    wr_hi = wr.astype(BF16)
    wr_lo = (wr - wr_hi.astype(F32)).astype(BF16)

    idx, wts = pl.pallas_call(
        _router_body,
        grid=(M // tm_tok,),
        in_specs=[pl.BlockSpec((tm_tok, D), lambda i: (i, 0)), _const_spec((1, D)), _const_spec((D, L)),
                  _const_spec((D, L))],
        out_specs=[pl.BlockSpec((tm_tok, L), lambda i: (i, 0)), pl.BlockSpec((tm_tok, L), lambda i: (i, 0))],
        out_shape=[jax.ShapeDtypeStruct((M, L), jnp.int32), jax.ShapeDtypeStruct((M, L), F32)],
        compiler_params=_cparams(("arbitrary",)),
        name="moe_router",
    )(X, g2, wr_hi, wr_lo)

    i1, i2 = idx[:, 0], idx[:, 1]
    eids = jnp.arange(E, dtype=jnp.int32)
    sel = ((i1[:, None] == eids) | (i2[:, None] == eids)).astype(jnp.int32)
    count = jnp.sum(sel, axis=0)
    tiles_e = (count + tm - 1) // tm
    tiles_end = jnp.cumsum(tiles_e)
    n_tiles = tiles_end[-1]
    start_e = (tiles_end - tiles_e) * tm
    pos = start_e[None, :] + jnp.cumsum(sel, axis=0) - sel
    pos1 = jnp.sum(jnp.where(i1[:, None] == eids, pos, 0), axis=1).astype(jnp.int32)
    pos2 = jnp.sum(jnp.where(i2[:, None] == eids, pos, 0), axis=1).astype(jnp.int32)
    max_tiles = (2 * M + E * (tm - 1)) // tm
    n_slots = max_tiles * tm
    pad_lo = jnp.concatenate([start_e + count, (n_tiles * tm)[None]]).astype(jnp.int32)
    pad_hi = jnp.concatenate([tiles_end * tm, jnp.array([n_slots], jnp.int32)]).astype(jnp.int32)
    tile_ids = jnp.arange(max_tiles, dtype=jnp.int32)
    tile_expert = jnp.sum((jnp.minimum(tile_ids, n_tiles - 1)[:, None] >= tiles_end[None, :]).astype(jnp.int32),
                          axis=1)
    tile_expert = jnp.minimum(tile_expert, E - 1).astype(jnp.int32)
    n_tiles_arr = n_tiles.astype(jnp.int32).reshape(1)

    smem = pl.BlockSpec(memory_space=pltpu.SMEM)
    tok_of_slot = pl.pallas_call(
        functools.partial(_slot_table_body, n_tokens=M, n_ranges=E + 1),
        in_specs=[smem, smem, smem, smem],
        out_specs=smem,
        out_shape=jax.ShapeDtypeStruct((n_slots,), jnp.int32),
        name="moe_slot_table",
    )(pos1, pos2, pad_lo, pad_hi)

    wg, wu, wd = w_gate.astype(BF16), w_up.astype(BF16), w_down.astype(BF16)
    y_sorted = pl.pallas_call(
        functools.partial(_moe_ffn_body, tm=tm, f_chunk=f_chunk),
        grid_spec=pltpu.PrefetchScalarGridSpec(
            num_scalar_prefetch=3,
            grid=(max_tiles,),
            in_specs=[
                pl.BlockSpec(memory_space=pl.ANY),
                pl.BlockSpec((1, D), lambda t, te, nt, tok: (0, 0)),
                pl.BlockSpec((1, D, d_ff), lambda t, te, nt, tok: (te[t], 0, 0)),
                pl.BlockSpec((1, D, d_ff), lambda t, te, nt, tok: (te[t], 0, 0)),
                pl.BlockSpec((1, d_ff, D), lambda t, te, nt, tok: (te[t], 0, 0)),
            ],
            out_specs=pl.BlockSpec((tm, D), lambda t, te, nt, tok: (t, 0)),
            scratch_shapes=[pltpu.VMEM((2, tm, D), F32), pltpu.SemaphoreType.DMA((2,))],
        ),
        out_shape=jax.ShapeDtypeStruct((n_slots, D), F32),
        compiler_params=_cparams(("arbitrary",)),
        name="moe_ffn",
    )(tile_expert, n_tiles_arr, tok_of_slot, X, g2, wg, wu, wd)

    def combine_rows(tm_c, first_block, n_steps, in_place):
        out_first = first_block if in_place else 0
        return pl.pallas_call(
            functools.partial(_moe_combine_body, tm=tm_c, row0=first_block * tm_c, n_steps=n_steps,
                              final_norm=final_norm),
            grid_spec=pltpu.PrefetchScalarGridSpec(
                num_scalar_prefetch=2,
                grid=(n_steps,),
                in_specs=[
                    pl.BlockSpec((tm_c, D), lambda i, p1, p2: (i + first_block, 0)),
                    pl.BlockSpec((tm_c, L), lambda i, p1, p2: (i + first_block, 0)),
                    pl.BlockSpec(memory_space=pl.ANY),
                    pl.BlockSpec((1, D), lambda i, p1, p2: (0, 0)),
                ],
                out_specs=pl.BlockSpec((tm_c, D), lambda i, p1, p2: (i + out_first, 0)),
                scratch_shapes=[pltpu.VMEM((2, tm_c, D), F32), pltpu.VMEM((2, tm_c, D), F32),
                                pltpu.SemaphoreType.DMA((2, 2))],
            ),
            out_shape=jax.ShapeDtypeStruct((M if in_place else n_steps * tm_c, D), F32),
            input_output_aliases={2: 0} if in_place else {},
            compiler_params=_cparams(("arbitrary",)),
            name="moe_combine",
        )(pos1, pos2, X, wts, y_sorted, g_final.reshape(1, D))

    if not final_norm:
        return combine_rows(tm_tok, 0, M // tm_tok, True)
    n_sample = M - n_prompt
    y_p = combine_rows(tm_out, 0, n_prompt // tm_out, False)
    y_s = combine_rows(n_sample, n_prompt // n_sample, 1, False)
    return y_p, y_s


def _ssd_post(y, xs, z, dsk, ng, wout_ref, ynorm_ref, inner):
    y = (y + dsk * xs) * _silu(z)
    gw = inner // SSM_GROUPS
    for g in range(SSM_GROUPS):
        cols = slice(g * gw, (g + 1) * gw)
        yg = y[:, cols]
        yg = yg * lax.rsqrt(jnp.mean(yg * yg, axis=-1, keepdims=True) + EPS) * ng[:, cols]
        ynorm_ref[:, cols] = yg.astype(BF16)
    return _dot(ynorm_ref[...], wout_ref[...])


def _mamba_prompt_body(x_ref, g_ref, win_ref, wdt_ref, cw_ref, cb_ref, dtb_ref, a_ref, dsk_ref, ng_ref,
                       wout_ref, r_ref, o_ref, conv_ref, ssm_ref, ctx_ref, s_ref, y_ref, ynorm_ref,
                       *, tm, tiles_per_seq, inner, bc):
    i = pl.program_id(0)

    @pl.when(i % tiles_per_seq == 0)
    def _():
        ctx_ref[0:SSM_PAD, :] = jnp.zeros((SSM_PAD, ctx_ref.shape[1]), F32)
        s_ref[...] = jnp.zeros_like(s_ref)

    x = x_ref[...]
    h = _rms(x, g_ref[...]).astype(BF16)
    proj = _dot(h, win_ref[...])
    z = proj[:, :inner]
    xbc = proj[:, inner:]
    ctx_ref[SSM_PAD:SSM_PAD + tm, :] = xbc
    conv_ref[0] = ctx_ref[SSM_PAD + tm - (SSM_CONV - 1):SSM_PAD + tm, :]
    conv = cb_ref[...] + xbc * cw_ref[SSM_CONV - 1:SSM_CONV, :]
    for k in range(SSM_CONV - 1):
        off = SSM_PAD - (SSM_CONV - 1) + k
        conv = conv + ctx_ref[off:off + tm, :] * cw_ref[k:k + 1, :]
    ctx_ref[0:SSM_PAD, :] = ctx_ref[tm:tm + SSM_PAD, :]
    act = _silu(conv)
    xs = act[:, :inner]
    bm = act[:, inner:inner + bc].astype(BF16)
    cm = act[:, inner + bc:].astype(BF16)
    dt = _softplus(_dot(h, wdt_ref[...]) + dtb_ref[...])
    da = dt * a_ref[...]

    rows_i = lax.broadcasted_iota(jnp.int32, (CHUNK, CHUNK), 0)
    cols_i = lax.broadcasted_iota(jnp.int32, (CHUNK, CHUNK), 1)
    causal = rows_i >= cols_i
    tril = causal.astype(F32)
    gw = inner // SSM_GROUPS
    hpg = gw // SSM_HEAD_DIM
    lane_head = lax.broadcasted_iota(jnp.int32, (CHUNK, gw), 1) // SSM_HEAD_DIM
    r_mat = r_ref[...]

    for c in range(tm // CHUNK):
        rows = slice(c * CHUNK, (c + 1) * CHUNK)
        dt_c = dt[rows, :]
        cum = _dot_hi(tril, da[rows, :])
        cum_t = cum.T
        dt_t = dt_c.T
        ecum = jnp.exp(cum)
        e_full = _dot_split_lhs(ecum, r_mat)
        w_in = jnp.exp(cum[CHUNK - 1:CHUNK, :] - cum) * dt_c
        xs_c = xs[rows, :]
        xw = (xs_c * _dot_split_lhs(w_in, r_mat)).astype(BF16)
        xs_b = xs_c.astype(BF16)
        for g in range(SSM_GROUPS):
            gcols = slice(g * gw, (g + 1) * gw)
            ncols = slice(g * SSM_STATE, (g + 1) * SSM_STATE)
            b_g = bm[rows, ncols]
            c_g = cm[rows, ncols]
            cb = _dot_nt(c_g, b_g)
            x_g = xs_b[:, gcols]
            s_g = s_ref[g]
            y_g = _dot_nt(c_g, s_g.astype(BF16)) * e_full[:, gcols]
            for r in range(hpg):
                hd = g * hpg + r
                seg = cum[:, hd:hd + 1] - cum_t[hd:hd + 1, :]
                decay = jnp.where(causal, jnp.exp(jnp.where(causal, seg, 0.0)), 0.0)
                wts = (cb * decay * dt_t[hd:hd + 1, :]).astype(BF16)
                y_g = y_g + _dot(wts, jnp.where(lane_head == r, x_g, jnp.zeros_like(x_g)))
            y_ref[rows, gcols] = y_g
            upd = _dot_tn(xw[:, gcols], b_g)
            for r in range(hpg):
                hd = g * hpg + r
                hrows = slice(r * SSM_HEAD_DIM, (r + 1) * SSM_HEAD_DIM)
                s_ref[g, hrows, :] = s_g[hrows, :] * ecum[CHUNK - 1:CHUNK, hd:hd + 1] + upd[hrows, :]

    out = _ssd_post(y_ref[...], xs, z, dsk_ref[...], ng_ref[...], wout_ref, ynorm_ref, inner)
    o_ref[...] = x + out
    ssm_ref[0] = s_ref[...]


def _mamba_sample_pre_body(x_ref, g_ref, win_ref, wdt_ref, cw_ref, cb_ref, dtb_ref, a_ref, r_ref, cs_ref,
                           z_ref, xs_ref, b_ref, c_ref, xdt_ref, e_ref, newconv_ref, *, inner, bc):
    h = _rms(x_ref[...], g_ref[...]).astype(BF16)
    proj = _dot(h, win_ref[...])
    z_ref[...] = proj[:, :inner]
    xbc = proj[:, inner:]
    conv = cb_ref[...] + xbc * cw_ref[SSM_CONV - 1:SSM_CONV, :]
    for k in range(SSM_CONV - 1):
        conv = conv + cs_ref[k] * cw_ref[k:k + 1, :]
    for k in range(SSM_CONV - 2):
        newconv_ref[k] = cs_ref[k + 1]
    newconv_ref[SSM_CONV - 2] = xbc
    act = _silu(conv)
    xs = act[:, :inner]
    xs_ref[...] = xs
    b_ref[...] = act[:, inner:inner + bc]
    c_ref[...] = act[:, inner + bc:]
    dt = _softplus(_dot(h, wdt_ref[...]) + dtb_ref[...])
    xdt_ref[...] = xs * _dot_split_lhs(dt, r_ref[...])
    e_ref[...] = jnp.exp(dt * a_ref[...])


def _mamba_sample_state_body(e_ref, h0_ref, xdt_t_ref, b_ref, c_ref, hn_ref, y_ref, *, bt, inner, heads):
    i = pl.program_id(0)
    gw = inner // SSM_GROUPS
    hpg = gw // SSM_HEAD_DIM
    row_id = lax.broadcasted_iota(jnp.int32, (bt, bt * SSM_STATE), 0)
    lane_tok = lax.broadcasted_iota(jnp.int32, (bt, bt * SSM_STATE), 1) // SSM_STATE
    for g in range(SSM_GROUPS):
        rows = slice(g * gw, (g + 1) * gw)
        ncols = slice(g * SSM_STATE, (g + 1) * SSM_STATE)
        states = []
        for j in range(bt):
            b_row = b_ref[0, j:j + 1, ncols]
            for r in range(hpg):
                hrows = slice(g * gw + r * SSM_HEAD_DIM, g * gw + (r + 1) * SSM_HEAD_DIM)
                e = e_ref[(i * bt + j) * heads + g * hpg + r]
                hn_ref[j, hrows, :] = h0_ref[j, hrows, :] * e + xdt_t_ref[0, hrows, j:j + 1] * b_row
            states.append(hn_ref[j, rows, :].astype(BF16))
        c_g = c_ref[0, :, ncols]
        c_diag = jnp.where(row_id == lane_tok, jnp.concatenate([c_g] * bt, axis=1), 0.0).astype(BF16)
        y_ref[:, rows] = _dot_nt(c_diag, jnp.concatenate(states, axis=1))


def _mamba_sample_post_body(x_ref, y_ref, xs_ref, z_ref, dsk_ref, ng_ref, wout_ref, o_ref, ynorm_ref, *, inner):
    out = _ssd_post(y_ref[...], xs_ref[...], z_ref[...], dsk_ref[...], ng_ref[...], wout_ref, ynorm_ref, inner)
    o_ref[...] = x_ref[...] + out


def _mamba(X, n_prompt, n_batch, state_ssm_all, layer, state_conv, g, w_in, conv_w, conv_b, dt_bias, a_log, d_skip,
           norm_g, w_out, *, tm, bt):
    M, D = X.shape
    inner = w_out.shape[0]
    heads = a_log.shape[0]
    conv_dim = conv_w.shape[1]
    bc = (conv_dim - inner) // 2
    n_sample = M - n_prompt
    seq = n_prompt // n_batch
    tiles_per_seq = seq // tm
    gw = inner // SSM_GROUPS
    L = LANES_V7X

    g2 = g.reshape(1, D)
    w_main = w_in[:, :inner + conv_dim].astype(BF16)
    w_dt = jnp.zeros((D, L), F32).at[:, :heads].set(w_in[:, inner + conv_dim:]).astype(BF16)
    cb2 = conv_b.reshape(1, conv_dim)
    dtb = jnp.zeros((1, L), F32).at[0, :heads].set(dt_bias)
    a_neg = jnp.zeros((1, L), F32).at[0, :heads].set(-jnp.exp(a_log))
    dsk = jnp.repeat(d_skip, SSM_HEAD_DIM).reshape(1, inner)
    ng2 = norm_g.reshape(1, inner)
    w_out_b = w_out.astype(BF16)
    r_mat = (jnp.arange(L, dtype=jnp.int32)[:, None]
             == (jnp.arange(inner, dtype=jnp.int32) // SSM_HEAD_DIM)[None, :]).astype(BF16)

    X, conv_p, ssm_p = pl.pallas_call(
        functools.partial(_mamba_prompt_body, tm=tm, tiles_per_seq=tiles_per_seq, inner=inner, bc=bc),
        grid=(n_prompt // tm,),
        in_specs=[
            pl.BlockSpec((tm, D), lambda i: (i, 0)),
            _const_spec((1, D)),
            _const_spec((D, inner + conv_dim)),
            _const_spec((D, L)),
            _const_spec((SSM_CONV, conv_dim)),
            _const_spec((1, conv_dim)),
            _const_spec((1, L)),
            _const_spec((1, L)),
            _const_spec((1, inner)),
            _const_spec((1, inner)),
            _const_spec((inner, D)),
            _const_spec((L, inner)),
        ],
        out_specs=[
            pl.BlockSpec((tm, D), lambda i: (i, 0)),
            pl.BlockSpec((1, SSM_CONV - 1, conv_dim), lambda i: (i // tiles_per_seq, 0, 0)),
            pl.BlockSpec((1, SSM_GROUPS, gw, SSM_STATE), lambda i: (i // tiles_per_seq, 0, 0, 0)),
        ],
        out_shape=[
            jax.ShapeDtypeStruct((M, D), F32),
            jax.ShapeDtypeStruct((n_batch, SSM_CONV - 1, conv_dim), F32),
            jax.ShapeDtypeStruct((n_batch, SSM_GROUPS, gw, SSM_STATE), F32),
        ],
        scratch_shapes=[
            pltpu.VMEM((SSM_PAD + tm, conv_dim), F32),
            pltpu.VMEM((SSM_GROUPS, gw, SSM_STATE), F32),
            pltpu.VMEM((tm, inner), F32),
            pltpu.VMEM((tm, inner), BF16),
        ],
        input_output_aliases={0: 0},
        compiler_params=_cparams(("arbitrary",)),
        name="mamba_prompt",
    )(X, g2, w_main, w_dt, conv_w, cb2, dtb, a_neg, dsk, ng2, w_out_b, r_mat)

    sblk = n_prompt // n_sample
    cs_t = jnp.transpose(state_conv, (1, 0, 2))
    z, xs, b_m, c_m, xdt, e_tok, newconv_t = pl.pallas_call(
        functools.partial(_mamba_sample_pre_body, inner=inner, bc=bc),
        grid=(1,),
        in_specs=[
            pl.BlockSpec((n_sample, D), lambda i: (sblk, 0)),
            _const_spec((1, D)),
            _const_spec((D, inner + conv_dim)),
            _const_spec((D, L)),
            _const_spec((SSM_CONV, conv_dim)),
            _const_spec((1, conv_dim)),
            _const_spec((1, L)),
            _const_spec((1, L)),
            _const_spec((L, inner)),
            _const_spec((SSM_CONV - 1, n_sample, conv_dim)),
        ],
        out_specs=[
            pl.BlockSpec((n_sample, inner), lambda i: (0, 0)),
            pl.BlockSpec((n_sample, inner), lambda i: (0, 0)),
            pl.BlockSpec((n_sample, bc), lambda i: (0, 0)),
            pl.BlockSpec((n_sample, bc), lambda i: (0, 0)),
            pl.BlockSpec((n_sample, inner), lambda i: (0, 0)),
            pl.BlockSpec((n_sample, L), lambda i: (0, 0)),
            pl.BlockSpec((SSM_CONV - 1, n_sample, conv_dim), lambda i: (0, 0, 0)),
        ],
        out_shape=[
            jax.ShapeDtypeStruct((n_sample, inner), F32),
            jax.ShapeDtypeStruct((n_sample, inner), F32),
            jax.ShapeDtypeStruct((n_sample, bc), F32),
            jax.ShapeDtypeStruct((n_sample, bc), F32),
            jax.ShapeDtypeStruct((n_sample, inner), F32),
            jax.ShapeDtypeStruct((n_sample, L), F32),
            jax.ShapeDtypeStruct((SSM_CONV - 1, n_sample, conv_dim), F32),
        ],
        compiler_params=_cparams(("arbitrary",)),
        name="mamba_sample_pre",
    )(X, g2, w_main, w_dt, conv_w, cb2, dtb, a_neg, r_mat, cs_t)
    conv_s = jnp.transpose(newconv_t, (1, 0, 2))

    nblk = n_sample // bt

    def to_cols(arr):
        return jnp.transpose(arr.reshape(nblk, bt, inner), (0, 2, 1))

    h0 = state_ssm_all.reshape(state_ssm_all.shape[0], n_sample, inner, SSM_STATE)
    h_new, y_s = pl.pallas_call(
        functools.partial(_mamba_sample_state_body, bt=bt, inner=inner, heads=heads),
        grid_spec=pltpu.PrefetchScalarGridSpec(
            num_scalar_prefetch=1,
            grid=(nblk,),
            in_specs=[
                pl.BlockSpec((None, bt, inner, SSM_STATE), lambda i, e: (layer, i, 0, 0)),
                pl.BlockSpec((1, inner, bt), lambda i, e: (i, 0, 0)),
                pl.BlockSpec((1, bt, bc), lambda i, e: (i, 0, 0)),
                pl.BlockSpec((1, bt, bc), lambda i, e: (i, 0, 0)),
            ],
            out_specs=[
                pl.BlockSpec((bt, inner, SSM_STATE), lambda i, e: (i, 0, 0)),
                pl.BlockSpec((bt, inner), lambda i, e: (i, 0)),
            ],
        ),
        out_shape=[
            jax.ShapeDtypeStruct((n_sample, inner, SSM_STATE), F32),
            jax.ShapeDtypeStruct((n_sample, inner), F32),
        ],
        compiler_params=_cparams(("arbitrary",)),
        name="mamba_sample_state",
    )(e_tok[:, :heads].reshape(n_sample * heads), h0, to_cols(xdt), b_m.reshape(nblk, bt, bc),
      c_m.reshape(nblk, bt, bc))

    X = pl.pallas_call(
        functools.partial(_mamba_sample_post_body, inner=inner),
        grid=(1,),
        in_specs=[
            pl.BlockSpec((n_sample, D), lambda i: (sblk, 0)),
            pl.BlockSpec((n_sample, inner), lambda i: (0, 0)),
            pl.BlockSpec((n_sample, inner), lambda i: (0, 0)),
            pl.BlockSpec((n_sample, inner), lambda i: (0, 0)),
            _const_spec((1, inner)),
            _const_spec((1, inner)),
            _const_spec((inner, D)),
        ],
        out_specs=pl.BlockSpec((n_sample, D), lambda i: (sblk, 0)),
        out_shape=jax.ShapeDtypeStruct((M, D), F32),
        scratch_shapes=[pltpu.VMEM((n_sample, inner), BF16)],
        input_output_aliases={0: 0},
        compiler_params=_cparams(("arbitrary",)),
        name="mamba_sample_post",
    )(X, y_s, xs, z, dsk, ng2, w_out_b)

    ssm_p = ssm_p.reshape(n_batch, heads, SSM_HEAD_DIM, SSM_STATE)
    ssm_s = h_new.reshape(n_sample, heads, SSM_HEAD_DIM, SSM_STATE)
    return X, conv_p, conv_s, ssm_p, ssm_s


def _causal_dwconv_tile(ctx_ref, dw_ref, acc_ref, tm, d):
    S = SUBLANES_V7X
    first = CFM_PAD - (CFM_KERNEL - 1)
    rb, lb = CONV_ROW_BLOCK, CONV_LANE_BLOCK
    for r0 in range(0, tm, rb):
        for l0 in range(0, d, lb):
            lanes = slice(l0, l0 + lb)
            y = None
            for b in range(S):
                pb = None
                for a in range((first + CFM_KERNEL - 1) // S + 1):
                    k = S * a + b - first
                    if 0 <= k < CFM_KERNEL:
                        term = ctx_ref[r0 + S * a:r0 + S * a + rb + S, lanes] * dw_ref[k:k + 1, lanes]
                        pb = term if pb is None else pb + term
                part = pb[b:b + rb, :]
                y = part if y is None else y + part
            acc_ref[r0:r0 + rb, lanes] = y


def _cfm_prompt_body(x_ref, g_ref, w1_ref, b1_ref, dw_ref, dwb_ref, lng_ref, lnb_ref, w2_ref, b2_ref,
                     o_ref, buf_ref, ctx_ref, acc_ref, *, tm, tiles_per_seq, d):
    i = pl.program_id(0)

    @pl.when(i % tiles_per_seq == 0)
    def _():
        ctx_ref[0:CFM_PAD, :] = jnp.zeros((CFM_PAD, d), F32)
        ctx_ref[CFM_PAD + tm:CFM_PAD + tm + SUBLANES_V7X, :] = jnp.zeros((SUBLANES_V7X, d), F32)

    x = x_ref[...]
    h = _rms(x, g_ref[...]).astype(BF16)
    a = _dot(h, w1_ref[...]) + b1_ref[...]
    ctx_ref[CFM_PAD:CFM_PAD + tm, :] = a[:, :d] * _sigmoid(a[:, d:])
    _causal_dwconv_tile(ctx_ref, dw_ref, acc_ref, tm, d)
    buf_ref[0] = ctx_ref[CFM_PAD + tm - (CFM_KERNEL - 1):CFM_PAD + tm, :]
    ctx_ref[0:CFM_PAD, :] = ctx_ref[tm:tm + CFM_PAD, :]
    acc = acc_ref[...] + dwb_ref[...]
    hc = _silu(_layernorm(acc, lng_ref[...], lnb_ref[...])).astype(BF16)
    o_ref[...] = x + _dot(hc, w2_ref[...]) + b2_ref[...]


def _cfm_sample_body(x_ref, g_ref, w1_ref, b1_ref, dw_ref, dwb_ref, lng_ref, lnb_ref, w2_ref, b2_ref, cs_ref,
                     o_ref, new_ref, *, d):
    x = x_ref[...]
    h = _rms(x, g_ref[...]).astype(BF16)
    a = _dot(h, w1_ref[...]) + b1_ref[...]
    glu = a[:, :d] * _sigmoid(a[:, d:])
    acc = dwb_ref[...] + glu * dw_ref[CFM_KERNEL - 1:CFM_KERNEL, :]
    for k in range(CFM_KERNEL - 1):
        acc = acc + cs_ref[k] * dw_ref[k:k + 1, :]
    for k in range(CFM_KERNEL - 2):
        new_ref[k] = cs_ref[k + 1]
    new_ref[CFM_KERNEL - 2] = glu
    hc = _silu(_layernorm(acc, lng_ref[...], lnb_ref[...])).astype(BF16)
    o_ref[...] = x + _dot(hc, w2_ref[...]) + b2_ref[...]


def _conformer(X, n_prompt, n_batch, state_conv, g, w_pw1, b_pw1, dw_w, dw_b, ln_g, ln_b, w_pw2, b_pw2,
               *, tm, bt):
    M, D = X.shape
    n_sample = M - n_prompt
    seq = n_prompt // n_batch
    tiles_per_seq = seq // tm
    K = CFM_KERNEL
    g2 = g.reshape(1, D)
    w1, w2 = w_pw1.astype(BF16), w_pw2.astype(BF16)
    b1, b2 = b_pw1.reshape(1, 2 * D), b_pw2.reshape(1, D)
    dwb, lng, lnb = dw_b.reshape(1, D), ln_g.reshape(1, D), ln_b.reshape(1, D)
    weight_specs = [
        _const_spec((1, D)), _const_spec((D, 2 * D)), _const_spec((1, 2 * D)), _const_spec((K, D)),
        _const_spec((1, D)), _const_spec((1, D)), _const_spec((1, D)), _const_spec((D, D)), _const_spec((1, D)),
    ]
    weights = (g2, w1, b1, dw_w, dwb, lng, lnb, w2, b2)

    X, buf_p = pl.pallas_call(
        functools.partial(_cfm_prompt_body, tm=tm, tiles_per_seq=tiles_per_seq, d=D),
        grid=(n_prompt // tm,),
        in_specs=[pl.BlockSpec((tm, D), lambda i: (i, 0))] + weight_specs,
        out_specs=[
            pl.BlockSpec((tm, D), lambda i: (i, 0)),
            pl.BlockSpec((1, K - 1, D), lambda i: (i // tiles_per_seq, 0, 0)),
        ],
        out_shape=[jax.ShapeDtypeStruct((M, D), F32), jax.ShapeDtypeStruct((n_batch, K - 1, D), F32)],
        scratch_shapes=[pltpu.VMEM((CFM_PAD + tm + SUBLANES_V7X, D), F32), pltpu.VMEM((tm, D), F32)],
        input_output_aliases={0: 0},
        compiler_params=_cparams(("arbitrary",)),
        name="conformer_prompt",
    )(X, *weights)

    first = n_prompt // bt
    cs_t = jnp.transpose(state_conv, (1, 0, 2))
    X, new_t = pl.pallas_call(
        functools.partial(_cfm_sample_body, d=D),
        grid=(n_sample // bt,),
        in_specs=[pl.BlockSpec((bt, D), lambda i: (i + first, 0))] + weight_specs
        + [pl.BlockSpec((K - 1, bt, D), lambda i: (0, i, 0))],
        out_specs=[
            pl.BlockSpec((bt, D), lambda i: (i + first, 0)),
            pl.BlockSpec((K - 1, bt, D), lambda i: (0, i, 0)),
        ],
        out_shape=[jax.ShapeDtypeStruct((M, D), F32), jax.ShapeDtypeStruct((K - 1, n_sample, D), F32)],
        input_output_aliases={0: 0},
        compiler_params=_cparams(("arbitrary",)),
        name="conformer_sample",
    )(X, *weights, cs_t)
    return X, buf_p, jnp.transpose(new_t, (1, 0, 2))


def _stack(parts):
    return parts[0][None] if len(parts) == 1 else jnp.stack(parts)


def kernel(x_prompt, x_sample, state_ssm, state_conv_ssm, state_conv_cfm, norm_mix_g, norm_ffn_g, norm_final_g, a_w_in, a_b_in, a_ln_g, a_ln_b, a_w_s, a_b_s, a_w_out, b_w_in, b_conv_w, b_conv_b, b_dt_bias, b_a_log, b_d, b_norm_g, b_w_out, c_w_pw1, c_b_pw1, c_dw_w, c_dw_b, c_ln_g, c_ln_b, c_w_pw2, c_b_pw2, f_w_gate, f_w_up, f_w_down, e_w_router, e_w_gate, e_w_up, e_w_down):
    n_batch, seq, D = x_prompt.shape
    n_sample = x_sample.shape[0]
    n_prompt = n_batch * seq
    depth = norm_mix_g.shape[0]
    d_ff = f_w_gate.shape[2]
    assert x_sample.shape[1] == 1 and n_prompt % n_sample == 0 and seq % CHUNK == 0

    X = jnp.concatenate([x_prompt.reshape(n_prompt, D), x_sample.reshape(n_sample, D)], axis=0)
    v_p, v_s, ssm_p, ssm_s, cs_p, cs_s, cc_p, cc_s = [], [], [], [], [], [], [], []
    for i in range(depth):
        kind, j = i % 3, i // 3
        if kind == 0:
            X, vp, vs = _gmlp(X, n_prompt, n_batch, norm_mix_g[i], a_w_in[j], a_b_in[j], a_ln_g[j], a_ln_b[j],
                              a_w_s[j], a_b_s[j], a_w_out[j], tm=512)
            v_p.append(vp)
            v_s.append(vs)
        elif kind == 1:
            X, cbp, cbs, hlp, hls = _mamba(X, n_prompt, n_batch, state_ssm, j, state_conv_ssm[j], norm_mix_g[i],
                                           b_w_in[j], b_conv_w[j], b_conv_b[j], b_dt_bias[j], b_a_log[j], b_d[j],
                                           b_norm_g[j], b_w_out[j], tm=256, bt=8)
            cs_p.append(cbp)
            cs_s.append(cbs)
            ssm_p.append(hlp)
            ssm_s.append(hls)
        else:
            X, cbp, cbs = _conformer(X, n_prompt, n_batch, state_conv_cfm[j], norm_mix_g[i], c_w_pw1[j], c_b_pw1[j],
                                     c_dw_w[j], c_dw_b[j], c_ln_g[j], c_ln_b[j], c_w_pw2[j], c_b_pw2[j],
                                     tm=512, bt=32)
            cc_p.append(cbp)
            cc_s.append(cbs)
        k = i // 2
        if i % 2 == 0:
            X = _ffn_dense(X, n_prompt, norm_ffn_g[i], f_w_gate[k], f_w_up[k], f_w_down[k],
                           tm=512, f_chunk=d_ff // 2)
        else:
            X = _moe(X, n_prompt, norm_ffn_g[i], e_w_router[k], e_w_gate[k], e_w_up[k], e_w_down[k], norm_final_g,
                     tm_tok=384, tm_out=512, tm=512, f_chunk=d_ff // 2, final_norm=(i == depth - 1))
    assert depth % 2 == 0
    y_prompt, y_sample = X
    y_prompt = y_prompt.reshape(n_batch, seq, D)
    y_sample = y_sample.reshape(n_sample, 1, D)
    return (y_prompt, y_sample, _stack(v_p), _stack(v_s), _stack(ssm_p), _stack(ssm_s),
            _stack(cs_p), _stack(cs_s), _stack(cc_p), _stack(cc_s))
```

```python
import functools

import jax
import jax.numpy as jnp
from jax import lax
from jax.experimental import pallas as pl
from jax.experimental.pallas import tpu as pltpu

F32 = jnp.float32
BF16 = jnp.bfloat16
EPS = 1e-6
HIGHEST = lax.Precision.HIGHEST

LANES_V7X = 128
SUBLANES_V7X = 8
VMEM_LIMIT_V7X = 60 * 1024 * 1024

CHUNK = 128
GM_GROUPS = 8
SSM_GROUPS = 8
SSM_HEAD_DIM = 64
SSM_STATE = 128
SSM_CONV = 4
CFM_KERNEL = 31
N_EXPERTS = 8
CFM_PAD = 32
SSM_PAD = 8
FFN_CHUNK = 256
CONV_ROW_BLOCK = 128
CONV_LANE_BLOCK = 256


def _cparams(sem):
    return pltpu.CompilerParams(dimension_semantics=sem, vmem_limit_bytes=VMEM_LIMIT_V7X)


def _const_spec(shape):
    nd = len(shape)
    return pl.BlockSpec(shape, lambda *_: (0,) * nd, pipeline_mode=pl.Buffered(1))


def _dot(a, b):
    return jnp.dot(a, b, preferred_element_type=F32)


def _dot_nt(a, b):
    return lax.dot_general(a, b, (((1,), (1,)), ((), ())), preferred_element_type=F32)


def _dot_tn(a, b):
    return lax.dot_general(a, b, (((0,), (0,)), ((), ())), preferred_element_type=F32)


def _dot_hi(a, b):
    return jnp.dot(a, b, precision=HIGHEST, preferred_element_type=F32)


def _split_bf16(a):
    hi = a.astype(BF16)
    return hi, (a - hi.astype(F32)).astype(BF16)


def _dot_split_lhs(a, b_bf16):
    hi, lo = _split_bf16(a)
    return _dot(hi, b_bf16) + _dot(lo, b_bf16)


def _rms(x, g):
    return x * lax.rsqrt(jnp.mean(x * x, axis=-1, keepdims=True) + EPS) * g


def _layernorm(x, g, b):
    xc = x - jnp.mean(x, axis=-1, keepdims=True)
    return xc * lax.rsqrt(jnp.mean(xc * xc, axis=-1, keepdims=True) + EPS) * g + b


def _sigmoid(x):
    return 1.0 / (1.0 + jnp.exp(-x))


def _silu(x):
    return x * _sigmoid(x)


def _gelu(x):
    return 0.5 * x * (1.0 + lax.erf(x * (2.0 ** -0.5)))


def _softplus(x):
    return jnp.maximum(x, 0.0) + jnp.log(1.0 + jnp.exp(-jnp.abs(x)))


def _gmlp_prompt_body(x_ref, g_ref, win_ref, bin_ref, lng_ref, lnb_ref, ws_ref, bs_ref, wout_ref,
                      o_ref, v_ref, gated_ref, *, tm, width):
    x = x_ref[...]
    h = _rms(x, g_ref[...]).astype(BF16)
    u = _gelu(_dot(h, win_ref[:, :width]) + bin_ref[:, :width])
    hv = _gelu(_dot(h, win_ref[:, width:]) + bin_ref[:, width:])
    v = _layernorm(hv, lng_ref[...], lnb_ref[...])
    v_ref[0] = v[tm - CHUNK:, :]
    vb = v.astype(BF16)
    gdim = width // GM_GROUPS
    causal = (lax.broadcasted_iota(jnp.int32, (CHUNK, CHUNK), 0)
              >= lax.broadcasted_iota(jnp.int32, (CHUNK, CHUNK), 1))
    for g in range(GM_GROUPS):
        wc = jnp.where(causal, ws_ref[g], 0.0).astype(BF16)
        bias = bs_ref[:, g:g + 1]
        for c in range(tm // CHUNK):
            rows = slice(c * CHUNK, (c + 1) * CHUNK)
            cols = slice(g * gdim, (g + 1) * gdim)
            s = _dot(wc, vb[rows, cols]) + bias
            gated_ref[rows, cols] = (u[rows, cols] * s).astype(BF16)
    o_ref[...] = x + _dot(gated_ref[...], wout_ref[...])


def _gmlp_sample_body(x_ref, g_ref, win_ref, bin_ref, lng_ref, lnb_ref, ws0_ref, bs0_ref, wout_ref,
                      o_ref, v_ref, *, width):
    x = x_ref[...]
    h = _rms(x, g_ref[...]).astype(BF16)
    u = _gelu(_dot(h, win_ref[:, :width]) + bin_ref[:, :width])
    hv = _gelu(_dot(h, win_ref[:, width:]) + bin_ref[:, width:])
    v = _layernorm(hv, lng_ref[...], lnb_ref[...])
    v_ref[...] = v
    s = v * ws0_ref[...] + bs0_ref[...]
    o_ref[...] = x + _dot((u * s).astype(BF16), wout_ref[...])


def _gmlp(X, n_prompt, n_batch, g, w_in, b_in, ln_g, ln_b, w_s, b_s, w_out, *, tm):
    M, D = X.shape
    width = w_out.shape[0]
    seq = n_prompt // n_batch
    tiles_per_seq = seq // tm
    n_sample = M - n_prompt
    g2 = g.reshape(1, D)
    b_in2 = b_in.reshape(1, 2 * width)
    ln_g2, ln_b2 = ln_g.reshape(1, width), ln_b.reshape(1, width)
    w_in_b, w_out_b = w_in.astype(BF16), w_out.astype(BF16)

    X, v_p = pl.pallas_call(
        functools.partial(_gmlp_prompt_body, tm=tm, width=width),
        grid=(n_prompt // tm,),
        in_specs=[
            pl.BlockSpec((tm, D), lambda i: (i, 0)),
            _const_spec((1, D)),
            _const_spec((D, 2 * width)),
            _const_spec((1, 2 * width)),
            _const_spec((1, width)),
            _const_spec((1, width)),
            _const_spec((GM_GROUPS, CHUNK, CHUNK)),
            _const_spec((CHUNK, GM_GROUPS)),
            _const_spec((width, D)),
        ],
        out_specs=[
            pl.BlockSpec((tm, D), lambda i: (i, 0)),
            pl.BlockSpec((1, CHUNK, width), lambda i: (i // tiles_per_seq, 0, 0)),
        ],
        out_shape=[jax.ShapeDtypeStruct((M, D), F32),
                   jax.ShapeDtypeStruct((n_batch, CHUNK, width), F32)],
        scratch_shapes=[pltpu.VMEM((tm, width), BF16)],
        input_output_aliases={0: 0},
        compiler_params=_cparams(("arbitrary",)),
        name="gmlp_prompt",
    )(X, g2, w_in_b, b_in2, ln_g2, ln_b2, w_s, b_s.T, w_out_b)

    gdim = width // GM_GROUPS
    ws0 = jnp.repeat(w_s[:, 0, 0], gdim).reshape(1, width)
    bs0 = jnp.repeat(b_s[:, 0], gdim).reshape(1, width)
    sblk = n_prompt // n_sample
    X, v_s = pl.pallas_call(
        functools.partial(_gmlp_sample_body, width=width),
        grid=(1,),
        in_specs=[
            pl.BlockSpec((n_sample, D), lambda i: (sblk, 0)),
            _const_spec((1, D)),
            _const_spec((D, 2 * width)),
            _const_spec((1, 2 * width)),
            _const_spec((1, width)),
            _const_spec((1, width)),
            _const_spec((1, width)),
            _const_spec((1, width)),
            _const_spec((width, D)),
        ],
        out_specs=[
            pl.BlockSpec((n_sample, D), lambda i: (sblk, 0)),
            pl.BlockSpec((n_sample, width), lambda i: (0, 0)),
        ],
        out_shape=[jax.ShapeDtypeStruct((M, D), F32),
                   jax.ShapeDtypeStruct((n_sample, width), F32)],
        input_output_aliases={0: 0},
        compiler_params=_cparams(("arbitrary",)),
        name="gmlp_sample",
    )(X, g2, w_in_b, b_in2, ln_g2, ln_b2, ws0, bs0, w_out_b)
    return X, v_p, v_s.reshape(n_sample, 1, width)


def _swiglu_tile(h, wg_ref, wu_ref, wd_ref, f_chunk):
    d_ff = wd_ref.shape[0]
    acc = None
    for f in range(d_ff // f_chunk):
        cols = slice(f * f_chunk, (f + 1) * f_chunk)
        a = _dot(h, wg_ref[:, cols].astype(BF16))
        b = _dot(h, wu_ref[:, cols].astype(BF16))
        part = _dot((_silu(a) * b).astype(BF16), wd_ref[cols, :].astype(BF16))
        acc = part if acc is None else acc + part
    return acc


def _ffn_dense_body(x_ref, g_ref, wg_ref, wu_ref, wd_ref, o_ref, *, f_chunk):
    x = x_ref[...]
    h = _rms(x, g_ref[...]).astype(BF16)
    o_ref[...] = x + _swiglu_tile(h, wg_ref, wu_ref, wd_ref, f_chunk)


def _ffn_dense_rows(X, g2, wg, wu, wd, *, tm, first_block, n_tiles, f_chunk):
    M, D = X.shape
    d_ff = wd.shape[0]
    return pl.pallas_call(
        functools.partial(_ffn_dense_body, f_chunk=f_chunk),
        grid=(n_tiles,),
        in_specs=[
            pl.BlockSpec((tm, D), lambda i: (i + first_block, 0)),
            _const_spec((1, D)),
            _const_spec((D, d_ff)),
            _const_spec((D, d_ff)),
            _const_spec((d_ff, D)),
        ],
        out_specs=pl.BlockSpec((tm, D), lambda i: (i + first_block, 0)),
        out_shape=jax.ShapeDtypeStruct((M, D), F32),
        input_output_aliases={0: 0},
        compiler_params=_cparams(("arbitrary",)),
        name="ffn_dense",
    )(X, g2, wg, wu, wd)


def _ffn_dense(X, n_prompt, g, w_gate, w_up, w_down, *, tm, f_chunk):
    M, D = X.shape
    n_sample = M - n_prompt
    g2 = g.reshape(1, D)
    X = _ffn_dense_rows(X, g2, w_gate, w_up, w_down, tm=tm, first_block=0, n_tiles=n_prompt // tm,
                        f_chunk=f_chunk)
    X = _ffn_dense_rows(X, g2, w_gate, w_up, w_down, tm=n_sample, first_block=n_prompt // n_sample, n_tiles=1,
                        f_chunk=f_chunk)
    return X


def _router_body(x_ref, g_ref, wr_hi_ref, wr_lo_ref, idx_ref, wts_ref):
    h_hi, h_lo = _split_bf16(_rms(x_ref[...], g_ref[...]))
    logits = _dot(h_hi, wr_hi_ref[...]) + (_dot(h_lo, wr_hi_ref[...]) + _dot(h_hi, wr_lo_ref[...]))
    lane = lax.broadcasted_iota(jnp.int32, logits.shape, 1)
    neg = jnp.float32(-jnp.inf)
    logits = jnp.where(lane < N_EXPERTS, logits, neg)
    m1 = jnp.max(logits, axis=-1, keepdims=True)
    i1 = jnp.min(jnp.where(logits == m1, lane, LANES_V7X), axis=-1, keepdims=True)
    rest = jnp.where(lane == i1, neg, logits)
    m2 = jnp.max(rest, axis=-1, keepdims=True)
    i2 = jnp.min(jnp.where(rest == m2, lane, LANES_V7X), axis=-1, keepdims=True)
    e = jnp.exp(m2 - m1)
    w1 = 1.0 / (1.0 + e)
    w2 = e / (1.0 + e)
    idx_ref[...] = jnp.where(lane == 0, i1, jnp.where(lane == 1, i2, 0))
    wts_ref[...] = jnp.where(lane == 0, w1, jnp.where(lane == 1, w2, 0.0))


def _slot_table_body(pos1_ref, pos2_ref, pad_lo_ref, pad_hi_ref, tok_ref, *, n_tokens, n_ranges):
    def pad(s, carry):
        tok_ref[s] = 0
        return carry

    for r in range(n_ranges):
        lax.fori_loop(pad_lo_ref[r], pad_hi_ref[r], pad, 0)

    def place(t, carry):
        tok_ref[pos1_ref[t]] = t
        tok_ref[pos2_ref[t]] = t
        return carry

    lax.fori_loop(0, n_tokens, place, 0, unroll=8)


def _issue_row_gather(src_hbm, idx_ref, base, buf, sem, n_rows):
    def body(r, carry):
        pltpu.make_async_copy(src_hbm.at[pl.ds(idx_ref[base + r], 1), :], buf.at[pl.ds(r, 1), :], sem).start()
        return carry
    lax.fori_loop(0, n_rows, body, 0, unroll=8)


def _wait_row_gather(src_hbm, buf, sem, n_rows):
    pltpu.make_async_copy(src_hbm.at[pl.ds(0, n_rows), :], buf, sem).wait()


def _moe_ffn_body(tile_expert_ref, n_tiles_ref, tok_ref, x_hbm, g_ref, wg_ref, wu_ref, wd_ref,
                  y_ref, buf_ref, sem_ref, *, tm, f_chunk):
    t = pl.program_id(0)
    n_tiles = n_tiles_ref[0]
    slot = t % 2

    @pl.when(t == 0)
    def _():
        _issue_row_gather(x_hbm, tok_ref, 0, buf_ref.at[0], sem_ref.at[0], tm)

    @pl.when(t + 1 < n_tiles)
    def _():
        _issue_row_gather(x_hbm, tok_ref, (t + 1) * tm, buf_ref.at[1 - slot], sem_ref.at[1 - slot], tm)

    @pl.when(t < n_tiles)
    def _():
        _wait_row_gather(x_hbm, buf_ref.at[slot], sem_ref.at[slot], tm)
        h = _rms(buf_ref[slot], g_ref[...]).astype(BF16)
        y_ref[...] = _swiglu_tile(h, wg_ref.at[0], wu_ref.at[0], wd_ref.at[0], f_chunk)

    @pl.when(t >= n_tiles)
    def _():
        y_ref[...] = jnp.zeros_like(y_ref)


def _moe_combine_body(p1_ref, p2_ref, x_ref, wts_ref, y_hbm, gf_ref, o_ref, buf1_ref, buf2_ref, sem_ref,
                      *, tm, row0, n_steps, final_norm):
    i = pl.program_id(0)
    slot = i % 2

    def issue(step, s):
        _issue_row_gather(y_hbm, p1_ref, row0 + step * tm, buf1_ref.at[s], sem_ref.at[0, s], tm)
        _issue_row_gather(y_hbm, p2_ref, row0 + step * tm, buf2_ref.at[s], sem_ref.at[1, s], tm)

    @pl.when(i == 0)
    def _():
        issue(0, 0)

    @pl.when(i + 1 < n_steps)
    def _():
        issue(i + 1, 1 - slot)

    _wait_row_gather(y_hbm, buf1_ref.at[slot], sem_ref.at[0, slot], tm)
    _wait_row_gather(y_hbm, buf2_ref.at[slot], sem_ref.at[1, slot], tm)
    out = x_ref[...] + (wts_ref[:, 0:1] * buf1_ref[slot] + wts_ref[:, 1:2] * buf2_ref[slot])
    if final_norm:
        out = _rms(out, gf_ref[...])
    o_ref[...] = out


def _moe(X, n_prompt, g, w_router, w_gate, w_up, w_down, g_final, *, tm_tok, tm_out, tm, f_chunk, final_norm):
    M, D = X.shape
    E = N_EXPERTS
    d_ff = w_down.shape[1]
    g2 = g.reshape(1, D)
    L = LANES_V7X
    wr = jnp.zeros((D, L), F32).at[:, :E].set(w_router)
    wr_hi = wr.astype(BF16)
    wr_lo = (wr - wr_hi.astype(F32)).astype(BF16)

    idx, wts = pl.pallas_call(
        _router_body,
        grid=(M // tm_tok,),
        in_specs=[pl.BlockSpec((tm_tok, D), lambda i: (i, 0)), _const_spec((1, D)), _const_spec((D, L)),
                  _const_spec((D, L))],
        out_specs=[pl.BlockSpec((tm_tok, L), lambda i: (i, 0)), pl.BlockSpec((tm_tok, L), lambda i: (i, 0))],
        out_shape=[jax.ShapeDtypeStruct((M, L), jnp.int32), jax.ShapeDtypeStruct((M, L), F32)],
        compiler_params=_cparams(("arbitrary",)),
        name="moe_router",
    )(X, g2, wr_hi, wr_lo)

    i1, i2 = idx[:, 0], idx[:, 1]
    eids = jnp.arange(E, dtype=jnp.int32)
    sel = ((i1[:, None] == eids) | (i2[:, None] == eids)).astype(jnp.int32)
    count = jnp.sum(sel, axis=0)
    tiles_e = (count + tm - 1) // tm
    tiles_end = jnp.cumsum(tiles_e)
    n_tiles = tiles_end[-1]
    start_e = (tiles_end - tiles_e) * tm
    pos = start_e[None, :] + jnp.cumsum(sel, axis=0) - sel
    pos1 = jnp.sum(jnp.where(i1[:, None] == eids, pos, 0), axis=1).astype(jnp.int32)
    pos2 = jnp.sum(jnp.where(i2[:, None] == eids, pos, 0), axis=1).astype(jnp.int32)
    max_tiles = (2 * M + E * (tm - 1)) // tm
    n_slots = max_tiles * tm
    pad_lo = jnp.concatenate([start_e + count, (n_tiles * tm)[None]]).astype(jnp.int32)
    pad_hi = jnp.concatenate([tiles_end * tm, jnp.array([n_slots], jnp.int32)]).astype(jnp.int32)
    tile_ids = jnp.arange(max_tiles, dtype=jnp.int32)
    tile_expert = jnp.sum((jnp.minimum(tile_ids, n_tiles - 1)[:, None] >= tiles_end[None, :]).astype(jnp.int32),
                          axis=1)
    tile_expert = jnp.minimum(tile_expert, E - 1).astype(jnp.int32)
    n_tiles_arr = n_tiles.astype(jnp.int32).reshape(1)

    smem = pl.BlockSpec(memory_space=pltpu.SMEM)
    tok_of_slot = pl.pallas_call(
        functools.partial(_slot_table_body, n_tokens=M, n_ranges=E + 1),
        in_specs=[smem, smem, smem, smem],
        out_specs=smem,
        out_shape=jax.ShapeDtypeStruct((n_slots,), jnp.int32),
        name="moe_slot_table",
    )(pos1, pos2, pad_lo, pad_hi)

    def expert_spec(rows, cols):
        return pl.BlockSpec((1, rows, cols), lambda t, te, nt, tok: (te[t], 0, 0), pipeline_mode=pl.Buffered(1))

    y_sorted = pl.pallas_call(
        functools.partial(_moe_ffn_body, tm=tm, f_chunk=f_chunk),
        grid_spec=pltpu.PrefetchScalarGridSpec(
            num_scalar_prefetch=3,
            grid=(max_tiles,),
            in_specs=[
                pl.BlockSpec(memory_space=pl.ANY),
                pl.BlockSpec((1, D), lambda t, te, nt, tok: (0, 0)),
                expert_spec(D, d_ff),
                expert_spec(D, d_ff),
                expert_spec(d_ff, D),
            ],
            out_specs=pl.BlockSpec((tm, D), lambda t, te, nt, tok: (t, 0)),
            scratch_shapes=[pltpu.VMEM((2, tm, D), F32), pltpu.SemaphoreType.DMA((2,))],
        ),
        out_shape=jax.ShapeDtypeStruct((n_slots, D), F32),
        compiler_params=_cparams(("arbitrary",)),
        name="moe_ffn",
    )(tile_expert, n_tiles_arr, tok_of_slot, X, g2, w_gate, w_up, w_down)

    def combine_rows(tm_c, first_block, n_steps, in_place):
        out_first = first_block if in_place else 0
        return pl.pallas_call(
            functools.partial(_moe_combine_body, tm=tm_c, row0=first_block * tm_c, n_steps=n_steps,
                              final_norm=final_norm),
            grid_spec=pltpu.PrefetchScalarGridSpec(
                num_scalar_prefetch=2,
                grid=(n_steps,),
                in_specs=[
                    pl.BlockSpec((tm_c, D), lambda i, p1, p2: (i + first_block, 0)),
                    pl.BlockSpec((tm_c, L), lambda i, p1, p2: (i + first_block, 0)),
                    pl.BlockSpec(memory_space=pl.ANY),
                    pl.BlockSpec((1, D), lambda i, p1, p2: (0, 0)),
                ],
                out_specs=pl.BlockSpec((tm_c, D), lambda i, p1, p2: (i + out_first, 0)),
                scratch_shapes=[pltpu.VMEM((2, tm_c, D), F32), pltpu.VMEM((2, tm_c, D), F32),
                                pltpu.SemaphoreType.DMA((2, 2))],
            ),
            out_shape=jax.ShapeDtypeStruct((M if in_place else n_steps * tm_c, D), F32),
            input_output_aliases={2: 0} if in_place else {},
            compiler_params=_cparams(("arbitrary",)),
            name="moe_combine",
        )(pos1, pos2, X, wts, y_sorted, g_final.reshape(1, D))

    if not final_norm:
        return combine_rows(tm_tok, 0, M // tm_tok, True)
    n_sample = M - n_prompt
    y_p = combine_rows(tm_out, 0, n_prompt // tm_out, False)
    y_s = combine_rows(n_sample, n_prompt // n_sample, 1, False)
    return y_p, y_s


def _ssd_post(y, xs, z, dsk, ng, wout_ref, ynorm_ref, inner):
    y = (y + dsk * xs) * _silu(z)
    gw = inner // SSM_GROUPS
    for g in range(SSM_GROUPS):
        cols = slice(g * gw, (g + 1) * gw)
        yg = y[:, cols]
        yg = yg * lax.rsqrt(jnp.mean(yg * yg, axis=-1, keepdims=True) + EPS) * ng[:, cols]
        ynorm_ref[:, cols] = yg.astype(BF16)
    return _dot(ynorm_ref[...], wout_ref[...])


def _mamba_prompt_body(x_ref, g_ref, win_ref, wdt_ref, cw_ref, cb_ref, dtb_ref, a_ref, dsk_ref, ng_ref,
                       wout_ref, r_ref, o_ref, conv_ref, ssm_ref, ctx_ref, s_ref, y_ref, ynorm_ref,
                       *, tm, tiles_per_seq, inner, bc):
    i = pl.program_id(0)

    @pl.when(i % tiles_per_seq == 0)
    def _():
        ctx_ref[0:SSM_PAD, :] = jnp.zeros((SSM_PAD, ctx_ref.shape[1]), F32)
        s_ref[...] = jnp.zeros_like(s_ref)

    x = x_ref[...]
    h = _rms(x, g_ref[...]).astype(BF16)
    proj = _dot(h, win_ref[...])
    z = proj[:, :inner]
    xbc = proj[:, inner:]
    ctx_ref[SSM_PAD:SSM_PAD + tm, :] = xbc
    conv_ref[0] = ctx_ref[SSM_PAD + tm - (SSM_CONV - 1):SSM_PAD + tm, :]
    conv = cb_ref[...] + xbc * cw_ref[SSM_CONV - 1:SSM_CONV, :]
    for k in range(SSM_CONV - 1):
        off = SSM_PAD - (SSM_CONV - 1) + k
        conv = conv + ctx_ref[off:off + tm, :] * cw_ref[k:k + 1, :]
    ctx_ref[0:SSM_PAD, :] = ctx_ref[tm:tm + SSM_PAD, :]
    act = _silu(conv)
    xs = act[:, :inner]
    bm = act[:, inner:inner + bc].astype(BF16)
    cm = act[:, inner + bc:].astype(BF16)
    dt = _softplus(_dot(h, wdt_ref[...]) + dtb_ref[...])
    da = dt * a_ref[...]

    rows_i = lax.broadcasted_iota(jnp.int32, (CHUNK, CHUNK), 0)
    cols_i = lax.broadcasted_iota(jnp.int32, (CHUNK, CHUNK), 1)
    causal = rows_i >= cols_i
    tril = causal.astype(F32)
    gw = inner // SSM_GROUPS
    hpg = gw // SSM_HEAD_DIM
    lane_head = lax.broadcasted_iota(jnp.int32, (CHUNK, gw), 1) // SSM_HEAD_DIM
    r_mat = r_ref[...]

    for c in range(tm // CHUNK):
        rows = slice(c * CHUNK, (c + 1) * CHUNK)
        dt_c = dt[rows, :]
        cum = _dot_hi(tril, da[rows, :])
        cum_t = cum.T
        dt_t = dt_c.T
        ecum = jnp.exp(cum)
        e_full = _dot_split_lhs(ecum, r_mat)
        w_in = jnp.exp(cum[CHUNK - 1:CHUNK, :] - cum) * dt_c
        xs_c = xs[rows, :]
        xw = (xs_c * _dot_split_lhs(w_in, r_mat)).astype(BF16)
        xs_b = xs_c.astype(BF16)
        for g in range(SSM_GROUPS):
            gcols = slice(g * gw, (g + 1) * gw)
            ncols = slice(g * SSM_STATE, (g + 1) * SSM_STATE)
            b_g = bm[rows, ncols]
            c_g = cm[rows, ncols]
            cb = _dot_nt(c_g, b_g)
            x_g = xs_b[:, gcols]
            s_g = s_ref[g]
            y_g = _dot_nt(c_g, s_g.astype(BF16)) * e_full[:, gcols]
            for r in range(hpg):
                hd = g * hpg + r
                seg = cum[:, hd:hd + 1] - cum_t[hd:hd + 1, :]
                decay = jnp.where(causal, jnp.exp(jnp.where(causal, seg, 0.0)), 0.0)
                wts = (cb * decay * dt_t[hd:hd + 1, :]).astype(BF16)
                y_g = y_g + _dot(wts, jnp.where(lane_head == r, x_g, jnp.zeros_like(x_g)))
            y_ref[rows, gcols] = y_g
            upd = _dot_tn(xw[:, gcols], b_g)
            for r in range(hpg):
                hd = g * hpg + r
                hrows = slice(r * SSM_HEAD_DIM, (r + 1) * SSM_HEAD_DIM)
                s_ref[g, hrows, :] = s_g[hrows, :] * ecum[CHUNK - 1:CHUNK, hd:hd + 1] + upd[hrows, :]

    out = _ssd_post(y_ref[...], xs, z, dsk_ref[...], ng_ref[...], wout_ref, ynorm_ref, inner)
    o_ref[...] = x + out
    ssm_ref[0] = s_ref[...]


def _mamba_sample_pre_body(x_ref, g_ref, win_ref, wdt_ref, cw_ref, cb_ref, dtb_ref, a_ref, r_ref, cs_ref,
                           z_ref, xs_ref, b_ref, c_ref, xdt_ref, e_ref, newconv_ref, *, inner, bc):
    h = _rms(x_ref[...], g_ref[...]).astype(BF16)
    proj = _dot(h, win_ref[...])
    z_ref[...] = proj[:, :inner]
    xbc = proj[:, inner:]
    conv = cb_ref[...] + xbc * cw_ref[SSM_CONV - 1:SSM_CONV, :]
    for k in range(SSM_CONV - 1):
        conv = conv + cs_ref[k] * cw_ref[k:k + 1, :]
    for k in range(SSM_CONV - 2):
        newconv_ref[k] = cs_ref[k + 1]
    newconv_ref[SSM_CONV - 2] = xbc
    act = _silu(conv)
    xs = act[:, :inner]
    xs_ref[...] = xs
    b_ref[...] = act[:, inner:inner + bc]
    c_ref[...] = act[:, inner + bc:]
    dt = _softplus(_dot(h, wdt_ref[...]) + dtb_ref[...])
    xdt_ref[...] = xs * _dot_split_lhs(dt, r_ref[...])
    e_ref[...] = jnp.exp(dt * a_ref[...])


def _mamba_sample_state_body(e_ref, h0_ref, xdt_t_ref, b_ref, c_ref, hn_ref, y_ref, *, bt, inner, heads):
    i = pl.program_id(0)
    gw = inner // SSM_GROUPS
    hpg = gw // SSM_HEAD_DIM
    row_id = lax.broadcasted_iota(jnp.int32, (bt, bt * SSM_STATE), 0)
    lane_tok = lax.broadcasted_iota(jnp.int32, (bt, bt * SSM_STATE), 1) // SSM_STATE
    for g in range(SSM_GROUPS):
        rows = slice(g * gw, (g + 1) * gw)
        ncols = slice(g * SSM_STATE, (g + 1) * SSM_STATE)
        states = []
        for j in range(bt):
            b_row = b_ref[0, j:j + 1, ncols]
            for r in range(hpg):
                hrows = slice(g * gw + r * SSM_HEAD_DIM, g * gw + (r + 1) * SSM_HEAD_DIM)
                e = e_ref[(i * bt + j) * heads + g * hpg + r]
                hn_ref[j, hrows, :] = h0_ref[j, hrows, :] * e + xdt_t_ref[0, hrows, j:j + 1] * b_row
            states.append(hn_ref[j, rows, :].astype(BF16))
        c_g = c_ref[0, :, ncols]
        c_diag = jnp.where(row_id == lane_tok, jnp.concatenate([c_g] * bt, axis=1), 0.0).astype(BF16)
        y_ref[:, rows] = _dot_nt(c_diag, jnp.concatenate(states, axis=1))


def _mamba_sample_post_body(x_ref, y_ref, xs_ref, z_ref, dsk_ref, ng_ref, wout_ref, o_ref, ynorm_ref, *, inner):
    out = _ssd_post(y_ref[...], xs_ref[...], z_ref[...], dsk_ref[...], ng_ref[...], wout_ref, ynorm_ref, inner)
    o_ref[...] = x_ref[...] + out


def _mamba(X, n_prompt, n_batch, state_ssm_all, layer, state_conv, g, w_in, conv_w, conv_b, dt_bias, a_log, d_skip,
           norm_g, w_out, *, tm, bt):
    M, D = X.shape
    inner = w_out.shape[0]
    heads = a_log.shape[0]
    conv_dim = conv_w.shape[1]
    bc = (conv_dim - inner) // 2
    n_sample = M - n_prompt
    seq = n_prompt // n_batch
    tiles_per_seq = seq // tm
    gw = inner // SSM_GROUPS
    L = LANES_V7X

    g2 = g.reshape(1, D)
    w_main = w_in[:, :inner + conv_dim].astype(BF16)
    w_dt = jnp.zeros((D, L), F32).at[:, :heads].set(w_in[:, inner + conv_dim:]).astype(BF16)
    cb2 = conv_b.reshape(1, conv_dim)
    dtb = jnp.zeros((1, L), F32).at[0, :heads].set(dt_bias)
    a_neg = jnp.zeros((1, L), F32).at[0, :heads].set(-jnp.exp(a_log))
    dsk = jnp.repeat(d_skip, SSM_HEAD_DIM).reshape(1, inner)
    ng2 = norm_g.reshape(1, inner)
    w_out_b = w_out.astype(BF16)
    r_mat = (jnp.arange(L, dtype=jnp.int32)[:, None]
             == (jnp.arange(inner, dtype=jnp.int32) // SSM_HEAD_DIM)[None, :]).astype(BF16)

    X, conv_p, ssm_p = pl.pallas_call(
        functools.partial(_mamba_prompt_body, tm=tm, tiles_per_seq=tiles_per_seq, inner=inner, bc=bc),
        grid=(n_prompt // tm,),
        in_specs=[
            pl.BlockSpec((tm, D), lambda i: (i, 0)),
            _const_spec((1, D)),
            _const_spec((D, inner + conv_dim)),
            _const_spec((D, L)),
            _const_spec((SSM_CONV, conv_dim)),
            _const_spec((1, conv_dim)),
            _const_spec((1, L)),
            _const_spec((1, L)),
            _const_spec((1, inner)),
            _const_spec((1, inner)),
            _const_spec((inner, D)),
            _const_spec((L, inner)),
        ],
        out_specs=[
            pl.BlockSpec((tm, D), lambda i: (i, 0)),
            pl.BlockSpec((1, SSM_CONV - 1, conv_dim), lambda i: (i // tiles_per_seq, 0, 0)),
            pl.BlockSpec((1, SSM_GROUPS, gw, SSM_STATE), lambda i: (i // tiles_per_seq, 0, 0, 0)),
        ],
        out_shape=[
            jax.ShapeDtypeStruct((M, D), F32),
            jax.ShapeDtypeStruct((n_batch, SSM_CONV - 1, conv_dim), F32),
            jax.ShapeDtypeStruct((n_batch, SSM_GROUPS, gw, SSM_STATE), F32),
        ],
        scratch_shapes=[
            pltpu.VMEM((SSM_PAD + tm, conv_dim), F32),
            pltpu.VMEM((SSM_GROUPS, gw, SSM_STATE), F32),
            pltpu.VMEM((tm, inner), F32),
            pltpu.VMEM((tm, inner), BF16),
        ],
        input_output_aliases={0: 0},
        compiler_params=_cparams(("arbitrary",)),
        name="mamba_prompt",
    )(X, g2, w_main, w_dt, conv_w, cb2, dtb, a_neg, dsk, ng2, w_out_b, r_mat)

    sblk = n_prompt // n_sample
    cs_t = jnp.transpose(state_conv, (1, 0, 2))
    z, xs, b_m, c_m, xdt, e_tok, newconv_t = pl.pallas_call(
        functools.partial(_mamba_sample_pre_body, inner=inner, bc=bc),
        grid=(1,),
        in_specs=[
            pl.BlockSpec((n_sample, D), lambda i: (sblk, 0)),
            _const_spec((1, D)),
            _const_spec((D, inner + conv_dim)),
            _const_spec((D, L)),
            _const_spec((SSM_CONV, conv_dim)),
            _const_spec((1, conv_dim)),
            _const_spec((1, L)),
            _const_spec((1, L)),
            _const_spec((L, inner)),
            _const_spec((SSM_CONV - 1, n_sample, conv_dim)),
        ],
        out_specs=[
            pl.BlockSpec((n_sample, inner), lambda i: (0, 0)),
            pl.BlockSpec((n_sample, inner), lambda i: (0, 0)),
            pl.BlockSpec((n_sample, bc), lambda i: (0, 0)),
            pl.BlockSpec((n_sample, bc), lambda i: (0, 0)),
            pl.BlockSpec((n_sample, inner), lambda i: (0, 0)),
            pl.BlockSpec((n_sample, L), lambda i: (0, 0)),
            pl.BlockSpec((SSM_CONV - 1, n_sample, conv_dim), lambda i: (0, 0, 0)),
        ],
        out_shape=[
            jax.ShapeDtypeStruct((n_sample, inner), F32),
            jax.ShapeDtypeStruct((n_sample, inner), F32),
            jax.ShapeDtypeStruct((n_sample, bc), F32),
            jax.ShapeDtypeStruct((n_sample, bc), F32),
            jax.ShapeDtypeStruct((n_sample, inner), F32),
            jax.ShapeDtypeStruct((n_sample, L), F32),
            jax.ShapeDtypeStruct((SSM_CONV - 1, n_sample, conv_dim), F32),
        ],
        compiler_params=_cparams(("arbitrary",)),
        name="mamba_sample_pre",
    )(X, g2, w_main, w_dt, conv_w, cb2, dtb, a_neg, r_mat, cs_t)
    conv_s = jnp.transpose(newconv_t, (1, 0, 2))

    nblk = n_sample // bt

    def to_cols(arr):
        return jnp.transpose(arr.reshape(nblk, bt, inner), (0, 2, 1))

    h0 = state_ssm_all.reshape(state_ssm_all.shape[0], n_sample, inner, SSM_STATE)
    h_new, y_s = pl.pallas_call(
        functools.partial(_mamba_sample_state_body, bt=bt, inner=inner, heads=heads),
        grid_spec=pltpu.PrefetchScalarGridSpec(
            num_scalar_prefetch=1,
            grid=(nblk,),
            in_specs=[
                pl.BlockSpec((None, bt, inner, SSM_STATE), lambda i, e: (layer, i, 0, 0)),
                pl.BlockSpec((1, inner, bt), lambda i, e: (i, 0, 0)),
                pl.BlockSpec((1, bt, bc), lambda i, e: (i, 0, 0)),
                pl.BlockSpec((1, bt, bc), lambda i, e: (i, 0, 0)),
            ],
            out_specs=[
                pl.BlockSpec((bt, inner, SSM_STATE), lambda i, e: (i, 0, 0)),
                pl.BlockSpec((bt, inner), lambda i, e: (i, 0)),
            ],
        ),
        out_shape=[
            jax.ShapeDtypeStruct((n_sample, inner, SSM_STATE), F32),
            jax.ShapeDtypeStruct((n_sample, inner), F32),
        ],
        compiler_params=_cparams(("arbitrary",)),
        name="mamba_sample_state",
    )(e_tok[:, :heads].reshape(n_sample * heads), h0, to_cols(xdt), b_m.reshape(nblk, bt, bc),
      c_m.reshape(nblk, bt, bc))

    X = pl.pallas_call(
        functools.partial(_mamba_sample_post_body, inner=inner),
        grid=(1,),
        in_specs=[
            pl.BlockSpec((n_sample, D), lambda i: (sblk, 0)),
            pl.BlockSpec((n_sample, inner), lambda i: (0, 0)),
            pl.BlockSpec((n_sample, inner), lambda i: (0, 0)),
            pl.BlockSpec((n_sample, inner), lambda i: (0, 0)),
            _const_spec((1, inner)),
            _const_spec((1, inner)),
            _const_spec((inner, D)),
        ],
        out_specs=pl.BlockSpec((n_sample, D), lambda i: (sblk, 0)),
        out_shape=jax.ShapeDtypeStruct((M, D), F32),
        scratch_shapes=[pltpu.VMEM((n_sample, inner), BF16)],
        input_output_aliases={0: 0},
        compiler_params=_cparams(("arbitrary",)),
        name="mamba_sample_post",
    )(X, y_s, xs, z, dsk, ng2, w_out_b)

    ssm_p = ssm_p.reshape(n_batch, heads, SSM_HEAD_DIM, SSM_STATE)
    ssm_s = h_new.reshape(n_sample, heads, SSM_HEAD_DIM, SSM_STATE)
    return X, conv_p, conv_s, ssm_p, ssm_s


def _causal_dwconv_tile(ctx_ref, dw_ref, acc_ref, tm, d):
    S = SUBLANES_V7X
    first = CFM_PAD - (CFM_KERNEL - 1)
    rb, lb = CONV_ROW_BLOCK, CONV_LANE_BLOCK
    for r0 in range(0, tm, rb):
        for l0 in range(0, d, lb):
            lanes = slice(l0, l0 + lb)
            y = None
            for b in range(S):
                pb = None
                for a in range((first + CFM_KERNEL - 1) // S + 1):
                    k = S * a + b - first
                    if 0 <= k < CFM_KERNEL:
                        term = ctx_ref[r0 + S * a:r0 + S * a + rb + S, lanes] * dw_ref[k:k + 1, lanes]
                        pb = term if pb is None else pb + term
                part = pb[b:b + rb, :]
                y = part if y is None else y + part
            acc_ref[r0:r0 + rb, lanes] = y


def _cfm_prompt_body(x_ref, g_ref, w1_ref, b1_ref, dw_ref, dwb_ref, lng_ref, lnb_ref, w2_ref, b2_ref,
                     o_ref, buf_ref, ctx_ref, acc_ref, *, tm, tiles_per_seq, d):
    i = pl.program_id(0)

    @pl.when(i % tiles_per_seq == 0)
    def _():
        ctx_ref[0:CFM_PAD, :] = jnp.zeros((CFM_PAD, d), F32)
        ctx_ref[CFM_PAD + tm:CFM_PAD + tm + SUBLANES_V7X, :] = jnp.zeros((SUBLANES_V7X, d), F32)

    x = x_ref[...]
    h = _rms(x, g_ref[...]).astype(BF16)
    a = _dot(h, w1_ref[...]) + b1_ref[...]
    ctx_ref[CFM_PAD:CFM_PAD + tm, :] = a[:, :d] * _sigmoid(a[:, d:])
    _causal_dwconv_tile(ctx_ref, dw_ref, acc_ref, tm, d)
    buf_ref[0] = ctx_ref[CFM_PAD + tm - (CFM_KERNEL - 1):CFM_PAD + tm, :]
    ctx_ref[0:CFM_PAD, :] = ctx_ref[tm:tm + CFM_PAD, :]
    acc = acc_ref[...] + dwb_ref[...]
    hc = _silu(_layernorm(acc, lng_ref[...], lnb_ref[...])).astype(BF16)
    o_ref[...] = x + _dot(hc, w2_ref[...]) + b2_ref[...]


def _cfm_sample_body(x_ref, g_ref, w1_ref, b1_ref, dw_ref, dwb_ref, lng_ref, lnb_ref, w2_ref, b2_ref, cs_ref,
                     o_ref, new_ref, *, d):
    x = x_ref[...]
    h = _rms(x, g_ref[...]).astype(BF16)
    a = _dot(h, w1_ref[...]) + b1_ref[...]
    glu = a[:, :d] * _sigmoid(a[:, d:])
    acc = dwb_ref[...] + glu * dw_ref[CFM_KERNEL - 1:CFM_KERNEL, :]
    for k in range(CFM_KERNEL - 1):
        acc = acc + cs_ref[k] * dw_ref[k:k + 1, :]
    for k in range(CFM_KERNEL - 2):
        new_ref[k] = cs_ref[k + 1]
    new_ref[CFM_KERNEL - 2] = glu
    hc = _silu(_layernorm(acc, lng_ref[...], lnb_ref[...])).astype(BF16)
    o_ref[...] = x + _dot(hc, w2_ref[...]) + b2_ref[...]


def _conformer(X, n_prompt, n_batch, state_conv, g, w_pw1, b_pw1, dw_w, dw_b, ln_g, ln_b, w_pw2, b_pw2,
               *, tm, bt):
    M, D = X.shape
    n_sample = M - n_prompt
    seq = n_prompt // n_batch
    tiles_per_seq = seq // tm
    K = CFM_KERNEL
    g2 = g.reshape(1, D)
    w1, w2 = w_pw1.astype(BF16), w_pw2.astype(BF16)
    b1, b2 = b_pw1.reshape(1, 2 * D), b_pw2.reshape(1, D)
    dwb, lng, lnb = dw_b.reshape(1, D), ln_g.reshape(1, D), ln_b.reshape(1, D)
    weight_specs = [
        _const_spec((1, D)), _const_spec((D, 2 * D)), _const_spec((1, 2 * D)), _const_spec((K, D)),
        _const_spec((1, D)), _const_spec((1, D)), _const_spec((1, D)), _const_spec((D, D)), _const_spec((1, D)),
    ]
    weights = (g2, w1, b1, dw_w, dwb, lng, lnb, w2, b2)

    X, buf_p = pl.pallas_call(
        functools.partial(_cfm_prompt_body, tm=tm, tiles_per_seq=tiles_per_seq, d=D),
        grid=(n_prompt // tm,),
        in_specs=[pl.BlockSpec((tm, D), lambda i: (i, 0))] + weight_specs,
        out_specs=[
            pl.BlockSpec((tm, D), lambda i: (i, 0)),
            pl.BlockSpec((1, K - 1, D), lambda i: (i // tiles_per_seq, 0, 0)),
        ],
        out_shape=[jax.ShapeDtypeStruct((M, D), F32), jax.ShapeDtypeStruct((n_batch, K - 1, D), F32)],
        scratch_shapes=[pltpu.VMEM((CFM_PAD + tm + SUBLANES_V7X, D), F32), pltpu.VMEM((tm, D), F32)],
        input_output_aliases={0: 0},
        compiler_params=_cparams(("arbitrary",)),
        name="conformer_prompt",
    )(X, *weights)

    first = n_prompt // bt
    cs_t = jnp.transpose(state_conv, (1, 0, 2))
    X, new_t = pl.pallas_call(
        functools.partial(_cfm_sample_body, d=D),
        grid=(n_sample // bt,),
        in_specs=[pl.BlockSpec((bt, D), lambda i: (i + first, 0))] + weight_specs
        + [pl.BlockSpec((K - 1, bt, D), lambda i: (0, i, 0))],
        out_specs=[
            pl.BlockSpec((bt, D), lambda i: (i + first, 0)),
            pl.BlockSpec((K - 1, bt, D), lambda i: (0, i, 0)),
        ],
        out_shape=[jax.ShapeDtypeStruct((M, D), F32), jax.ShapeDtypeStruct((K - 1, n_sample, D), F32)],
        input_output_aliases={0: 0},
        compiler_params=_cparams(("arbitrary",)),
        name="conformer_sample",
    )(X, *weights, cs_t)
    return X, buf_p, jnp.transpose(new_t, (1, 0, 2))


def _stack(parts):
    return parts[0][None] if len(parts) == 1 else jnp.stack(parts)


def kernel(x_prompt, x_sample, state_ssm, state_conv_ssm, state_conv_cfm, norm_mix_g, norm_ffn_g, norm_final_g, a_w_in, a_b_in, a_ln_g, a_ln_b, a_w_s, a_b_s, a_w_out, b_w_in, b_conv_w, b_conv_b, b_dt_bias, b_a_log, b_d, b_norm_g, b_w_out, c_w_pw1, c_b_pw1, c_dw_w, c_dw_b, c_ln_g, c_ln_b, c_w_pw2, c_b_pw2, f_w_gate, f_w_up, f_w_down, e_w_router, e_w_gate, e_w_up, e_w_down):
    n_batch, seq, D = x_prompt.shape
    n_sample = x_sample.shape[0]
    n_prompt = n_batch * seq
    depth = norm_mix_g.shape[0]
    d_ff = f_w_gate.shape[2]
    assert x_sample.shape[1] == 1 and n_prompt % n_sample == 0 and seq % CHUNK == 0

    X = jnp.concatenate([x_prompt.reshape(n_prompt, D), x_sample.reshape(n_sample, D)], axis=0)
    v_p, v_s, ssm_p, ssm_s, cs_p, cs_s, cc_p, cc_s = [], [], [], [], [], [], [], []
    for i in range(depth):
        kind, j = i % 3, i // 3
        if kind == 0:
            X, vp, vs = _gmlp(X, n_prompt, n_batch, norm_mix_g[i], a_w_in[j], a_b_in[j], a_ln_g[j], a_ln_b[j],
                              a_w_s[j], a_b_s[j], a_w_out[j], tm=512)
            v_p.append(vp)
            v_s.append(vs)
        elif kind == 1:
            X, cbp, cbs, hlp, hls = _mamba(X, n_prompt, n_batch, state_ssm, j, state_conv_ssm[j], norm_mix_g[i],
                                           b_w_in[j], b_conv_w[j], b_conv_b[j], b_dt_bias[j], b_a_log[j], b_d[j],
                                           b_norm_g[j], b_w_out[j], tm=256, bt=8)
            cs_p.append(cbp)
            cs_s.append(cbs)
            ssm_p.append(hlp)
            ssm_s.append(hls)
        else:
            X, cbp, cbs = _conformer(X, n_prompt, n_batch, state_conv_cfm[j], norm_mix_g[i], c_w_pw1[j], c_b_pw1[j],
                                     c_dw_w[j], c_dw_b[j], c_ln_g[j], c_ln_b[j], c_w_pw2[j], c_b_pw2[j],
                                     tm=512, bt=32)
            cc_p.append(cbp)
            cc_s.append(cbs)
        k = i // 2
        if i % 2 == 0:
            X = _ffn_dense(X, n_prompt, norm_ffn_g[i], f_w_gate[k], f_w_up[k], f_w_down[k],
                           tm=512, f_chunk=FFN_CHUNK)
        else:
            X = _moe(X, n_prompt, norm_ffn_g[i], e_w_router[k], e_w_gate[k], e_w_up[k], e_w_down[k], norm_final_g,
                     tm_tok=384, tm_out=512, tm=512, f_chunk=FFN_CHUNK, final_norm=(i == depth - 1))
    assert depth % 2 == 0
    y_prompt, y_sample = X
    y_prompt = y_prompt.reshape(n_batch, seq, D)
    y_sample = y_sample.reshape(n_sample, 1, D)
    return (y_prompt, y_sample, _stack(v_p), _stack(v_s), _stack(ssm_p), _stack(ssm_s),
            _stack(cs_p), _stack(cs_s), _stack(cc_p), _stack(cc_s))
```

```python
import functools

import jax
import jax.numpy as jnp
from jax import lax
from jax.experimental import pallas as pl
from jax.experimental.pallas import tpu as pltpu

F32 = jnp.float32
BF16 = jnp.bfloat16
EPS = 1e-6
HIGHEST = lax.Precision.HIGHEST

LANES_V7X = 128
SUBLANES_V7X = 8
VMEM_LIMIT_V7X = 60 * 1024 * 1024

CHUNK = 128
GM_GROUPS = 8
SSM_GROUPS = 8
SSM_HEAD_DIM = 64
SSM_STATE = 128
SSM_CONV = 4
CFM_KERNEL = 31
N_EXPERTS = 8
CFM_PAD = 32
SSM_PAD = 8
FFN_CHUNK = 256
CONV_ROW_BLOCK = 128
CONV_LANE_BLOCK = 256


def _cparams(sem):
    return pltpu.CompilerParams(dimension_semantics=sem, vmem_limit_bytes=VMEM_LIMIT_V7X)


def _const_spec(shape):
    nd = len(shape)
    return pl.BlockSpec(shape, lambda *_: (0,) * nd, pipeline_mode=pl.Buffered(1))


def _dot(a, b):
    return jnp.dot(a, b, preferred_element_type=F32)


def _dot_nt(a, b):
    return lax.dot_general(a, b, (((1,), (1,)), ((), ())), preferred_element_type=F32)


def _dot_tn(a, b):
    return lax.dot_general(a, b, (((0,), (0,)), ((), ())), preferred_element_type=F32)


def _dot_hi(a, b):
    return jnp.dot(a, b, precision=HIGHEST, preferred_element_type=F32)


def _split_bf16(a):
    hi = a.astype(BF16)
    return hi, (a - hi.astype(F32)).astype(BF16)


def _dot_split_lhs(a, b_bf16):
    hi, lo = _split_bf16(a)
    return _dot(hi, b_bf16) + _dot(lo, b_bf16)


def _rms(x, g):
    return x * lax.rsqrt(jnp.mean(x * x, axis=-1, keepdims=True) + EPS) * g


def _layernorm(x, g, b):
    xc = x - jnp.mean(x, axis=-1, keepdims=True)
    return xc * lax.rsqrt(jnp.mean(xc * xc, axis=-1, keepdims=True) + EPS) * g + b


def _sigmoid(x):
    return 1.0 / (1.0 + jnp.exp(-x))


def _silu(x):
    return x * _sigmoid(x)


def _gelu(x):
    return 0.5 * x * (1.0 + lax.erf(x * (2.0 ** -0.5)))


def _softplus(x):
    return jnp.maximum(x, 0.0) + jnp.log(1.0 + jnp.exp(-jnp.abs(x)))


def _gmlp_prompt_body(x_ref, g_ref, win_ref, bin_ref, lng_ref, lnb_ref, ws_ref, bs_ref, wout_ref,
                      o_ref, v_ref, gated_ref, *, tm, width):
    x = x_ref[...]
    h = _rms(x, g_ref[...]).astype(BF16)
    u = _gelu(_dot(h, win_ref[:, :width]) + bin_ref[:, :width])
    hv = _gelu(_dot(h, win_ref[:, width:]) + bin_ref[:, width:])
    v = _layernorm(hv, lng_ref[...], lnb_ref[...])
    v_ref[0] = v[tm - CHUNK:, :]
    vb = v.astype(BF16)
    gdim = width // GM_GROUPS
    causal = (lax.broadcasted_iota(jnp.int32, (CHUNK, CHUNK), 0)
              >= lax.broadcasted_iota(jnp.int32, (CHUNK, CHUNK), 1))
    for g in range(GM_GROUPS):
        wc = jnp.where(causal, ws_ref[g], 0.0).astype(BF16)
        bias = bs_ref[:, g:g + 1]
        for c in range(tm // CHUNK):
            rows = slice(c * CHUNK, (c + 1) * CHUNK)
            cols = slice(g * gdim, (g + 1) * gdim)
            s = _dot(wc, vb[rows, cols]) + bias
            gated_ref[rows, cols] = (u[rows, cols] * s).astype(BF16)
    o_ref[...] = x + _dot(gated_ref[...], wout_ref[...])


def _gmlp_sample_body(x_ref, g_ref, win_ref, bin_ref, lng_ref, lnb_ref, ws0_ref, bs0_ref, wout_ref,
                      o_ref, v_ref, *, width):
    x = x_ref[...]
    h = _rms(x, g_ref[...]).astype(BF16)
    u = _gelu(_dot(h, win_ref[:, :width]) + bin_ref[:, :width])
    hv = _gelu(_dot(h, win_ref[:, width:]) + bin_ref[:, width:])
    v = _layernorm(hv, lng_ref[...], lnb_ref[...])
    v_ref[...] = v
    s = v * ws0_ref[...] + bs0_ref[...]
    o_ref[...] = x + _dot((u * s).astype(BF16), wout_ref[...])


def _gmlp(X, n_prompt, n_batch, g, w_in, b_in, ln_g, ln_b, w_s, b_s, w_out, *, tm):
    M, D = X.shape
    width = w_out.shape[0]
    seq = n_prompt // n_batch
    tiles_per_seq = seq // tm
    n_sample = M - n_prompt
    g2 = g.reshape(1, D)
    b_in2 = b_in.reshape(1, 2 * width)
    ln_g2, ln_b2 = ln_g.reshape(1, width), ln_b.reshape(1, width)
    w_in_b, w_out_b = w_in.astype(BF16), w_out.astype(BF16)

    X, v_p = pl.pallas_call(
        functools.partial(_gmlp_prompt_body, tm=tm, width=width),
        grid=(n_prompt // tm,),
        in_specs=[
            pl.BlockSpec((tm, D), lambda i: (i, 0)),
            _const_spec((1, D)),
            _const_spec((D, 2 * width)),
            _const_spec((1, 2 * width)),
            _const_spec((1, width)),
            _const_spec((1, width)),
            _const_spec((GM_GROUPS, CHUNK, CHUNK)),
            _const_spec((CHUNK, GM_GROUPS)),
            _const_spec((width, D)),
        ],
        out_specs=[
            pl.BlockSpec((tm, D), lambda i: (i, 0)),
            pl.BlockSpec((1, CHUNK, width), lambda i: (i // tiles_per_seq, 0, 0)),
        ],
        out_shape=[jax.ShapeDtypeStruct((M, D), F32),
                   jax.ShapeDtypeStruct((n_batch, CHUNK, width), F32)],
        scratch_shapes=[pltpu.VMEM((tm, width), BF16)],
        input_output_aliases={0: 0},
        compiler_params=_cparams(("arbitrary",)),
        name="gmlp_prompt",
    )(X, g2, w_in_b, b_in2, ln_g2, ln_b2, w_s, b_s.T, w_out_b)

    gdim = width // GM_GROUPS
    ws0 = jnp.repeat(w_s[:, 0, 0], gdim).reshape(1, width)
    bs0 = jnp.repeat(b_s[:, 0], gdim).reshape(1, width)
    sblk = n_prompt // n_sample
    X, v_s = pl.pallas_call(
        functools.partial(_gmlp_sample_body, width=width),
        grid=(1,),
        in_specs=[
            pl.BlockSpec((n_sample, D), lambda i: (sblk, 0)),
            _const_spec((1, D)),
            _const_spec((D, 2 * width)),
            _const_spec((1, 2 * width)),
            _const_spec((1, width)),
            _const_spec((1, width)),
            _const_spec((1, width)),
            _const_spec((1, width)),
            _const_spec((width, D)),
        ],
        out_specs=[
            pl.BlockSpec((n_sample, D), lambda i: (sblk, 0)),
            pl.BlockSpec((n_sample, width), lambda i: (0, 0)),
        ],
        out_shape=[jax.ShapeDtypeStruct((M, D), F32),
                   jax.ShapeDtypeStruct((n_sample, width), F32)],
        input_output_aliases={0: 0},
        compiler_params=_cparams(("arbitrary",)),
        name="gmlp_sample",
    )(X, g2, w_in_b, b_in2, ln_g2, ln_b2, ws0, bs0, w_out_b)
    return X, v_p, v_s.reshape(n_sample, 1, width)


def _swiglu_tile(h, wg_ref, wu_ref, wd_ref, f_chunk):
    d_ff = wd_ref.shape[0]
    acc = None
    for f in range(d_ff // f_chunk):
        cols = slice(f * f_chunk, (f + 1) * f_chunk)
        a = _dot(h, wg_ref[:, cols].astype(BF16))
        b = _dot(h, wu_ref[:, cols].astype(BF16))
        part = _dot((_silu(a) * b).astype(BF16), wd_ref[cols, :].astype(BF16))
        acc = part if acc is None else acc + part
    return acc


def _ffn_dense_body(x_ref, g_ref, wg_ref, wu_ref, wd_ref, o_ref, *, f_chunk):
    x = x_ref[...]
    h = _rms(x, g_ref[...]).astype(BF16)
    o_ref[...] = x + _swiglu_tile(h, wg_ref, wu_ref, wd_ref, f_chunk)


def _ffn_dense_rows(X, g2, wg, wu, wd, layer, *, tm, first_block, n_tiles, f_chunk):
    M, D = X.shape
    d_ff = wd.shape[1]

    def layer_spec(rows, cols):
        return pl.BlockSpec((None, rows, cols), lambda i: (layer, 0, 0), pipeline_mode=pl.Buffered(1))

    return pl.pallas_call(
        functools.partial(_ffn_dense_body, f_chunk=f_chunk),
        grid=(n_tiles,),
        in_specs=[
            pl.BlockSpec((tm, D), lambda i: (i + first_block, 0)),
            _const_spec((1, D)),
            layer_spec(D, d_ff),
            layer_spec(D, d_ff),
            layer_spec(d_ff, D),
        ],
        out_specs=pl.BlockSpec((tm, D), lambda i: (i + first_block, 0)),
        out_shape=jax.ShapeDtypeStruct((M, D), F32),
        input_output_aliases={0: 0},
        compiler_params=_cparams(("arbitrary",)),
        name="ffn_dense",
    )(X, g2, wg, wu, wd)


def _ffn_dense(X, n_prompt, g, w_gate, w_up, w_down, layer, *, tm, f_chunk):
    M, D = X.shape
    n_sample = M - n_prompt
    g2 = g.reshape(1, D)
    X = _ffn_dense_rows(X, g2, w_gate, w_up, w_down, layer, tm=tm, first_block=0, n_tiles=n_prompt // tm,
                        f_chunk=f_chunk)
    X = _ffn_dense_rows(X, g2, w_gate, w_up, w_down, layer, tm=n_sample, first_block=n_prompt // n_sample,
                        n_tiles=1, f_chunk=f_chunk)
    return X


def _router_body(x_ref, g_ref, wr_hi_ref, wr_lo_ref, idx_ref, wts_ref):
    h_hi, h_lo = _split_bf16(_rms(x_ref[...], g_ref[...]))
    logits = _dot(h_hi, wr_hi_ref[...]) + (_dot(h_lo, wr_hi_ref[...]) + _dot(h_hi, wr_lo_ref[...]))
    lane = lax.broadcasted_iota(jnp.int32, logits.shape, 1)
    neg = jnp.float32(-jnp.inf)
    logits = jnp.where(lane < N_EXPERTS, logits, neg)
    m1 = jnp.max(logits, axis=-1, keepdims=True)
    i1 = jnp.min(jnp.where(logits == m1, lane, LANES_V7X), axis=-1, keepdims=True)
    rest = jnp.where(lane == i1, neg, logits)
    m2 = jnp.max(rest, axis=-1, keepdims=True)
    i2 = jnp.min(jnp.where(rest == m2, lane, LANES_V7X), axis=-1, keepdims=True)
    e = jnp.exp(m2 - m1)
    w1 = 1.0 / (1.0 + e)
    w2 = e / (1.0 + e)
    idx_ref[...] = jnp.where(lane == 0, i1, jnp.where(lane == 1, i2, 0))
    wts_ref[...] = jnp.where(lane == 0, w1, jnp.where(lane == 1, w2, 0.0))


def _slot_table_body(pos1_ref, pos2_ref, pad_lo_ref, pad_hi_ref, tok_ref, *, n_tokens, n_ranges):
    def pad(s, carry):
        tok_ref[s] = 0
        return carry

    for r in range(n_ranges):
        lax.fori_loop(pad_lo_ref[r], pad_hi_ref[r], pad, 0)

    def place(t, carry):
        tok_ref[pos1_ref[t]] = t
        tok_ref[pos2_ref[t]] = t
        return carry

    lax.fori_loop(0, n_tokens, place, 0, unroll=8)


def _issue_row_gather(src_hbm, idx_ref, base, buf, sem, n_rows):
    def body(r, carry):
        pltpu.make_async_copy(src_hbm.at[pl.ds(idx_ref[base + r], 1), :], buf.at[pl.ds(r, 1), :], sem).start()
        return carry
    lax.fori_loop(0, n_rows, body, 0, unroll=8)


def _wait_row_gather(src_hbm, buf, sem, n_rows):
    pltpu.make_async_copy(src_hbm.at[pl.ds(0, n_rows), :], buf, sem).wait()


def _moe_ffn_body(tile_expert_ref, n_tiles_ref, tok_ref, x_hbm, g_ref, wg_ref, wu_ref, wd_ref,
                  y_ref, buf_ref, sem_ref, *, tm, f_chunk):
    t = pl.program_id(0)
    n_tiles = n_tiles_ref[0]
    slot = t % 2

    @pl.when(t == 0)
    def _():
        _issue_row_gather(x_hbm, tok_ref, 0, buf_ref.at[0], sem_ref.at[0], tm)

    @pl.when(t + 1 < n_tiles)
    def _():
        _issue_row_gather(x_hbm, tok_ref, (t + 1) * tm, buf_ref.at[1 - slot], sem_ref.at[1 - slot], tm)

    @pl.when(t < n_tiles)
    def _():
        _wait_row_gather(x_hbm, buf_ref.at[slot], sem_ref.at[slot], tm)
        h = _rms(buf_ref[slot], g_ref[...]).astype(BF16)
        y_ref[...] = _swiglu_tile(h, wg_ref, wu_ref, wd_ref, f_chunk)

    @pl.when(t >= n_tiles)
    def _():
        y_ref[...] = jnp.zeros_like(y_ref)


def _moe_combine_body(p1_ref, p2_ref, x_ref, wts_ref, y_hbm, gf_ref, o_ref, buf1_ref, buf2_ref, sem_ref,
                      *, tm, row0, n_steps, final_norm):
    i = pl.program_id(0)
    slot = i % 2

    def issue(step, s):
        _issue_row_gather(y_hbm, p1_ref, row0 + step * tm, buf1_ref.at[s], sem_ref.at[0, s], tm)
        _issue_row_gather(y_hbm, p2_ref, row0 + step * tm, buf2_ref.at[s], sem_ref.at[1, s], tm)

    @pl.when(i == 0)
    def _():
        issue(0, 0)

    @pl.when(i + 1 < n_steps)
    def _():
        issue(i + 1, 1 - slot)

    _wait_row_gather(y_hbm, buf1_ref.at[slot], sem_ref.at[0, slot], tm)
    _wait_row_gather(y_hbm, buf2_ref.at[slot], sem_ref.at[1, slot], tm)
    out = x_ref[...] + (wts_ref[:, 0:1] * buf1_ref[slot] + wts_ref[:, 1:2] * buf2_ref[slot])
    if final_norm:
        out = _rms(out, gf_ref[...])
    o_ref[...] = out


def _moe(X, n_prompt, g, w_router, w_gate, w_up, w_down, layer, g_final, *, tm_tok, tm_out, tm, f_chunk,
         final_norm):
    M, D = X.shape
    E = N_EXPERTS
    d_ff = w_down.shape[2]
    g2 = g.reshape(1, D)
    L = LANES_V7X
    wr = jnp.zeros((D, L), F32).at[:, :E].set(w_router)
    wr_hi = wr.astype(BF16)
    wr_lo = (wr - wr_hi.astype(F32)).astype(BF16)

    idx, wts = pl.pallas_call(
        _router_body,
        grid=(M // tm_tok,),
        in_specs=[pl.BlockSpec((tm_tok, D), lambda i: (i, 0)), _const_spec((1, D)), _const_spec((D, L)),
                  _const_spec((D, L))],
        out_specs=[pl.BlockSpec((tm_tok, L), lambda i: (i, 0)), pl.BlockSpec((tm_tok, L), lambda i: (i, 0))],
        out_shape=[jax.ShapeDtypeStruct((M, L), jnp.int32), jax.ShapeDtypeStruct((M, L), F32)],
        compiler_params=_cparams(("arbitrary",)),
        name="moe_router",
    )(X, g2, wr_hi, wr_lo)

    i1, i2 = idx[:, 0], idx[:, 1]
    eids = jnp.arange(E, dtype=jnp.int32)
    sel = ((i1[:, None] == eids) | (i2[:, None] == eids)).astype(jnp.int32)
    count = jnp.sum(sel, axis=0)
    tiles_e = (count + tm - 1) // tm
    tiles_end = jnp.cumsum(tiles_e)
    n_tiles = tiles_end[-1]
    start_e = (tiles_end - tiles_e) * tm
    pos = start_e[None, :] + jnp.cumsum(sel, axis=0) - sel
    pos1 = jnp.sum(jnp.where(i1[:, None] == eids, pos, 0), axis=1).astype(jnp.int32)
    pos2 = jnp.sum(jnp.where(i2[:, None] == eids, pos, 0), axis=1).astype(jnp.int32)
    max_tiles = (2 * M + E * (tm - 1)) // tm
    n_slots = max_tiles * tm
    pad_lo = jnp.concatenate([start_e + count, (n_tiles * tm)[None]]).astype(jnp.int32)
    pad_hi = jnp.concatenate([tiles_end * tm, jnp.array([n_slots], jnp.int32)]).astype(jnp.int32)
    tile_ids = jnp.arange(max_tiles, dtype=jnp.int32)
    tile_expert = jnp.sum((jnp.minimum(tile_ids, n_tiles - 1)[:, None] >= tiles_end[None, :]).astype(jnp.int32),
                          axis=1)
    tile_expert = jnp.minimum(tile_expert, E - 1).astype(jnp.int32)
    n_tiles_arr = n_tiles.astype(jnp.int32).reshape(1)

    smem = pl.BlockSpec(memory_space=pltpu.SMEM)
    tok_of_slot = pl.pallas_call(
        functools.partial(_slot_table_body, n_tokens=M, n_ranges=E + 1),
        in_specs=[smem, smem, smem, smem],
        out_specs=smem,
        out_shape=jax.ShapeDtypeStruct((n_slots,), jnp.int32),
        name="moe_slot_table",
    )(pos1, pos2, pad_lo, pad_hi)

    def expert_spec(rows, cols):
        return pl.BlockSpec((None, None, rows, cols), lambda t, te, nt, tok: (layer, te[t], 0, 0),
                            pipeline_mode=pl.Buffered(1))

    y_sorted = pl.pallas_call(
        functools.partial(_moe_ffn_body, tm=tm, f_chunk=f_chunk),
        grid_spec=pltpu.PrefetchScalarGridSpec(
            num_scalar_prefetch=3,
            grid=(max_tiles,),
            in_specs=[
                pl.BlockSpec(memory_space=pl.ANY),
                pl.BlockSpec((1, D), lambda t, te, nt, tok: (0, 0)),
                expert_spec(D, d_ff),
                expert_spec(D, d_ff),
                expert_spec(d_ff, D),
            ],
            out_specs=pl.BlockSpec((tm, D), lambda t, te, nt, tok: (t, 0)),
            scratch_shapes=[pltpu.VMEM((2, tm, D), F32), pltpu.SemaphoreType.DMA((2,))],
        ),
        out_shape=jax.ShapeDtypeStruct((n_slots, D), F32),
        compiler_params=_cparams(("arbitrary",)),
        name="moe_ffn",
    )(tile_expert, n_tiles_arr, tok_of_slot, X, g2, w_gate, w_up, w_down)

    def combine_rows(tm_c, first_block, n_steps, in_place):
        out_first = first_block if in_place else 0
        return pl.pallas_call(
            functools.partial(_moe_combine_body, tm=tm_c, row0=first_block * tm_c, n_steps=n_steps,
                              final_norm=final_norm),
            grid_spec=pltpu.PrefetchScalarGridSpec(
                num_scalar_prefetch=2,
                grid=(n_steps,),
                in_specs=[
                    pl.BlockSpec((tm_c, D), lambda i, p1, p2: (i + first_block, 0)),
                    pl.BlockSpec((tm_c, L), lambda i, p1, p2: (i + first_block, 0)),
                    pl.BlockSpec(memory_space=pl.ANY),
                    pl.BlockSpec((1, D), lambda i, p1, p2: (0, 0)),
                ],
                out_specs=pl.BlockSpec((tm_c, D), lambda i, p1, p2: (i + out_first, 0)),
                scratch_shapes=[pltpu.VMEM((2, tm_c, D), F32), pltpu.VMEM((2, tm_c, D), F32),
                                pltpu.SemaphoreType.DMA((2, 2))],
            ),
            out_shape=jax.ShapeDtypeStruct((M if in_place else n_steps * tm_c, D), F32),
            input_output_aliases={2: 0} if in_place else {},
            compiler_params=_cparams(("arbitrary",)),
            name="moe_combine",
        )(pos1, pos2, X, wts, y_sorted, g_final.reshape(1, D))

    if not final_norm:
        return combine_rows(tm_tok, 0, M // tm_tok, True)
    n_sample = M - n_prompt
    y_p = combine_rows(tm_out, 0, n_prompt // tm_out, False)
    y_s = combine_rows(n_sample, n_prompt // n_sample, 1, False)
    return y_p, y_s


def _ssd_post(y, xs, z, dsk, ng, wout_ref, ynorm_ref, inner):
    y = (y + dsk * xs) * _silu(z)
    gw = inner // SSM_GROUPS
    for g in range(SSM_GROUPS):
        cols = slice(g * gw, (g + 1) * gw)
        yg = y[:, cols]
        yg = yg * lax.rsqrt(jnp.mean(yg * yg, axis=-1, keepdims=True) + EPS) * ng[:, cols]
        ynorm_ref[:, cols] = yg.astype(BF16)
    return _dot(ynorm_ref[...], wout_ref[...])


def _mamba_prompt_body(x_ref, g_ref, win_ref, wdt_ref, cw_ref, cb_ref, dtb_ref, a_ref, dsk_ref, ng_ref,
                       wout_ref, r_ref, o_ref, conv_ref, ssm_ref, ctx_ref, s_ref, y_ref, ynorm_ref,
                       *, tm, tiles_per_seq, inner, bc):
    i = pl.program_id(0)

    @pl.when(i % tiles_per_seq == 0)
    def _():
        ctx_ref[0:SSM_PAD, :] = jnp.zeros((SSM_PAD, ctx_ref.shape[1]), F32)
        s_ref[...] = jnp.zeros_like(s_ref)

    x = x_ref[...]
    h = _rms(x, g_ref[...]).astype(BF16)
    proj = _dot(h, win_ref[...])
    z = proj[:, :inner]
    xbc = proj[:, inner:]
    ctx_ref[SSM_PAD:SSM_PAD + tm, :] = xbc
    conv_ref[0] = ctx_ref[SSM_PAD + tm - (SSM_CONV - 1):SSM_PAD + tm, :]
    conv = cb_ref[...] + xbc * cw_ref[SSM_CONV - 1:SSM_CONV, :]
    for k in range(SSM_CONV - 1):
        off = SSM_PAD - (SSM_CONV - 1) + k
        conv = conv + ctx_ref[off:off + tm, :] * cw_ref[k:k + 1, :]
    ctx_ref[0:SSM_PAD, :] = ctx_ref[tm:tm + SSM_PAD, :]
    act = _silu(conv)
    xs = act[:, :inner]
    bm = act[:, inner:inner + bc].astype(BF16)
    cm = act[:, inner + bc:].astype(BF16)
    dt = _softplus(_dot(h, wdt_ref[...]) + dtb_ref[...])
    da = dt * a_ref[...]

    rows_i = lax.broadcasted_iota(jnp.int32, (CHUNK, CHUNK), 0)
    cols_i = lax.broadcasted_iota(jnp.int32, (CHUNK, CHUNK), 1)
    causal = rows_i >= cols_i
    tril = causal.astype(F32)
    gw = inner // SSM_GROUPS
    hpg = gw // SSM_HEAD_DIM
    lane_head = lax.broadcasted_iota(jnp.int32, (CHUNK, gw), 1) // SSM_HEAD_DIM
    r_mat = r_ref[...]

    for c in range(tm // CHUNK):
        rows = slice(c * CHUNK, (c + 1) * CHUNK)
        dt_c = dt[rows, :]
        cum = _dot_hi(tril, da[rows, :])
        cum_t = cum.T
        dt_t = dt_c.T
        ecum = jnp.exp(cum)
        e_full = _dot_split_lhs(ecum, r_mat)
        w_in = jnp.exp(cum[CHUNK - 1:CHUNK, :] - cum) * dt_c
        xs_c = xs[rows, :]
        xw = (xs_c * _dot_split_lhs(w_in, r_mat)).astype(BF16)
        xs_b = xs_c.astype(BF16)
        for g in range(SSM_GROUPS):
            gcols = slice(g * gw, (g + 1) * gw)
            ncols = slice(g * SSM_STATE, (g + 1) * SSM_STATE)
            b_g = bm[rows, ncols]
            c_g = cm[rows, ncols]
            cb = _dot_nt(c_g, b_g)
            x_g = xs_b[:, gcols]
            s_g = s_ref[g]
            y_g = _dot_nt(c_g, s_g.astype(BF16)) * e_full[:, gcols]
            for r in range(hpg):
                hd = g * hpg + r
                seg = cum[:, hd:hd + 1] - cum_t[hd:hd + 1, :]
                decay = jnp.where(causal, jnp.exp(jnp.where(causal, seg, 0.0)), 0.0)
                wts = (cb * decay * dt_t[hd:hd + 1, :]).astype(BF16)
                y_g = y_g + _dot(wts, jnp.where(lane_head == r, x_g, jnp.zeros_like(x_g)))
            y_ref[rows, gcols] = y_g
            upd = _dot_tn(xw[:, gcols], b_g)
            for r in range(hpg):
                hd = g * hpg + r
                hrows = slice(r * SSM_HEAD_DIM, (r + 1) * SSM_HEAD_DIM)
                s_ref[g, hrows, :] = s_g[hrows, :] * ecum[CHUNK - 1:CHUNK, hd:hd + 1] + upd[hrows, :]

    out = _ssd_post(y_ref[...], xs, z, dsk_ref[...], ng_ref[...], wout_ref, ynorm_ref, inner)
    o_ref[...] = x + out
    ssm_ref[0] = s_ref[...]


def _mamba_sample_pre_body(x_ref, g_ref, win_ref, wdt_ref, cw_ref, cb_ref, dtb_ref, a_ref, r_ref, cs_ref,
                           z_ref, xs_ref, b_ref, c_ref, xdt_ref, e_ref, newconv_ref, *, inner, bc):
    h = _rms(x_ref[...], g_ref[...]).astype(BF16)
    proj = _dot(h, win_ref[...])
    z_ref[...] = proj[:, :inner]
    xbc = proj[:, inner:]
    conv = cb_ref[...] + xbc * cw_ref[SSM_CONV - 1:SSM_CONV, :]
    for k in range(SSM_CONV - 1):
        conv = conv + cs_ref[k] * cw_ref[k:k + 1, :]
    for k in range(SSM_CONV - 2):
        newconv_ref[k] = cs_ref[k + 1]
    newconv_ref[SSM_CONV - 2] = xbc
    act = _silu(conv)
    xs = act[:, :inner]
    xs_ref[...] = xs
    b_ref[...] = act[:, inner:inner + bc]
    c_ref[...] = act[:, inner + bc:]
    dt = _softplus(_dot(h, wdt_ref[...]) + dtb_ref[...])
    xdt_ref[...] = xs * _dot_split_lhs(dt, r_ref[...])
    e_ref[...] = jnp.exp(dt * a_ref[...])


def _mamba_sample_state_body(e_ref, h0_ref, xdt_t_ref, b_ref, c_ref, hn_ref, y_ref, *, bt, inner, heads):
    i = pl.program_id(0)
    gw = inner // SSM_GROUPS
    hpg = gw // SSM_HEAD_DIM
    row_id = lax.broadcasted_iota(jnp.int32, (bt, bt * SSM_STATE), 0)
    lane_tok = lax.broadcasted_iota(jnp.int32, (bt, bt * SSM_STATE), 1) // SSM_STATE
    for g in range(SSM_GROUPS):
        rows = slice(g * gw, (g + 1) * gw)
        ncols = slice(g * SSM_STATE, (g + 1) * SSM_STATE)
        states = []
        for j in range(bt):
            b_row = b_ref[0, j:j + 1, ncols]
            for r in range(hpg):
                hrows = slice(g * gw + r * SSM_HEAD_DIM, g * gw + (r + 1) * SSM_HEAD_DIM)
                e = e_ref[(i * bt + j) * heads + g * hpg + r]
                hn_ref[j, hrows, :] = h0_ref[j, hrows, :] * e + xdt_t_ref[0, hrows, j:j + 1] * b_row
            states.append(hn_ref[j, rows, :].astype(BF16))
        c_g = c_ref[0, :, ncols]
        c_diag = jnp.where(row_id == lane_tok, jnp.concatenate([c_g] * bt, axis=1), 0.0).astype(BF16)
        y_ref[:, rows] = _dot_nt(c_diag, jnp.concatenate(states, axis=1))


def _mamba_sample_post_body(x_ref, y_ref, xs_ref, z_ref, dsk_ref, ng_ref, wout_ref, o_ref, ynorm_ref, *, inner):
    out = _ssd_post(y_ref[...], xs_ref[...], z_ref[...], dsk_ref[...], ng_ref[...], wout_ref, ynorm_ref, inner)
    o_ref[...] = x_ref[...] + out


def _mamba(X, n_prompt, n_batch, state_ssm_all, layer, state_conv, g, w_in, conv_w, conv_b, dt_bias, a_log, d_skip,
           norm_g, w_out, *, tm, bt):
    M, D = X.shape
    inner = w_out.shape[0]
    heads = a_log.shape[0]
    conv_dim = conv_w.shape[1]
    bc = (conv_dim - inner) // 2
    n_sample = M - n_prompt
    seq = n_prompt // n_batch
    tiles_per_seq = seq // tm
    gw = inner // SSM_GROUPS
    L = LANES_V7X

    g2 = g.reshape(1, D)
    w_main = w_in[:, :inner + conv_dim].astype(BF16)
    w_dt = jnp.zeros((D, L), F32).at[:, :heads].set(w_in[:, inner + conv_dim:]).astype(BF16)
    cb2 = conv_b.reshape(1, conv_dim)
    dtb = jnp.zeros((1, L), F32).at[0, :heads].set(dt_bias)
    a_neg = jnp.zeros((1, L), F32).at[0, :heads].set(-jnp.exp(a_log))
    dsk = jnp.repeat(d_skip, SSM_HEAD_DIM).reshape(1, inner)
    ng2 = norm_g.reshape(1, inner)
    w_out_b = w_out.astype(BF16)
    r_mat = (jnp.arange(L, dtype=jnp.int32)[:, None]
             == (jnp.arange(inner, dtype=jnp.int32) // SSM_HEAD_DIM)[None, :]).astype(BF16)

    X, conv_p, ssm_p = pl.pallas_call(
        functools.partial(_mamba_prompt_body, tm=tm, tiles_per_seq=tiles_per_seq, inner=inner, bc=bc),
        grid=(n_prompt // tm,),
        in_specs=[
            pl.BlockSpec((tm, D), lambda i: (i, 0)),
            _const_spec((1, D)),
            _const_spec((D, inner + conv_dim)),
            _const_spec((D, L)),
            _const_spec((SSM_CONV, conv_dim)),
            _const_spec((1, conv_dim)),
            _const_spec((1, L)),
            _const_spec((1, L)),
            _const_spec((1, inner)),
            _const_spec((1, inner)),
            _const_spec((inner, D)),
            _const_spec((L, inner)),
        ],
        out_specs=[
            pl.BlockSpec((tm, D), lambda i: (i, 0)),
            pl.BlockSpec((1, SSM_CONV - 1, conv_dim), lambda i: (i // tiles_per_seq, 0, 0)),
            pl.BlockSpec((1, SSM_GROUPS, gw, SSM_STATE), lambda i: (i // tiles_per_seq, 0, 0, 0)),
        ],
        out_shape=[
            jax.ShapeDtypeStruct((M, D), F32),
            jax.ShapeDtypeStruct((n_batch, SSM_CONV - 1, conv_dim), F32),
            jax.ShapeDtypeStruct((n_batch, SSM_GROUPS, gw, SSM_STATE), F32),
        ],
        scratch_shapes=[
            pltpu.VMEM((SSM_PAD + tm, conv_dim), F32),
            pltpu.VMEM((SSM_GROUPS, gw, SSM_STATE), F32),
            pltpu.VMEM((tm, inner), F32),
            pltpu.VMEM((tm, inner), BF16),
        ],
        input_output_aliases={0: 0},
        compiler_params=_cparams(("arbitrary",)),
        name="mamba_prompt",
    )(X, g2, w_main, w_dt, conv_w, cb2, dtb, a_neg, dsk, ng2, w_out_b, r_mat)

    sblk = n_prompt // n_sample
    cs_t = jnp.transpose(state_conv, (1, 0, 2))
    z, xs, b_m, c_m, xdt, e_tok, newconv_t = pl.pallas_call(
        functools.partial(_mamba_sample_pre_body, inner=inner, bc=bc),
        grid=(1,),
        in_specs=[
            pl.BlockSpec((n_sample, D), lambda i: (sblk, 0)),
            _const_spec((1, D)),
            _const_spec((D, inner + conv_dim)),
            _const_spec((D, L)),
            _const_spec((SSM_CONV, conv_dim)),
            _const_spec((1, conv_dim)),
            _const_spec((1, L)),
            _const_spec((1, L)),
            _const_spec((L, inner)),
            _const_spec((SSM_CONV - 1, n_sample, conv_dim)),
        ],
        out_specs=[
            pl.BlockSpec((n_sample, inner), lambda i: (0, 0)),
            pl.BlockSpec((n_sample, inner), lambda i: (0, 0)),
            pl.BlockSpec((n_sample, bc), lambda i: (0, 0)),
            pl.BlockSpec((n_sample, bc), lambda i: (0, 0)),
            pl.BlockSpec((n_sample, inner), lambda i: (0, 0)),
            pl.BlockSpec((n_sample, L), lambda i: (0, 0)),
            pl.BlockSpec((SSM_CONV - 1, n_sample, conv_dim), lambda i: (0, 0, 0)),
        ],
        out_shape=[
            jax.ShapeDtypeStruct((n_sample, inner), F32),
            jax.ShapeDtypeStruct((n_sample, inner), F32),
            jax.ShapeDtypeStruct((n_sample, bc), F32),
            jax.ShapeDtypeStruct((n_sample, bc), F32),
            jax.ShapeDtypeStruct((n_sample, inner), F32),
            jax.ShapeDtypeStruct((n_sample, L), F32),
            jax.ShapeDtypeStruct((SSM_CONV - 1, n_sample, conv_dim), F32),
        ],
        compiler_params=_cparams(("arbitrary",)),
        name="mamba_sample_pre",
    )(X, g2, w_main, w_dt, conv_w, cb2, dtb, a_neg, r_mat, cs_t)
    conv_s = jnp.transpose(newconv_t, (1, 0, 2))

    nblk = n_sample // bt

    def to_cols(arr):
        return jnp.transpose(arr.reshape(nblk, bt, inner), (0, 2, 1))

    h0 = state_ssm_all.reshape(state_ssm_all.shape[0], n_sample, inner, SSM_STATE)
    h_new, y_s = pl.pallas_call(
        functools.partial(_mamba_sample_state_body, bt=bt, inner=inner, heads=heads),
        grid_spec=pltpu.PrefetchScalarGridSpec(
            num_scalar_prefetch=1,
            grid=(nblk,),
            in_specs=[
                pl.BlockSpec((None, bt, inner, SSM_STATE), lambda i, e: (layer, i, 0, 0)),
                pl.BlockSpec((1, inner, bt), lambda i, e: (i, 0, 0)),
                pl.BlockSpec((1, bt, bc), lambda i, e: (i, 0, 0)),
                pl.BlockSpec((1, bt, bc), lambda i, e: (i, 0, 0)),
            ],
            out_specs=[
                pl.BlockSpec((bt, inner, SSM_STATE), lambda i, e: (i, 0, 0)),
                pl.BlockSpec((bt, inner), lambda i, e: (i, 0)),
            ],
        ),
        out_shape=[
            jax.ShapeDtypeStruct((n_sample, inner, SSM_STATE), F32),
            jax.ShapeDtypeStruct((n_sample, inner), F32),
        ],
        compiler_params=_cparams(("arbitrary",)),
        name="mamba_sample_state",
    )(e_tok[:, :heads].reshape(n_sample * heads), h0, to_cols(xdt), b_m.reshape(nblk, bt, bc),
      c_m.reshape(nblk, bt, bc))

    X = pl.pallas_call(
        functools.partial(_mamba_sample_post_body, inner=inner),
        grid=(1,),
        in_specs=[
            pl.BlockSpec((n_sample, D), lambda i: (sblk, 0)),
            pl.BlockSpec((n_sample, inner), lambda i: (0, 0)),
            pl.BlockSpec((n_sample, inner), lambda i: (0, 0)),
            pl.BlockSpec((n_sample, inner), lambda i: (0, 0)),
            _const_spec((1, inner)),
            _const_spec((1, inner)),
            _const_spec((inner, D)),
        ],
        out_specs=pl.BlockSpec((n_sample, D), lambda i: (sblk, 0)),
        out_shape=jax.ShapeDtypeStruct((M, D), F32),
        scratch_shapes=[pltpu.VMEM((n_sample, inner), BF16)],
        input_output_aliases={0: 0},
        compiler_params=_cparams(("arbitrary",)),
        name="mamba_sample_post",
    )(X, y_s, xs, z, dsk, ng2, w_out_b)

    ssm_p = ssm_p.reshape(n_batch, heads, SSM_HEAD_DIM, SSM_STATE)
    ssm_s = h_new.reshape(n_sample, heads, SSM_HEAD_DIM, SSM_STATE)
    return X, conv_p, conv_s, ssm_p, ssm_s


def _causal_dwconv_tile(ctx_ref, dw_ref, acc_ref, tm, d):
    S = SUBLANES_V7X
    first = CFM_PAD - (CFM_KERNEL - 1)
    rb, lb = CONV_ROW_BLOCK, CONV_LANE_BLOCK
    for r0 in range(0, tm, rb):
        for l0 in range(0, d, lb):
            lanes = slice(l0, l0 + lb)
            y = None
            for b in range(S):
                pb = None
                for a in range((first + CFM_KERNEL - 1) // S + 1):
                    k = S * a + b - first
                    if 0 <= k < CFM_KERNEL:
                        term = ctx_ref[r0 + S * a:r0 + S * a + rb + S, lanes] * dw_ref[k:k + 1, lanes]
                        pb = term if pb is None else pb + term
                part = pb[b:b + rb, :]
                y = part if y is None else y + part
            acc_ref[r0:r0 + rb, lanes] = y


def _cfm_prompt_body(x_ref, g_ref, w1_ref, b1_ref, dw_ref, dwb_ref, lng_ref, lnb_ref, w2_ref, b2_ref,
                     o_ref, buf_ref, ctx_ref, acc_ref, *, tm, tiles_per_seq, d):
    i = pl.program_id(0)

    @pl.when(i % tiles_per_seq == 0)
    def _():
        ctx_ref[0:CFM_PAD, :] = jnp.zeros((CFM_PAD, d), F32)
        ctx_ref[CFM_PAD + tm:CFM_PAD + tm + SUBLANES_V7X, :] = jnp.zeros((SUBLANES_V7X, d), F32)

    x = x_ref[...]
    h = _rms(x, g_ref[...]).astype(BF16)
    a = _dot(h, w1_ref[...]) + b1_ref[...]
    ctx_ref[CFM_PAD:CFM_PAD + tm, :] = a[:, :d] * _sigmoid(a[:, d:])
    _causal_dwconv_tile(ctx_ref, dw_ref, acc_ref, tm, d)
    buf_ref[0] = ctx_ref[CFM_PAD + tm - (CFM_KERNEL - 1):CFM_PAD + tm, :]
    ctx_ref[0:CFM_PAD, :] = ctx_ref[tm:tm + CFM_PAD, :]
    acc = acc_ref[...] + dwb_ref[...]
    hc = _silu(_layernorm(acc, lng_ref[...], lnb_ref[...])).astype(BF16)
    o_ref[...] = x + _dot(hc, w2_ref[...]) + b2_ref[...]


def _cfm_sample_body(x_ref, g_ref, w1_ref, b1_ref, dw_ref, dwb_ref, lng_ref, lnb_ref, w2_ref, b2_ref, cs_ref,
                     o_ref, new_ref, *, d):
    x = x_ref[...]
    h = _rms(x, g_ref[...]).astype(BF16)
    a = _dot(h, w1_ref[...]) + b1_ref[...]
    glu = a[:, :d] * _sigmoid(a[:, d:])
    acc = dwb_ref[...] + glu * dw_ref[CFM_KERNEL - 1:CFM_KERNEL, :]
    for k in range(CFM_KERNEL - 1):
        acc = acc + cs_ref[k] * dw_ref[k:k + 1, :]
    for k in range(CFM_KERNEL - 2):
        new_ref[k] = cs_ref[k + 1]
    new_ref[CFM_KERNEL - 2] = glu
    hc = _silu(_layernorm(acc, lng_ref[...], lnb_ref[...])).astype(BF16)
    o_ref[...] = x + _dot(hc, w2_ref[...]) + b2_ref[...]


def _conformer(X, n_prompt, n_batch, state_conv, g, w_pw1, b_pw1, dw_w, dw_b, ln_g, ln_b, w_pw2, b_pw2,
               *, tm, bt):
    M, D = X.shape
    n_sample = M - n_prompt
    seq = n_prompt // n_batch
    tiles_per_seq = seq // tm
    K = CFM_KERNEL
    g2 = g.reshape(1, D)
    w1, w2 = w_pw1.astype(BF16), w_pw2.astype(BF16)
    b1, b2 = b_pw1.reshape(1, 2 * D), b_pw2.reshape(1, D)
    dwb, lng, lnb = dw_b.reshape(1, D), ln_g.reshape(1, D), ln_b.reshape(1, D)
    weight_specs = [
        _const_spec((1, D)), _const_spec((D, 2 * D)), _const_spec((1, 2 * D)), _const_spec((K, D)),
        _const_spec((1, D)), _const_spec((1, D)), _const_spec((1, D)), _const_spec((D, D)), _const_spec((1, D)),
    ]
    weights = (g2, w1, b1, dw_w, dwb, lng, lnb, w2, b2)

    X, buf_p = pl.pallas_call(
        functools.partial(_cfm_prompt_body, tm=tm, tiles_per_seq=tiles_per_seq, d=D),
        grid=(n_prompt // tm,),
        in_specs=[pl.BlockSpec((tm, D), lambda i: (i, 0))] + weight_specs,
        out_specs=[
            pl.BlockSpec((tm, D), lambda i: (i, 0)),
            pl.BlockSpec((1, K - 1, D), lambda i: (i // tiles_per_seq, 0, 0)),
        ],
        out_shape=[jax.ShapeDtypeStruct((M, D), F32), jax.ShapeDtypeStruct((n_batch, K - 1, D), F32)],
        scratch_shapes=[pltpu.VMEM((CFM_PAD + tm + SUBLANES_V7X, D), F32), pltpu.VMEM((tm, D), F32)],
        input_output_aliases={0: 0},
        compiler_params=_cparams(("arbitrary",)),
        name="conformer_prompt",
    )(X, *weights)

    first = n_prompt // bt
    cs_t = jnp.transpose(state_conv, (1, 0, 2))
    X, new_t = pl.pallas_call(
        functools.partial(_cfm_sample_body, d=D),
        grid=(n_sample // bt,),
        in_specs=[pl.BlockSpec((bt, D), lambda i: (i + first, 0))] + weight_specs
        + [pl.BlockSpec((K - 1, bt, D), lambda i: (0, i, 0))],
        out_specs=[
            pl.BlockSpec((bt, D), lambda i: (i + first, 0)),
            pl.BlockSpec((K - 1, bt, D), lambda i: (0, i, 0)),
        ],
        out_shape=[jax.ShapeDtypeStruct((M, D), F32), jax.ShapeDtypeStruct((K - 1, n_sample, D), F32)],
        input_output_aliases={0: 0},
        compiler_params=_cparams(("arbitrary",)),
        name="conformer_sample",
    )(X, *weights, cs_t)
    return X, buf_p, jnp.transpose(new_t, (1, 0, 2))


def _stack(parts):
    return parts[0][None] if len(parts) == 1 else jnp.stack(parts)


def kernel(x_prompt, x_sample, state_ssm, state_conv_ssm, state_conv_cfm, norm_mix_g, norm_ffn_g, norm_final_g, a_w_in, a_b_in, a_ln_g, a_ln_b, a_w_s, a_b_s, a_w_out, b_w_in, b_conv_w, b_conv_b, b_dt_bias, b_a_log, b_d, b_norm_g, b_w_out, c_w_pw1, c_b_pw1, c_dw_w, c_dw_b, c_ln_g, c_ln_b, c_w_pw2, c_b_pw2, f_w_gate, f_w_up, f_w_down, e_w_router, e_w_gate, e_w_up, e_w_down):
    n_batch, seq, D = x_prompt.shape
    n_sample = x_sample.shape[0]
    n_prompt = n_batch * seq
    depth = norm_mix_g.shape[0]
    d_ff = f_w_gate.shape[2]
    assert x_sample.shape[1] == 1 and n_prompt % n_sample == 0 and seq % CHUNK == 0

    X = jnp.concatenate([x_prompt.reshape(n_prompt, D), x_sample.reshape(n_sample, D)], axis=0)
    v_p, v_s, ssm_p, ssm_s, cs_p, cs_s, cc_p, cc_s = [], [], [], [], [], [], [], []
    for i in range(depth):
        kind, j = i % 3, i // 3
        if kind == 0:
            X, vp, vs = _gmlp(X, n_prompt, n_batch, norm_mix_g[i], a_w_in[j], a_b_in[j], a_ln_g[j], a_ln_b[j],
                              a_w_s[j], a_b_s[j], a_w_out[j], tm=512)
            v_p.append(vp)
            v_s.append(vs)
        elif kind == 1:
            X, cbp, cbs, hlp, hls = _mamba(X, n_prompt, n_batch, state_ssm, j, state_conv_ssm[j], norm_mix_g[i],
                                           b_w_in[j], b_conv_w[j], b_conv_b[j], b_dt_bias[j], b_a_log[j], b_d[j],
                                           b_norm_g[j], b_w_out[j], tm=256, bt=8)
            cs_p.append(cbp)
            cs_s.append(cbs)
            ssm_p.append(hlp)
            ssm_s.append(hls)
        else:
            X, cbp, cbs = _conformer(X, n_prompt, n_batch, state_conv_cfm[j], norm_mix_g[i], c_w_pw1[j], c_b_pw1[j],
                                     c_dw_w[j], c_dw_b[j], c_ln_g[j], c_ln_b[j], c_w_pw2[j], c_b_pw2[j],
                                     tm=512, bt=32)
            cc_p.append(cbp)
            cc_s.append(cbs)
        k = i // 2
        if i % 2 == 0:
            X = _ffn_dense(X, n_prompt, norm_ffn_g[i], f_w_gate, f_w_up, f_w_down, k,
                           tm=512, f_chunk=FFN_CHUNK)
        else:
            X = _moe(X, n_prompt, norm_ffn_g[i], e_w_router[k], e_w_gate, e_w_up, e_w_down, k, norm_final_g,
                     tm_tok=384, tm_out=512, tm=512, f_chunk=FFN_CHUNK, final_norm=(i == depth - 1))
    assert depth % 2 == 0
    y_prompt, y_sample = X
    y_prompt = y_prompt.reshape(n_batch, seq, D)
    y_sample = y_sample.reshape(n_sample, 1, D)
    return (y_prompt, y_sample, _stack(v_p), _stack(v_s), _stack(ssm_p), _stack(ssm_s),
            _stack(cs_p), _stack(cs_s), _stack(cc_p), _stack(cc_s))
```

```python
import functools

import jax
import jax.numpy as jnp
from jax import lax
from jax.experimental import pallas as pl
from jax.experimental.pallas import tpu as pltpu

F32 = jnp.float32
BF16 = jnp.bfloat16
EPS = 1e-6
HIGHEST = lax.Precision.HIGHEST

LANES_V7X = 128
SUBLANES_V7X = 8
VMEM_LIMIT_V7X = 60 * 1024 * 1024

CHUNK = 128
GM_GROUPS = 8
SSM_GROUPS = 8
SSM_HEAD_DIM = 64
SSM_STATE = 128
SSM_CONV = 4
CFM_KERNEL = 31
N_EXPERTS = 8
CFM_PAD = 32
SSM_PAD = 8
FFN_CHUNK = 256
CONV_ROW_BLOCK = 128
CONV_LANE_BLOCK = 256


def _cparams(sem):
    return pltpu.CompilerParams(dimension_semantics=sem, vmem_limit_bytes=VMEM_LIMIT_V7X)


def _const_spec(shape):
    nd = len(shape)
    return pl.BlockSpec(shape, lambda *_: (0,) * nd, pipeline_mode=pl.Buffered(1))


def _dot(a, b):
    return jnp.dot(a, b, preferred_element_type=F32)


def _dot_nt(a, b):
    return lax.dot_general(a, b, (((1,), (1,)), ((), ())), preferred_element_type=F32)


def _dot_tn(a, b):
    return lax.dot_general(a, b, (((0,), (0,)), ((), ())), preferred_element_type=F32)


def _dot_hi(a, b):
    return jnp.dot(a, b, precision=HIGHEST, preferred_element_type=F32)


def _split_bf16(a):
    hi = a.astype(BF16)
    return hi, (a - hi.astype(F32)).astype(BF16)


def _dot_split_lhs(a, b_bf16):
    hi, lo = _split_bf16(a)
    return _dot(hi, b_bf16) + _dot(lo, b_bf16)


def _rms(x, g):
    return x * lax.rsqrt(jnp.mean(x * x, axis=-1, keepdims=True) + EPS) * g


def _layernorm(x, g, b):
    xc = x - jnp.mean(x, axis=-1, keepdims=True)
    return xc * lax.rsqrt(jnp.mean(xc * xc, axis=-1, keepdims=True) + EPS) * g + b


def _sigmoid(x):
    return 1.0 / (1.0 + jnp.exp(-x))


def _silu(x):
    return x * _sigmoid(x)


def _gelu(x):
    return 0.5 * x * (1.0 + lax.erf(x * (2.0 ** -0.5)))


def _softplus(x):
    return jnp.maximum(x, 0.0) + jnp.log(1.0 + jnp.exp(-jnp.abs(x)))


def _gmlp_prompt_body(x_ref, g_ref, win_ref, bin_ref, lng_ref, lnb_ref, ws_ref, bs_ref, wout_ref,
                      o_ref, v_ref, gated_ref, *, tm, width):
    x = x_ref[...]
    h = _rms(x, g_ref[...]).astype(BF16)
    u = _gelu(_dot(h, win_ref[:, :width]) + bin_ref[:, :width])
    hv = _gelu(_dot(h, win_ref[:, width:]) + bin_ref[:, width:])
    v = _layernorm(hv, lng_ref[...], lnb_ref[...])
    v_ref[0] = v[tm - CHUNK:, :]
    vb = v.astype(BF16)
    gdim = width // GM_GROUPS
    causal = (lax.broadcasted_iota(jnp.int32, (CHUNK, CHUNK), 0)
              >= lax.broadcasted_iota(jnp.int32, (CHUNK, CHUNK), 1))
    for g in range(GM_GROUPS):
        wc = jnp.where(causal, ws_ref[g], 0.0).astype(BF16)
        bias = bs_ref[:, g:g + 1]
        for c in range(tm // CHUNK):
            rows = slice(c * CHUNK, (c + 1) * CHUNK)
            cols = slice(g * gdim, (g + 1) * gdim)
            s = _dot(wc, vb[rows, cols]) + bias
            gated_ref[rows, cols] = (u[rows, cols] * s).astype(BF16)
    o_ref[...] = x + _dot(gated_ref[...], wout_ref[...])


def _gmlp_sample_body(x_ref, g_ref, win_ref, bin_ref, lng_ref, lnb_ref, ws0_ref, bs0_ref, wout_ref,
                      o_ref, v_ref, *, width):
    x = x_ref[...]
    h = _rms(x, g_ref[...]).astype(BF16)
    u = _gelu(_dot(h, win_ref[:, :width]) + bin_ref[:, :width])
    hv = _gelu(_dot(h, win_ref[:, width:]) + bin_ref[:, width:])
    v = _layernorm(hv, lng_ref[...], lnb_ref[...])
    v_ref[...] = v
    s = v * ws0_ref[...] + bs0_ref[...]
    o_ref[...] = x + _dot((u * s).astype(BF16), wout_ref[...])


def _gmlp(X, n_prompt, n_batch, g, w_in, b_in, ln_g, ln_b, w_s, b_s, w_out, *, tm):
    M, D = X.shape
    width = w_out.shape[0]
    seq = n_prompt // n_batch
    tiles_per_seq = seq // tm
    n_sample = M - n_prompt
    g2 = g.reshape(1, D)
    b_in2 = b_in.reshape(1, 2 * width)
    ln_g2, ln_b2 = ln_g.reshape(1, width), ln_b.reshape(1, width)
    w_in_b, w_out_b = w_in.astype(BF16), w_out.astype(BF16)

    X, v_p = pl.pallas_call(
        functools.partial(_gmlp_prompt_body, tm=tm, width=width),
        grid=(n_prompt // tm,),
        in_specs=[
            pl.BlockSpec((tm, D), lambda i: (i, 0)),
            _const_spec((1, D)),
            _const_spec((D, 2 * width)),
            _const_spec((1, 2 * width)),
            _const_spec((1, width)),
            _const_spec((1, width)),
            _const_spec((GM_GROUPS, CHUNK, CHUNK)),
            _const_spec((CHUNK, GM_GROUPS)),
            _const_spec((width, D)),
        ],
        out_specs=[
            pl.BlockSpec((tm, D), lambda i: (i, 0)),
            pl.BlockSpec((1, CHUNK, width), lambda i: (i // tiles_per_seq, 0, 0)),
        ],
        out_shape=[jax.ShapeDtypeStruct((M, D), F32),
                   jax.ShapeDtypeStruct((n_batch, CHUNK, width), F32)],
        scratch_shapes=[pltpu.VMEM((tm, width), BF16)],
        input_output_aliases={0: 0},
        compiler_params=_cparams(("arbitrary",)),
        name="gmlp_prompt",
    )(X, g2, w_in_b, b_in2, ln_g2, ln_b2, w_s, b_s.T, w_out_b)

    gdim = width // GM_GROUPS
    ws0 = jnp.repeat(w_s[:, 0, 0], gdim).reshape(1, width)
    bs0 = jnp.repeat(b_s[:, 0], gdim).reshape(1, width)
    sblk = n_prompt // n_sample
    X, v_s = pl.pallas_call(
        functools.partial(_gmlp_sample_body, width=width),
        grid=(1,),
        in_specs=[
            pl.BlockSpec((n_sample, D), lambda i: (sblk, 0)),
            _const_spec((1, D)),
            _const_spec((D, 2 * width)),
            _const_spec((1, 2 * width)),
            _const_spec((1, width)),
            _const_spec((1, width)),
            _const_spec((1, width)),
            _const_spec((1, width)),
            _const_spec((width, D)),
        ],
        out_specs=[
            pl.BlockSpec((n_sample, D), lambda i: (sblk, 0)),
            pl.BlockSpec((n_sample, width), lambda i: (0, 0)),
        ],
        out_shape=[jax.ShapeDtypeStruct((M, D), F32),
                   jax.ShapeDtypeStruct((n_sample, width), F32)],
        input_output_aliases={0: 0},
        compiler_params=_cparams(("arbitrary",)),
        name="gmlp_sample",
    )(X, g2, w_in_b, b_in2, ln_g2, ln_b2, ws0, bs0, w_out_b)
    return X, v_p, v_s.reshape(n_sample, 1, width)


def _swiglu_tile(h, wg_ref, wu_ref, wd_ref, f_chunk):
    d_ff = wd_ref.shape[0]
    acc = None
    for f in range(d_ff // f_chunk):
        cols = slice(f * f_chunk, (f + 1) * f_chunk)
        a = _dot(h, wg_ref[:, cols].astype(BF16))
        b = _dot(h, wu_ref[:, cols].astype(BF16))
        part = _dot((_silu(a) * b).astype(BF16), wd_ref[cols, :].astype(BF16))
        acc = part if acc is None else acc + part
    return acc


def _ffn_dense_body(x_ref, g_ref, wg_ref, wu_ref, wd_ref, o_ref, *, f_chunk):
    x = x_ref[...]
    h = _rms(x, g_ref[...]).astype(BF16)
    o_ref[...] = x + _swiglu_tile(h, wg_ref, wu_ref, wd_ref, f_chunk)


def _ffn_dense_rows(X, g2, wg, wu, wd, layer, *, tm, first_block, n_tiles, f_chunk):
    M, D = X.shape
    d_ff = wd.shape[1]

    def layer_spec(rows, cols):
        return pl.BlockSpec((None, rows, cols), lambda i: (layer, 0, 0), pipeline_mode=pl.Buffered(1))

    return pl.pallas_call(
        functools.partial(_ffn_dense_body, f_chunk=f_chunk),
        grid=(n_tiles,),
        in_specs=[
            pl.BlockSpec((tm, D), lambda i: (i + first_block, 0)),
            _const_spec((1, D)),
            layer_spec(D, d_ff),
            layer_spec(D, d_ff),
            layer_spec(d_ff, D),
        ],
        out_specs=pl.BlockSpec((tm, D), lambda i: (i + first_block, 0)),
        out_shape=jax.ShapeDtypeStruct((M, D), F32),
        input_output_aliases={0: 0},
        compiler_params=_cparams(("arbitrary",)),
        name="ffn_dense",
    )(X, g2, wg, wu, wd)


def _ffn_dense(X, n_prompt, g, w_gate, w_up, w_down, layer, *, tm, f_chunk):
    M, D = X.shape
    n_sample = M - n_prompt
    g2 = g.reshape(1, D)
    X = _ffn_dense_rows(X, g2, w_gate, w_up, w_down, layer, tm=tm, first_block=0, n_tiles=n_prompt // tm,
                        f_chunk=f_chunk)
    X = _ffn_dense_rows(X, g2, w_gate, w_up, w_down, layer, tm=n_sample, first_block=n_prompt // n_sample,
                        n_tiles=1, f_chunk=f_chunk)
    return X


def _router_body(x_ref, g_ref, wr_ref, idx_ref, wts_ref):
    L = LANES_V7X
    h_hi, h_lo = _split_bf16(_rms(x_ref[...], g_ref[...]))
    hi_part = _dot(h_hi, wr_ref[...])
    logits = hi_part[:, :L] + (hi_part[:, L:] + _dot(h_lo, wr_ref[:, :L]))
    lane = lax.broadcasted_iota(jnp.int32, logits.shape, 1)
    neg = jnp.float32(-jnp.inf)
    logits = jnp.where(lane < N_EXPERTS, logits, neg)
    m1 = jnp.max(logits, axis=-1, keepdims=True)
    i1 = jnp.min(jnp.where(logits == m1, lane, LANES_V7X), axis=-1, keepdims=True)
    rest = jnp.where(lane == i1, neg, logits)
    m2 = jnp.max(rest, axis=-1, keepdims=True)
    i2 = jnp.min(jnp.where(rest == m2, lane, LANES_V7X), axis=-1, keepdims=True)
    e = jnp.exp(m2 - m1)
    w1 = 1.0 / (1.0 + e)
    w2 = e / (1.0 + e)
    idx_ref[...] = jnp.where(lane == 0, i1, jnp.where(lane == 1, i2, 0))
    wts_ref[...] = jnp.where(lane == 0, w1, jnp.where(lane == 1, w2, 0.0))


def _slot_table_body(pos1_ref, pos2_ref, pad_lo_ref, pad_hi_ref, tok_ref, *, n_tokens, n_ranges):
    def pad(s, carry):
        tok_ref[s] = 0
        return carry

    for r in range(n_ranges):
        lax.fori_loop(pad_lo_ref[r], pad_hi_ref[r], pad, 0)

    def place(t, carry):
        tok_ref[pos1_ref[t]] = t
        tok_ref[pos2_ref[t]] = t
        return carry

    lax.fori_loop(0, n_tokens, place, 0, unroll=8)


def _issue_row_gather(src_hbm, idx_ref, base, buf, sem, n_rows):
    def body(r, carry):
        pltpu.make_async_copy(src_hbm.at[pl.ds(idx_ref[base + r], 1), :], buf.at[pl.ds(r, 1), :], sem).start()
        return carry
    lax.fori_loop(0, n_rows, body, 0, unroll=8)


def _wait_row_gather(src_hbm, buf, sem, n_rows):
    pltpu.make_async_copy(src_hbm.at[pl.ds(0, n_rows), :], buf, sem).wait()


def _moe_ffn_body(tile_expert_ref, n_tiles_ref, tok_ref, x_hbm, g_ref, wg_ref, wu_ref, wd_ref,
                  y_ref, buf_ref, sem_ref, *, tm, f_chunk):
    t = pl.program_id(0)
    n_tiles = n_tiles_ref[0]
    slot = t % 2

    @pl.when(t == 0)
    def _():
        _issue_row_gather(x_hbm, tok_ref, 0, buf_ref.at[0], sem_ref.at[0], tm)

    @pl.when(t + 1 < n_tiles)
    def _():
        _issue_row_gather(x_hbm, tok_ref, (t + 1) * tm, buf_ref.at[1 - slot], sem_ref.at[1 - slot], tm)

    @pl.when(t < n_tiles)
    def _():
        _wait_row_gather(x_hbm, buf_ref.at[slot], sem_ref.at[slot], tm)
        h = _rms(buf_ref[slot], g_ref[...]).astype(BF16)
        y_ref[...] = _swiglu_tile(h, wg_ref, wu_ref, wd_ref, f_chunk)

    @pl.when(t >= n_tiles)
    def _():
        y_ref[...] = jnp.zeros_like(y_ref)


def _moe_combine_body(p1_ref, p2_ref, x_ref, wts_ref, y_hbm, gf_ref, o_ref, buf1_ref, buf2_ref, sem_ref,
                      *, tm, row0, n_steps, final_norm):
    i = pl.program_id(0)
    slot = i % 2

    def issue(step, s):
        _issue_row_gather(y_hbm, p1_ref, row0 + step * tm, buf1_ref.at[s], sem_ref.at[0, s], tm)
        _issue_row_gather(y_hbm, p2_ref, row0 + step * tm, buf2_ref.at[s], sem_ref.at[1, s], tm)

    @pl.when(i == 0)
    def _():
        issue(0, 0)

    @pl.when(i + 1 < n_steps)
    def _():
        issue(i + 1, 1 - slot)

    _wait_row_gather(y_hbm, buf1_ref.at[slot], sem_ref.at[0, slot], tm)
    _wait_row_gather(y_hbm, buf2_ref.at[slot], sem_ref.at[1, slot], tm)
    out = x_ref[...] + (wts_ref[:, 0:1] * buf1_ref[slot] + wts_ref[:, 1:2] * buf2_ref[slot])
    if final_norm:
        out = _rms(out, gf_ref[...])
    o_ref[...] = out


def _moe(X, n_prompt, g, w_router, w_gate, w_up, w_down, layer, g_final, *, tm_tok, tm_out, tm, f_chunk,
         final_norm):
    M, D = X.shape
    E = N_EXPERTS
    d_ff = w_down.shape[2]
    g2 = g.reshape(1, D)
    L = LANES_V7X
    wr = jnp.zeros((D, L), F32).at[:, :E].set(w_router)
    wr_hi = wr.astype(BF16)
    wr_split = jnp.concatenate([wr_hi, (wr - wr_hi.astype(F32)).astype(BF16)], axis=1)

    idx, wts = pl.pallas_call(
        _router_body,
        grid=(M // tm_tok,),
        in_specs=[pl.BlockSpec((tm_tok, D), lambda i: (i, 0)), _const_spec((1, D)), _const_spec((D, 2 * L))],
        out_specs=[pl.BlockSpec((tm_tok, L), lambda i: (i, 0)), pl.BlockSpec((tm_tok, L), lambda i: (i, 0))],
        out_shape=[jax.ShapeDtypeStruct((M, L), jnp.int32), jax.ShapeDtypeStruct((M, L), F32)],
        compiler_params=_cparams(("arbitrary",)),
        name="moe_router",
    )(X, g2, wr_split)

    i1, i2 = idx[:, 0], idx[:, 1]
    eids = jnp.arange(E, dtype=jnp.int32)
    sel = ((i1[:, None] == eids) | (i2[:, None] == eids)).astype(jnp.int32)
    count = jnp.sum(sel, axis=0)
    tiles_e = (count + tm - 1) // tm
    tiles_end = jnp.cumsum(tiles_e)
    n_tiles = tiles_end[-1]
    start_e = (tiles_end - tiles_e) * tm
    pos = start_e[None, :] + jnp.cumsum(sel, axis=0) - sel
    pos1 = jnp.sum(jnp.where(i1[:, None] == eids, pos, 0), axis=1).astype(jnp.int32)
    pos2 = jnp.sum(jnp.where(i2[:, None] == eids, pos, 0), axis=1).astype(jnp.int32)
    max_tiles = (2 * M + E * (tm - 1)) // tm
    n_slots = max_tiles * tm
    pad_lo = jnp.concatenate([start_e + count, (n_tiles * tm)[None]]).astype(jnp.int32)
    pad_hi = jnp.concatenate([tiles_end * tm, jnp.array([n_slots], jnp.int32)]).astype(jnp.int32)
    tile_ids = jnp.arange(max_tiles, dtype=jnp.int32)
    tile_expert = jnp.sum((jnp.minimum(tile_ids, n_tiles - 1)[:, None] >= tiles_end[None, :]).astype(jnp.int32),
                          axis=1)
    tile_expert = jnp.minimum(tile_expert, E - 1).astype(jnp.int32)
    n_tiles_arr = n_tiles.astype(jnp.int32).reshape(1)

    smem = pl.BlockSpec(memory_space=pltpu.SMEM)
    tok_of_slot = pl.pallas_call(
        functools.partial(_slot_table_body, n_tokens=M, n_ranges=E + 1),
        in_specs=[smem, smem, smem, smem],
        out_specs=smem,
        out_shape=jax.ShapeDtypeStruct((n_slots,), jnp.int32),
        name="moe_slot_table",
    )(pos1, pos2, pad_lo, pad_hi)

    def expert_spec(rows, cols):
        return pl.BlockSpec((None, None, rows, cols), lambda t, te, nt, tok: (layer, te[t], 0, 0),
                            pipeline_mode=pl.Buffered(1))

    y_sorted = pl.pallas_call(
        functools.partial(_moe_ffn_body, tm=tm, f_chunk=f_chunk),
        grid_spec=pltpu.PrefetchScalarGridSpec(
            num_scalar_prefetch=3,
            grid=(max_tiles,),
            in_specs=[
                pl.BlockSpec(memory_space=pl.ANY),
                pl.BlockSpec((1, D), lambda t, te, nt, tok: (0, 0)),
                expert_spec(D, d_ff),
                expert_spec(D, d_ff),
                expert_spec(d_ff, D),
            ],
            out_specs=pl.BlockSpec((tm, D), lambda t, te, nt, tok: (t, 0)),
            scratch_shapes=[pltpu.VMEM((2, tm, D), F32), pltpu.SemaphoreType.DMA((2,))],
        ),
        out_shape=jax.ShapeDtypeStruct((n_slots, D), F32),
        compiler_params=_cparams(("arbitrary",)),
        name="moe_ffn",
    )(tile_expert, n_tiles_arr, tok_of_slot, X, g2, w_gate, w_up, w_down)

    def combine_rows(tm_c, first_block, n_steps, in_place):
        out_first = first_block if in_place else 0
        return pl.pallas_call(
            functools.partial(_moe_combine_body, tm=tm_c, row0=first_block * tm_c, n_steps=n_steps,
                              final_norm=final_norm),
            grid_spec=pltpu.PrefetchScalarGridSpec(
                num_scalar_prefetch=2,
                grid=(n_steps,),
                in_specs=[
                    pl.BlockSpec((tm_c, D), lambda i, p1, p2: (i + first_block, 0)),
                    pl.BlockSpec((tm_c, L), lambda i, p1, p2: (i + first_block, 0)),
                    pl.BlockSpec(memory_space=pl.ANY),
                    pl.BlockSpec((1, D), lambda i, p1, p2: (0, 0)),
                ],
                out_specs=pl.BlockSpec((tm_c, D), lambda i, p1, p2: (i + out_first, 0)),
                scratch_shapes=[pltpu.VMEM((2, tm_c, D), F32), pltpu.VMEM((2, tm_c, D), F32),
                                pltpu.SemaphoreType.DMA((2, 2))],
            ),
            out_shape=jax.ShapeDtypeStruct((M if in_place else n_steps * tm_c, D), F32),
            input_output_aliases={2: 0} if in_place else {},
            compiler_params=_cparams(("arbitrary",)),
            name="moe_combine",
        )(pos1, pos2, X, wts, y_sorted, g_final.reshape(1, D))

    if not final_norm:
        return combine_rows(tm_tok, 0, M // tm_tok, True)
    n_sample = M - n_prompt
    y_p = combine_rows(tm_out, 0, n_prompt // tm_out, False)
    y_s = combine_rows(n_sample, n_prompt // n_sample, 1, False)
    return y_p, y_s


def _ssd_post(y, xs, z, dsk, ng, wout_ref, ynorm_ref, inner):
    y = (y + dsk * xs) * _silu(z)
    gw = inner // SSM_GROUPS
    for g in range(SSM_GROUPS):
        cols = slice(g * gw, (g + 1) * gw)
        yg = y[:, cols]
        yg = yg * lax.rsqrt(jnp.mean(yg * yg, axis=-1, keepdims=True) + EPS) * ng[:, cols]
        ynorm_ref[:, cols] = yg.astype(BF16)
    return _dot(ynorm_ref[...], wout_ref[...])


def _mamba_prompt_body(x_ref, g_ref, win_ref, wdt_ref, cw_ref, cb_ref, dtb_ref, a_ref, dsk_ref, ng_ref,
                       wout_ref, r_ref, o_ref, conv_ref, ssm_ref, ctx_ref, s_ref, y_ref, ynorm_ref,
                       *, tm, tiles_per_seq, inner, bc):
    i = pl.program_id(0)

    @pl.when(i % tiles_per_seq == 0)
    def _():
        ctx_ref[0:SSM_PAD, :] = jnp.zeros((SSM_PAD, ctx_ref.shape[1]), F32)
        s_ref[...] = jnp.zeros_like(s_ref)

    x = x_ref[...]
    h = _rms(x, g_ref[...]).astype(BF16)
    proj = _dot(h, win_ref[...])
    z = proj[:, :inner]
    xbc = proj[:, inner:]
    ctx_ref[SSM_PAD:SSM_PAD + tm, :] = xbc
    conv_ref[0] = ctx_ref[SSM_PAD + tm - (SSM_CONV - 1):SSM_PAD + tm, :]
    conv = cb_ref[...] + xbc * cw_ref[SSM_CONV - 1:SSM_CONV, :]
    for k in range(SSM_CONV - 1):
        off = SSM_PAD - (SSM_CONV - 1) + k
        conv = conv + ctx_ref[off:off + tm, :] * cw_ref[k:k + 1, :]
    ctx_ref[0:SSM_PAD, :] = ctx_ref[tm:tm + SSM_PAD, :]
    act = _silu(conv)
    xs = act[:, :inner]
    bm = act[:, inner:inner + bc].astype(BF16)
    cm = act[:, inner + bc:].astype(BF16)
    dt = _softplus(_dot(h, wdt_ref[...]) + dtb_ref[...])
    da = dt * a_ref[...]

    rows_i = lax.broadcasted_iota(jnp.int32, (CHUNK, CHUNK), 0)
    cols_i = lax.broadcasted_iota(jnp.int32, (CHUNK, CHUNK), 1)
    causal = rows_i >= cols_i
    tril = causal.astype(F32)
    gw = inner // SSM_GROUPS
    hpg = gw // SSM_HEAD_DIM
    lane_head = lax.broadcasted_iota(jnp.int32, (CHUNK, gw), 1) // SSM_HEAD_DIM
    r_mat = r_ref[...]

    for c in range(tm // CHUNK):
        rows = slice(c * CHUNK, (c + 1) * CHUNK)
        dt_c = dt[rows, :]
        cum = _dot_hi(tril, da[rows, :])
        cum_t = cum.T
        dt_t = dt_c.T
        ecum = jnp.exp(cum)
        e_full = _dot_split_lhs(ecum, r_mat)
        w_in = jnp.exp(cum[CHUNK - 1:CHUNK, :] - cum) * dt_c
        xs_c = xs[rows, :]
        xw = (xs_c * _dot_split_lhs(w_in, r_mat)).astype(BF16)
        xs_b = xs_c.astype(BF16)
        for g in range(SSM_GROUPS):
            gcols = slice(g * gw, (g + 1) * gw)
            ncols = slice(g * SSM_STATE, (g + 1) * SSM_STATE)
            b_g = bm[rows, ncols]
            c_g = cm[rows, ncols]
            cb = _dot_nt(c_g, b_g)
            x_g = xs_b[:, gcols]
            s_g = s_ref[g]
            wts, x_heads = [], []
            for r in range(hpg):
                hd = g * hpg + r
                seg = cum[:, hd:hd + 1] - cum_t[hd:hd + 1, :]
                decay = jnp.where(causal, jnp.exp(jnp.where(causal, seg, 0.0)), 0.0)
                wts.append((cb * decay * dt_t[hd:hd + 1, :]).astype(BF16))
                x_heads.append(jnp.where(lane_head == r, x_g, jnp.zeros_like(x_g)))
            y_intra = _dot(jnp.concatenate(wts, axis=1), jnp.concatenate(x_heads, axis=0))
            y_ref[rows, gcols] = y_intra + _dot_nt(c_g, s_g.astype(BF16)) * e_full[:, gcols]
            upd = _dot_tn(xw[:, gcols], b_g)
            for r in range(hpg):
                hd = g * hpg + r
                hrows = slice(r * SSM_HEAD_DIM, (r + 1) * SSM_HEAD_DIM)
                s_ref[g, hrows, :] = s_g[hrows, :] * ecum[CHUNK - 1:CHUNK, hd:hd + 1] + upd[hrows, :]

    out = _ssd_post(y_ref[...], xs, z, dsk_ref[...], ng_ref[...], wout_ref, ynorm_ref, inner)
    o_ref[...] = x + out
    ssm_ref[0] = s_ref[...]


def _mamba_sample_pre_body(x_ref, g_ref, win_ref, wdt_ref, cw_ref, cb_ref, dtb_ref, a_ref, r_ref, cs_ref,
                           z_ref, xs_ref, b_ref, c_ref, xdt_ref, e_ref, newconv_ref, *, inner, bc):
    h = _rms(x_ref[...], g_ref[...]).astype(BF16)
    proj = _dot(h, win_ref[...])
    z_ref[...] = proj[:, :inner]
    xbc = proj[:, inner:]
    conv = cb_ref[...] + xbc * cw_ref[SSM_CONV - 1:SSM_CONV, :]
    for k in range(SSM_CONV - 1):
        conv = conv + cs_ref[k] * cw_ref[k:k + 1, :]
    for k in range(SSM_CONV - 2):
        newconv_ref[k] = cs_ref[k + 1]
    newconv_ref[SSM_CONV - 2] = xbc
    act = _silu(conv)
    xs = act[:, :inner]
    xs_ref[...] = xs
    b_ref[...] = act[:, inner:inner + bc]
    c_ref[...] = act[:, inner + bc:]
    dt = _softplus(_dot(h, wdt_ref[...]) + dtb_ref[...])
    xdt_ref[...] = xs * _dot_split_lhs(dt, r_ref[...])
    e_ref[...] = jnp.exp(dt * a_ref[...])


def _mamba_sample_state_body(e_ref, h0_ref, xdt_t_ref, b_ref, c_ref, hn_ref, y_ref, *, bt, inner, heads):
    i = pl.program_id(0)
    gw = inner // SSM_GROUPS
    hpg = gw // SSM_HEAD_DIM
    row_id = lax.broadcasted_iota(jnp.int32, (bt, bt * SSM_STATE), 0)
    lane_tok = lax.broadcasted_iota(jnp.int32, (bt, bt * SSM_STATE), 1) // SSM_STATE
    for g in range(SSM_GROUPS):
        rows = slice(g * gw, (g + 1) * gw)
        ncols = slice(g * SSM_STATE, (g + 1) * SSM_STATE)
        states = []
        for j in range(bt):
            b_row = b_ref[0, j:j + 1, ncols]
            for r in range(hpg):
                hrows = slice(g * gw + r * SSM_HEAD_DIM, g * gw + (r + 1) * SSM_HEAD_DIM)
                e = e_ref[(i * bt + j) * heads + g * hpg + r]
                hn_ref[j, hrows, :] = h0_ref[j, hrows, :] * e + xdt_t_ref[0, hrows, j:j + 1] * b_row
            states.append(hn_ref[j, rows, :].astype(BF16))
        c_g = c_ref[0, :, ncols]
        c_diag = jnp.where(row_id == lane_tok, jnp.concatenate([c_g] * bt, axis=1), 0.0).astype(BF16)
        y_ref[:, rows] = _dot_nt(c_diag, jnp.concatenate(states, axis=1))


def _mamba_sample_post_body(x_ref, y_ref, xs_ref, z_ref, dsk_ref, ng_ref, wout_ref, o_ref, ynorm_ref, *, inner):
    out = _ssd_post(y_ref[...], xs_ref[...], z_ref[...], dsk_ref[...], ng_ref[...], wout_ref, ynorm_ref, inner)
    o_ref[...] = x_ref[...] + out


def _mamba(X, n_prompt, n_batch, state_ssm_all, layer, state_conv, g, w_in, conv_w, conv_b, dt_bias, a_log, d_skip,
           norm_g, w_out, *, tm, bt):
    M, D = X.shape
    inner = w_out.shape[0]
    heads = a_log.shape[0]
    conv_dim = conv_w.shape[1]
    bc = (conv_dim - inner) // 2
    n_sample = M - n_prompt
    seq = n_prompt // n_batch
    tiles_per_seq = seq // tm
    gw = inner // SSM_GROUPS
    L = LANES_V7X

    g2 = g.reshape(1, D)
    w_main = w_in[:, :inner + conv_dim].astype(BF16)
    w_dt = jnp.zeros((D, L), F32).at[:, :heads].set(w_in[:, inner + conv_dim:]).astype(BF16)
    cb2 = conv_b.reshape(1, conv_dim)
    dtb = jnp.zeros((1, L), F32).at[0, :heads].set(dt_bias)
    a_neg = jnp.zeros((1, L), F32).at[0, :heads].set(-jnp.exp(a_log))
    dsk = jnp.repeat(d_skip, SSM_HEAD_DIM).reshape(1, inner)
    ng2 = norm_g.reshape(1, inner)
    w_out_b = w_out.astype(BF16)
    r_mat = (jnp.arange(L, dtype=jnp.int32)[:, None]
             == (jnp.arange(inner, dtype=jnp.int32) // SSM_HEAD_DIM)[None, :]).astype(BF16)

    X, conv_p, ssm_p = pl.pallas_call(
        functools.partial(_mamba_prompt_body, tm=tm, tiles_per_seq=tiles_per_seq, inner=inner, bc=bc),
        grid=(n_prompt // tm,),
        in_specs=[
            pl.BlockSpec((tm, D), lambda i: (i, 0)),
            _const_spec((1, D)),
            _const_spec((D, inner + conv_dim)),
            _const_spec((D, L)),
            _const_spec((SSM_CONV, conv_dim)),
            _const_spec((1, conv_dim)),
            _const_spec((1, L)),
            _const_spec((1, L)),
            _const_spec((1, inner)),
            _const_spec((1, inner)),
            _const_spec((inner, D)),
            _const_spec((L, inner)),
        ],
        out_specs=[
            pl.BlockSpec((tm, D), lambda i: (i, 0)),
            pl.BlockSpec((1, SSM_CONV - 1, conv_dim), lambda i: (i // tiles_per_seq, 0, 0)),
            pl.BlockSpec((1, SSM_GROUPS, gw, SSM_STATE), lambda i: (i // tiles_per_seq, 0, 0, 0)),
        ],
        out_shape=[
            jax.ShapeDtypeStruct((M, D), F32),
            jax.ShapeDtypeStruct((n_batch, SSM_CONV - 1, conv_dim), F32),
            jax.ShapeDtypeStruct((n_batch, SSM_GROUPS, gw, SSM_STATE), F32),
        ],
        scratch_shapes=[
            pltpu.VMEM((SSM_PAD + tm, conv_dim), F32),
            pltpu.VMEM((SSM_GROUPS, gw, SSM_STATE), F32),
            pltpu.VMEM((tm, inner), F32),
            pltpu.VMEM((tm, inner), BF16),
        ],
        input_output_aliases={0: 0},
        compiler_params=_cparams(("arbitrary",)),
        name="mamba_prompt",
    )(X, g2, w_main, w_dt, conv_w, cb2, dtb, a_neg, dsk, ng2, w_out_b, r_mat)

    sblk = n_prompt // n_sample
    cs_t = jnp.transpose(state_conv, (1, 0, 2))
    z, xs, b_m, c_m, xdt, e_tok, newconv_t = pl.pallas_call(
        functools.partial(_mamba_sample_pre_body, inner=inner, bc=bc),
        grid=(1,),
        in_specs=[
            pl.BlockSpec((n_sample, D), lambda i: (sblk, 0)),
            _const_spec((1, D)),
            _const_spec((D, inner + conv_dim)),
            _const_spec((D, L)),
            _const_spec((SSM_CONV, conv_dim)),
            _const_spec((1, conv_dim)),
            _const_spec((1, L)),
            _const_spec((1, L)),
            _const_spec((L, inner)),
            _const_spec((SSM_CONV - 1, n_sample, conv_dim)),
        ],
        out_specs=[
            pl.BlockSpec((n_sample, inner), lambda i: (0, 0)),
            pl.BlockSpec((n_sample, inner), lambda i: (0, 0)),
            pl.BlockSpec((n_sample, bc), lambda i: (0, 0)),
            pl.BlockSpec((n_sample, bc), lambda i: (0, 0)),
            pl.BlockSpec((n_sample, inner), lambda i: (0, 0)),
            pl.BlockSpec((n_sample, L), lambda i: (0, 0)),
            pl.BlockSpec((SSM_CONV - 1, n_sample, conv_dim), lambda i: (0, 0, 0)),
        ],
        out_shape=[
            jax.ShapeDtypeStruct((n_sample, inner), F32),
            jax.ShapeDtypeStruct((n_sample, inner), F32),
            jax.ShapeDtypeStruct((n_sample, bc), F32),
            jax.ShapeDtypeStruct((n_sample, bc), F32),
            jax.ShapeDtypeStruct((n_sample, inner), F32),
            jax.ShapeDtypeStruct((n_sample, L), F32),
            jax.ShapeDtypeStruct((SSM_CONV - 1, n_sample, conv_dim), F32),
        ],
        compiler_params=_cparams(("arbitrary",)),
        name="mamba_sample_pre",
    )(X, g2, w_main, w_dt, conv_w, cb2, dtb, a_neg, r_mat, cs_t)
    conv_s = jnp.transpose(newconv_t, (1, 0, 2))

    nblk = n_sample // bt

    def to_cols(arr):
        return jnp.transpose(arr.reshape(nblk, bt, inner), (0, 2, 1))

    h0 = state_ssm_all.reshape(state_ssm_all.shape[0], n_sample, inner, SSM_STATE)
    h_new, y_s = pl.pallas_call(
        functools.partial(_mamba_sample_state_body, bt=bt, inner=inner, heads=heads),
        grid_spec=pltpu.PrefetchScalarGridSpec(
            num_scalar_prefetch=1,
            grid=(nblk,),
            in_specs=[
                pl.BlockSpec((None, bt, inner, SSM_STATE), lambda i, e: (layer, i, 0, 0)),
                pl.BlockSpec((1, inner, bt), lambda i, e: (i, 0, 0)),
                pl.BlockSpec((1, bt, bc), lambda i, e: (i, 0, 0)),
                pl.BlockSpec((1, bt, bc), lambda i, e: (i, 0, 0)),
            ],
            out_specs=[
                pl.BlockSpec((bt, inner, SSM_STATE), lambda i, e: (i, 0, 0)),
                pl.BlockSpec((bt, inner), lambda i, e: (i, 0)),
            ],
        ),
        out_shape=[
            jax.ShapeDtypeStruct((n_sample, inner, SSM_STATE), F32),
            jax.ShapeDtypeStruct((n_sample, inner), F32),
        ],
        compiler_params=_cparams(("arbitrary",)),
        name="mamba_sample_state",
    )(e_tok[:, :heads].reshape(n_sample * heads), h0, to_cols(xdt), b_m.reshape(nblk, bt, bc),
      c_m.reshape(nblk, bt, bc))

    X = pl.pallas_call(
        functools.partial(_mamba_sample_post_body, inner=inner),
        grid=(1,),
        in_specs=[
            pl.BlockSpec((n_sample, D), lambda i: (sblk, 0)),
            pl.BlockSpec((n_sample, inner), lambda i: (0, 0)),
            pl.BlockSpec((n_sample, inner), lambda i: (0, 0)),
            pl.BlockSpec((n_sample, inner), lambda i: (0, 0)),
            _const_spec((1, inner)),
            _const_spec((1, inner)),
            _const_spec((inner, D)),
        ],
        out_specs=pl.BlockSpec((n_sample, D), lambda i: (sblk, 0)),
        out_shape=jax.ShapeDtypeStruct((M, D), F32),
        scratch_shapes=[pltpu.VMEM((n_sample, inner), BF16)],
        input_output_aliases={0: 0},
        compiler_params=_cparams(("arbitrary",)),
        name="mamba_sample_post",
    )(X, y_s, xs, z, dsk, ng2, w_out_b)

    ssm_p = ssm_p.reshape(n_batch, heads, SSM_HEAD_DIM, SSM_STATE)
    ssm_s = h_new.reshape(n_sample, heads, SSM_HEAD_DIM, SSM_STATE)
    return X, conv_p, conv_s, ssm_p, ssm_s


def _causal_dwconv_tile(ctx_ref, dw_ref, acc_ref, tm, d):
    S = SUBLANES_V7X
    first = CFM_PAD - (CFM_KERNEL - 1)
    rb, lb = CONV_ROW_BLOCK, CONV_LANE_BLOCK
    for r0 in range(0, tm, rb):
        for l0 in range(0, d, lb):
            lanes = slice(l0, l0 + lb)
            y = None
            for b in range(S):
                pb = None
                for a in range((first + CFM_KERNEL - 1) // S + 1):
                    k = S * a + b - first
                    if 0 <= k < CFM_KERNEL:
                        term = ctx_ref[r0 + S * a:r0 + S * a + rb + S, lanes] * dw_ref[k:k + 1, lanes]
                        pb = term if pb is None else pb + term
                part = pb[b:b + rb, :]
                y = part if y is None else y + part
            acc_ref[r0:r0 + rb, lanes] = y


def _cfm_prompt_body(x_ref, g_ref, w1_ref, b1_ref, dw_ref, dwb_ref, lng_ref, lnb_ref, w2_ref, b2_ref,
                     o_ref, buf_ref, ctx_ref, acc_ref, *, tm, tiles_per_seq, d):
    i = pl.program_id(0)

    @pl.when(i % tiles_per_seq == 0)
    def _():
        ctx_ref[0:CFM_PAD, :] = jnp.zeros((CFM_PAD, d), F32)
        ctx_ref[CFM_PAD + tm:CFM_PAD + tm + SUBLANES_V7X, :] = jnp.zeros((SUBLANES_V7X, d), F32)

    x = x_ref[...]
    h = _rms(x, g_ref[...]).astype(BF16)
    a = _dot(h, w1_ref[...]) + b1_ref[...]
    ctx_ref[CFM_PAD:CFM_PAD + tm, :] = a[:, :d] * _sigmoid(a[:, d:])
    _causal_dwconv_tile(ctx_ref, dw_ref, acc_ref, tm, d)
    buf_ref[0] = ctx_ref[CFM_PAD + tm - (CFM_KERNEL - 1):CFM_PAD + tm, :]
    ctx_ref[0:CFM_PAD, :] = ctx_ref[tm:tm + CFM_PAD, :]
    acc = acc_ref[...] + dwb_ref[...]
    hc = _silu(_layernorm(acc, lng_ref[...], lnb_ref[...])).astype(BF16)
    o_ref[...] = x + _dot(hc, w2_ref[...]) + b2_ref[...]


def _cfm_sample_body(x_ref, g_ref, w1_ref, b1_ref, dw_ref, dwb_ref, lng_ref, lnb_ref, w2_ref, b2_ref, cs_ref,
                     o_ref, new_ref, *, d):
    x = x_ref[...]
    h = _rms(x, g_ref[...]).astype(BF16)
    a = _dot(h, w1_ref[...]) + b1_ref[...]
    glu = a[:, :d] * _sigmoid(a[:, d:])
    acc = dwb_ref[...] + glu * dw_ref[CFM_KERNEL - 1:CFM_KERNEL, :]
    for k in range(CFM_KERNEL - 1):
        acc = acc + cs_ref[k] * dw_ref[k:k + 1, :]
    for k in range(CFM_KERNEL - 2):
        new_ref[k] = cs_ref[k + 1]
    new_ref[CFM_KERNEL - 2] = glu
    hc = _silu(_layernorm(acc, lng_ref[...], lnb_ref[...])).astype(BF16)
    o_ref[...] = x + _dot(hc, w2_ref[...]) + b2_ref[...]


def _conformer(X, n_prompt, n_batch, state_conv, g, w_pw1, b_pw1, dw_w, dw_b, ln_g, ln_b, w_pw2, b_pw2,
               *, tm, bt):
    M, D = X.shape
    n_sample = M - n_prompt
    seq = n_prompt // n_batch
    tiles_per_seq = seq // tm
    K = CFM_KERNEL
    g2 = g.reshape(1, D)
    w1, w2 = w_pw1.astype(BF16), w_pw2.astype(BF16)
    b1, b2 = b_pw1.reshape(1, 2 * D), b_pw2.reshape(1, D)
    dwb, lng, lnb = dw_b.reshape(1, D), ln_g.reshape(1, D), ln_b.reshape(1, D)
    weight_specs = [
        _const_spec((1, D)), _const_spec((D, 2 * D)), _const_spec((1, 2 * D)), _const_spec((K, D)),
        _const_spec((1, D)), _const_spec((1, D)), _const_spec((1, D)), _const_spec((D, D)), _const_spec((1, D)),
    ]
    weights = (g2, w1, b1, dw_w, dwb, lng, lnb, w2, b2)

    X, buf_p = pl.pallas_call(
        functools.partial(_cfm_prompt_body, tm=tm, tiles_per_seq=tiles_per_seq, d=D),
        grid=(n_prompt // tm,),
        in_specs=[pl.BlockSpec((tm, D), lambda i: (i, 0))] + weight_specs,
        out_specs=[
            pl.BlockSpec((tm, D), lambda i: (i, 0)),
            pl.BlockSpec((1, K - 1, D), lambda i: (i // tiles_per_seq, 0, 0)),
        ],
        out_shape=[jax.ShapeDtypeStruct((M, D), F32), jax.ShapeDtypeStruct((n_batch, K - 1, D), F32)],
        scratch_shapes=[pltpu.VMEM((CFM_PAD + tm + SUBLANES_V7X, D), F32), pltpu.VMEM((tm, D), F32)],
        input_output_aliases={0: 0},
        compiler_params=_cparams(("arbitrary",)),
        name="conformer_prompt",
    )(X, *weights)

    first = n_prompt // bt
    cs_t = jnp.transpose(state_conv, (1, 0, 2))
    X, new_t = pl.pallas_call(
        functools.partial(_cfm_sample_body, d=D),
        grid=(n_sample // bt,),
        in_specs=[pl.BlockSpec((bt, D), lambda i: (i + first, 0))] + weight_specs
        + [pl.BlockSpec((K - 1, bt, D), lambda i: (0, i, 0))],
        out_specs=[
            pl.BlockSpec((bt, D), lambda i: (i + first, 0)),
            pl.BlockSpec((K - 1, bt, D), lambda i: (0, i, 0)),
        ],
        out_shape=[jax.ShapeDtypeStruct((M, D), F32), jax.ShapeDtypeStruct((K - 1, n_sample, D), F32)],
        input_output_aliases={0: 0},
        compiler_params=_cparams(("arbitrary",)),
        name="conformer_sample",
    )(X, *weights, cs_t)
    return X, buf_p, jnp.transpose(new_t, (1, 0, 2))


def _stack(parts):
    return parts[0][None] if len(parts) == 1 else jnp.stack(parts)


def kernel(x_prompt, x_sample, state_ssm, state_conv_ssm, state_conv_cfm, norm_mix_g, norm_ffn_g, norm_final_g, a_w_in, a_b_in, a_ln_g, a_ln_b, a_w_s, a_b_s, a_w_out, b_w_in, b_conv_w, b_conv_b, b_dt_bias, b_a_log, b_d, b_norm_g, b_w_out, c_w_pw1, c_b_pw1, c_dw_w, c_dw_b, c_ln_g, c_ln_b, c_w_pw2, c_b_pw2, f_w_gate, f_w_up, f_w_down, e_w_router, e_w_gate, e_w_up, e_w_down):
    n_batch, seq, D = x_prompt.shape
    n_sample = x_sample.shape[0]
    n_prompt = n_batch * seq
    depth = norm_mix_g.shape[0]
    d_ff = f_w_gate.shape[2]
    assert x_sample.shape[1] == 1 and n_prompt % n_sample == 0 and seq % CHUNK == 0

    X = jnp.concatenate([x_prompt.reshape(n_prompt, D), x_sample.reshape(n_sample, D)], axis=0)
    v_p, v_s, ssm_p, ssm_s, cs_p, cs_s, cc_p, cc_s = [], [], [], [], [], [], [], []
    for i in range(depth):
        kind, j = i % 3, i // 3
        if kind == 0:
            X, vp, vs = _gmlp(X, n_prompt, n_batch, norm_mix_g[i], a_w_in[j], a_b_in[j], a_ln_g[j], a_ln_b[j],
                              a_w_s[j], a_b_s[j], a_w_out[j], tm=512)
            v_p.append(vp)
            v_s.append(vs)
        elif kind == 1:
            X, cbp, cbs, hlp, hls = _mamba(X, n_prompt, n_batch, state_ssm, j, state_conv_ssm[j], norm_mix_g[i],
                                           b_w_in[j], b_conv_w[j], b_conv_b[j], b_dt_bias[j], b_a_log[j], b_d[j],
                                           b_norm_g[j], b_w_out[j], tm=256, bt=8)
            cs_p.append(cbp)
            cs_s.append(cbs)
            ssm_p.append(hlp)
            ssm_s.append(hls)
        else:
            X, cbp, cbs = _conformer(X, n_prompt, n_batch, state_conv_cfm[j], norm_mix_g[i], c_w_pw1[j], c_b_pw1[j],
                                     c_dw_w[j], c_dw_b[j], c_ln_g[j], c_ln_b[j], c_w_pw2[j], c_b_pw2[j],
                                     tm=512, bt=32)
            cc_p.append(cbp)
            cc_s.append(cbs)
        k = i // 2
        if i % 2 == 0:
            X = _ffn_dense(X, n_prompt, norm_ffn_g[i], f_w_gate, f_w_up, f_w_down, k,
                           tm=512, f_chunk=FFN_CHUNK)
        else:
            X = _moe(X, n_prompt, norm_ffn_g[i], e_w_router[k], e_w_gate, e_w_up, e_w_down, k, norm_final_g,
                     tm_tok=384, tm_out=512, tm=512, f_chunk=FFN_CHUNK, final_norm=(i == depth - 1))
    assert depth % 2 == 0
    y_prompt, y_sample = X
    y_prompt = y_prompt.reshape(n_batch, seq, D)
    y_sample = y_sample.reshape(n_sample, 1, D)
    return (y_prompt, y_sample, _stack(v_p), _stack(v_s), _stack(ssm_p), _stack(ssm_s),
            _stack(cs_p), _stack(cs_s), _stack(cc_p), _stack(cc_s))
```

```python
import functools

import jax
import jax.numpy as jnp
from jax import lax
from jax.experimental import pallas as pl
from jax.experimental.pallas import tpu as pltpu

F32 = jnp.float32
BF16 = jnp.bfloat16
EPS = 1e-6
HIGHEST = lax.Precision.HIGHEST

LANES_V7X = 128
SUBLANES_V7X = 8
VMEM_LIMIT_V7X = 60 * 1024 * 1024

CHUNK = 128
GM_GROUPS = 8
SSM_GROUPS = 8
SSM_HEAD_DIM = 64
SSM_STATE = 128
SSM_CONV = 4
CFM_KERNEL = 31
N_EXPERTS = 8
CFM_PAD = 32
SSM_PAD = 8
FFN_CHUNK = 256
CONV_ROW_BLOCK = 128
CONV_LANE_BLOCK = 256


def _cparams(sem):
    return pltpu.CompilerParams(dimension_semantics=sem, vmem_limit_bytes=VMEM_LIMIT_V7X)


def _const_spec(shape):
    nd = len(shape)
    return pl.BlockSpec(shape, lambda *_: (0,) * nd, pipeline_mode=pl.Buffered(1))


def _dot(a, b):
    return jnp.dot(a, b, preferred_element_type=F32)


def _dot_nt(a, b):
    return lax.dot_general(a, b, (((1,), (1,)), ((), ())), preferred_element_type=F32)


def _dot_tn(a, b):
    return lax.dot_general(a, b, (((0,), (0,)), ((), ())), preferred_element_type=F32)


def _dot_hi(a, b):
    return jnp.dot(a, b, precision=HIGHEST, preferred_element_type=F32)


def _split_bf16(a):
    hi = a.astype(BF16)
    return hi, (a - hi.astype(F32)).astype(BF16)


def _dot_split_lhs(a, b_bf16):
    hi, lo = _split_bf16(a)
    return _dot(hi, b_bf16) + _dot(lo, b_bf16)


def _rms(x, g):
    return x * lax.rsqrt(jnp.mean(x * x, axis=-1, keepdims=True) + EPS) * g


def _layernorm(x, g, b):
    xc = x - jnp.mean(x, axis=-1, keepdims=True)
    return xc * lax.rsqrt(jnp.mean(xc * xc, axis=-1, keepdims=True) + EPS) * g + b


def _sigmoid(x):
    return 1.0 / (1.0 + jnp.exp(-x))


def _silu(x):
    return x * _sigmoid(x)


def _gelu(x):
    return 0.5 * x * (1.0 + lax.erf(x * (2.0 ** -0.5)))


def _softplus(x):
    return jnp.maximum(x, 0.0) + jnp.log(1.0 + jnp.exp(-jnp.abs(x)))


def _gmlp_prompt_body(x_ref, g_ref, win_ref, bin_ref, lng_ref, lnb_ref, ws_ref, bs_ref, wout_ref,
                      o_ref, v_ref, gated_ref, *, tm, width):
    x = x_ref[...]
    h = _rms(x, g_ref[...]).astype(BF16)
    u = _gelu(_dot(h, win_ref[:, :width]) + bin_ref[:, :width])
    hv = _gelu(_dot(h, win_ref[:, width:]) + bin_ref[:, width:])
    v = _layernorm(hv, lng_ref[...], lnb_ref[...])
    v_ref[0] = v[tm - CHUNK:, :]
    vb = v.astype(BF16)
    gdim = width // GM_GROUPS
    causal = (lax.broadcasted_iota(jnp.int32, (CHUNK, CHUNK), 0)
              >= lax.broadcasted_iota(jnp.int32, (CHUNK, CHUNK), 1))
    for g in range(GM_GROUPS):
        wc = jnp.where(causal, ws_ref[g], 0.0).astype(BF16)
        bias = bs_ref[:, g:g + 1]
        for c in range(tm // CHUNK):
            rows = slice(c * CHUNK, (c + 1) * CHUNK)
            cols = slice(g * gdim, (g + 1) * gdim)
            s = _dot(wc, vb[rows, cols]) + bias
            gated_ref[rows, cols] = (u[rows, cols] * s).astype(BF16)
    o_ref[...] = x + _dot(gated_ref[...], wout_ref[...])


def _gmlp_sample_body(x_ref, g_ref, win_ref, bin_ref, lng_ref, lnb_ref, ws0_ref, bs0_ref, wout_ref,
                      o_ref, v_ref, *, width):
    x = x_ref[...]
    h = _rms(x, g_ref[...]).astype(BF16)
    u = _gelu(_dot(h, win_ref[:, :width]) + bin_ref[:, :width])
    hv = _gelu(_dot(h, win_ref[:, width:]) + bin_ref[:, width:])
    v = _layernorm(hv, lng_ref[...], lnb_ref[...])
    v_ref[...] = v
    s = v * ws0_ref[...] + bs0_ref[...]
    o_ref[...] = x + _dot((u * s).astype(BF16), wout_ref[...])


def _gmlp(X, n_prompt, n_batch, g, w_in, b_in, ln_g, ln_b, w_s, b_s, w_out, *, tm):
    M, D = X.shape
    width = w_out.shape[0]
    seq = n_prompt // n_batch
    tiles_per_seq = seq // tm
    n_sample = M - n_prompt
    g2 = g.reshape(1, D)
    b_in2 = b_in.reshape(1, 2 * width)
    ln_g2, ln_b2 = ln_g.reshape(1, width), ln_b.reshape(1, width)
    w_in_b, w_out_b = w_in.astype(BF16), w_out.astype(BF16)

    X, v_p = pl.pallas_call(
        functools.partial(_gmlp_prompt_body, tm=tm, width=width),
        grid=(n_prompt // tm,),
        in_specs=[
            pl.BlockSpec((tm, D), lambda i: (i, 0)),
            _const_spec((1, D)),
            _const_spec((D, 2 * width)),
            _const_spec((1, 2 * width)),
            _const_spec((1, width)),
            _const_spec((1, width)),
            _const_spec((GM_GROUPS, CHUNK, CHUNK)),
            _const_spec((CHUNK, GM_GROUPS)),
            _const_spec((width, D)),
        ],
        out_specs=[
            pl.BlockSpec((tm, D), lambda i: (i, 0)),
            pl.BlockSpec((1, CHUNK, width), lambda i: (i // tiles_per_seq, 0, 0)),
        ],
        out_shape=[jax.ShapeDtypeStruct((M, D), F32),
                   jax.ShapeDtypeStruct((n_batch, CHUNK, width), F32)],
        scratch_shapes=[pltpu.VMEM((tm, width), BF16)],
        input_output_aliases={0: 0},
        compiler_params=_cparams(("arbitrary",)),
        name="gmlp_prompt",
    )(X, g2, w_in_b, b_in2, ln_g2, ln_b2, w_s, b_s.T, w_out_b)

    gdim = width // GM_GROUPS
    ws0 = jnp.repeat(w_s[:, 0, 0], gdim).reshape(1, width)
    bs0 = jnp.repeat(b_s[:, 0], gdim).reshape(1, width)
    sblk = n_prompt // n_sample
    X, v_s = pl.pallas_call(
        functools.partial(_gmlp_sample_body, width=width),
        grid=(1,),
        in_specs=[
            pl.BlockSpec((n_sample, D), lambda i: (sblk, 0)),
            _const_spec((1, D)),
            _const_spec((D, 2 * width)),
            _const_spec((1, 2 * width)),
            _const_spec((1, width)),
            _const_spec((1, width)),
            _const_spec((1, width)),
            _const_spec((1, width)),
            _const_spec((width, D)),
        ],
        out_specs=[
            pl.BlockSpec((n_sample, D), lambda i: (sblk, 0)),
            pl.BlockSpec((n_sample, width), lambda i: (0, 0)),
        ],
        out_shape=[jax.ShapeDtypeStruct((M, D), F32),
                   jax.ShapeDtypeStruct((n_sample, width), F32)],
        input_output_aliases={0: 0},
        compiler_params=_cparams(("arbitrary",)),
        name="gmlp_sample",
    )(X, g2, w_in_b, b_in2, ln_g2, ln_b2, ws0, bs0, w_out_b)
    return X, v_p, v_s.reshape(n_sample, 1, width)


def _swiglu_tile(h, wg_ref, wu_ref, wd_ref, f_chunk):
    d_ff = wd_ref.shape[0]
    acc = None
    for f in range(d_ff // f_chunk):
        cols = slice(f * f_chunk, (f + 1) * f_chunk)
        a = _dot(h, wg_ref[:, cols].astype(BF16))
        b = _dot(h, wu_ref[:, cols].astype(BF16))
        part = _dot((_silu(a) * b).astype(BF16), wd_ref[cols, :].astype(BF16))
        acc = part if acc is None else acc + part
    return acc


def _ffn_dense_body(x_ref, g_ref, wg_ref, wu_ref, wd_ref, o_ref, *, f_chunk):
    x = x_ref[...]
    h = _rms(x, g_ref[...]).astype(BF16)
    o_ref[...] = x + _swiglu_tile(h, wg_ref, wu_ref, wd_ref, f_chunk)


def _ffn_dense_rows(X, g2, wg, wu, wd, layer, *, tm, first_block, n_tiles, f_chunk):
    M, D = X.shape
    d_ff = wd.shape[1]

    def layer_spec(rows, cols):
        return pl.BlockSpec((None, rows, cols), lambda i: (layer, 0, 0), pipeline_mode=pl.Buffered(1))

    return pl.pallas_call(
        functools.partial(_ffn_dense_body, f_chunk=f_chunk),
        grid=(n_tiles,),
        in_specs=[
            pl.BlockSpec((tm, D), lambda i: (i + first_block, 0)),
            _const_spec((1, D)),
            layer_spec(D, d_ff),
            layer_spec(D, d_ff),
            layer_spec(d_ff, D),
        ],
        out_specs=pl.BlockSpec((tm, D), lambda i: (i + first_block, 0)),
        out_shape=jax.ShapeDtypeStruct((M, D), F32),
        input_output_aliases={0: 0},
        compiler_params=_cparams(("arbitrary",)),
        name="ffn_dense",
    )(X, g2, wg, wu, wd)


def _ffn_dense(X, n_prompt, g, w_gate, w_up, w_down, layer, *, tm, f_chunk):
    M, D = X.shape
    n_sample = M - n_prompt
    g2 = g.reshape(1, D)
    X = _ffn_dense_rows(X, g2, w_gate, w_up, w_down, layer, tm=tm, first_block=0, n_tiles=n_prompt // tm,
                        f_chunk=f_chunk)
    X = _ffn_dense_rows(X, g2, w_gate, w_up, w_down, layer, tm=n_sample, first_block=n_prompt // n_sample,
                        n_tiles=1, f_chunk=f_chunk)
    return X


def _router_body(x_ref, g_ref, wr_ref, idx_ref, wts_ref):
    L = LANES_V7X
    h_hi, h_lo = _split_bf16(_rms(x_ref[...], g_ref[...]))
    hi_part = _dot(h_hi, wr_ref[...])
    logits = hi_part[:, :L] + (hi_part[:, L:] + _dot(h_lo, wr_ref[:, :L]))
    lane = lax.broadcasted_iota(jnp.int32, logits.shape, 1)
    neg = jnp.float32(-jnp.inf)
    logits = jnp.where(lane < N_EXPERTS, logits, neg)
    m1 = jnp.max(logits, axis=-1, keepdims=True)
    i1 = jnp.min(jnp.where(logits == m1, lane, LANES_V7X), axis=-1, keepdims=True)
    rest = jnp.where(lane == i1, neg, logits)
    m2 = jnp.max(rest, axis=-1, keepdims=True)
    i2 = jnp.min(jnp.where(rest == m2, lane, LANES_V7X), axis=-1, keepdims=True)
    e = jnp.exp(m2 - m1)
    w1 = 1.0 / (1.0 + e)
    w2 = e / (1.0 + e)
    idx_ref[...] = jnp.where(lane == 0, i1, jnp.where(lane == 1, i2, 0))
    wts_ref[...] = jnp.where(lane == 0, w1, jnp.where(lane == 1, w2, 0.0))


def _slot_table_body(pos1_ref, pos2_ref, pad_lo_ref, pad_hi_ref, tok_ref, *, n_tokens, n_ranges):
    def pad(s, carry):
        tok_ref[s] = 0
        return carry

    for r in range(n_ranges):
        lax.fori_loop(pad_lo_ref[r], pad_hi_ref[r], pad, 0)

    def place(t, carry):
        tok_ref[pos1_ref[t]] = t
        tok_ref[pos2_ref[t]] = t
        return carry

    lax.fori_loop(0, n_tokens, place, 0, unroll=8)


def _issue_row_gather(src_hbm, idx_ref, base, buf, sem, n_rows):
    def body(r, carry):
        pltpu.make_async_copy(src_hbm.at[pl.ds(idx_ref[base + r], 1), :], buf.at[pl.ds(r, 1), :], sem).start()
        return carry
    lax.fori_loop(0, n_rows, body, 0, unroll=8)


def _wait_row_gather(src_hbm, buf, sem, n_rows):
    pltpu.make_async_copy(src_hbm.at[pl.ds(0, n_rows), :], buf, sem).wait()


def _moe_ffn_body(tile_expert_ref, n_tiles_ref, tok_ref, x_hbm, g_ref, wg_ref, wu_ref, wd_ref,
                  y_ref, buf_ref, sem_ref, *, tm, f_chunk):
    t = pl.program_id(0)
    n_tiles = n_tiles_ref[0]
    slot = t % 2

    @pl.when(t == 0)
    def _():
        _issue_row_gather(x_hbm, tok_ref, 0, buf_ref.at[0], sem_ref.at[0], tm)

    @pl.when(t + 1 < n_tiles)
    def _():
        _issue_row_gather(x_hbm, tok_ref, (t + 1) * tm, buf_ref.at[1 - slot], sem_ref.at[1 - slot], tm)

    @pl.when(t < n_tiles)
    def _():
        _wait_row_gather(x_hbm, buf_ref.at[slot], sem_ref.at[slot], tm)
        h = _rms(buf_ref[slot], g_ref[...]).astype(BF16)
        y_ref[...] = _swiglu_tile(h, wg_ref, wu_ref, wd_ref, f_chunk)

    @pl.when(t >= n_tiles)
    def _():
        y_ref[...] = jnp.zeros_like(y_ref)


def _moe_combine_body(p1_ref, p2_ref, x_ref, wts_ref, y_hbm, gf_ref, o_ref, buf1_ref, buf2_ref, sem_ref,
                      *, tm, row0, n_steps, final_norm):
    i = pl.program_id(0)
    slot = i % 2

    def issue(step, s):
        _issue_row_gather(y_hbm, p1_ref, row0 + step * tm, buf1_ref.at[s], sem_ref.at[0, s], tm)
        _issue_row_gather(y_hbm, p2_ref, row0 + step * tm, buf2_ref.at[s], sem_ref.at[1, s], tm)

    @pl.when(i == 0)
    def _():
        issue(0, 0)

    @pl.when(i + 1 < n_steps)
    def _():
        issue(i + 1, 1 - slot)

    _wait_row_gather(y_hbm, buf1_ref.at[slot], sem_ref.at[0, slot], tm)
    _wait_row_gather(y_hbm, buf2_ref.at[slot], sem_ref.at[1, slot], tm)
    out = x_ref[...] + (wts_ref[:, 0:1] * buf1_ref[slot] + wts_ref[:, 1:2] * buf2_ref[slot])
    if final_norm:
        out = _rms(out, gf_ref[...])
    o_ref[...] = out


def _moe(X, n_prompt, g, w_router, w_gate, w_up, w_down, layer, g_final, *, tm_tok, tm_out, tm, f_chunk,
         final_norm):
    M, D = X.shape
    E = N_EXPERTS
    d_ff = w_down.shape[2]
    g2 = g.reshape(1, D)
    L = LANES_V7X
    wr = jnp.zeros((D, L), F32).at[:, :E].set(w_router)
    wr_hi = wr.astype(BF16)
    wr_split = jnp.concatenate([wr_hi, (wr - wr_hi.astype(F32)).astype(BF16)], axis=1)

    idx, wts = pl.pallas_call(
        _router_body,
        grid=(M // tm_tok,),
        in_specs=[pl.BlockSpec((tm_tok, D), lambda i: (i, 0)), _const_spec((1, D)), _const_spec((D, 2 * L))],
        out_specs=[pl.BlockSpec((tm_tok, L), lambda i: (i, 0)), pl.BlockSpec((tm_tok, L), lambda i: (i, 0))],
        out_shape=[jax.ShapeDtypeStruct((M, L), jnp.int32), jax.ShapeDtypeStruct((M, L), F32)],
        compiler_params=_cparams(("arbitrary",)),
        name="moe_router",
    )(X, g2, wr_split)

    i1, i2 = idx[:, 0], idx[:, 1]
    eids = jnp.arange(E, dtype=jnp.int32)
    sel = ((i1[:, None] == eids) | (i2[:, None] == eids)).astype(jnp.int32)
    count = jnp.sum(sel, axis=0)
    tiles_e = (count + tm - 1) // tm
    tiles_end = jnp.cumsum(tiles_e)
    n_tiles = tiles_end[-1]
    start_e = (tiles_end - tiles_e) * tm
    pos = start_e[None, :] + jnp.cumsum(sel, axis=0) - sel
    pos1 = jnp.sum(jnp.where(i1[:, None] == eids, pos, 0), axis=1).astype(jnp.int32)
    pos2 = jnp.sum(jnp.where(i2[:, None] == eids, pos, 0), axis=1).astype(jnp.int32)
    max_tiles = (2 * M + E * (tm - 1)) // tm
    n_slots = max_tiles * tm
    pad_lo = jnp.concatenate([start_e + count, (n_tiles * tm)[None]]).astype(jnp.int32)
    pad_hi = jnp.concatenate([tiles_end * tm, jnp.array([n_slots], jnp.int32)]).astype(jnp.int32)
    tile_ids = jnp.arange(max_tiles, dtype=jnp.int32)
    tile_expert = jnp.sum((jnp.minimum(tile_ids, n_tiles - 1)[:, None] >= tiles_end[None, :]).astype(jnp.int32),
                          axis=1)
    tile_expert = jnp.minimum(tile_expert, E - 1).astype(jnp.int32)
    n_tiles_arr = n_tiles.astype(jnp.int32).reshape(1)

    smem = pl.BlockSpec(memory_space=pltpu.SMEM)
    tok_of_slot = pl.pallas_call(
        functools.partial(_slot_table_body, n_tokens=M, n_ranges=E + 1),
        in_specs=[smem, smem, smem, smem],
        out_specs=smem,
        out_shape=jax.ShapeDtypeStruct((n_slots,), jnp.int32),
        name="moe_slot_table",
    )(pos1, pos2, pad_lo, pad_hi)

    def expert_spec(rows, cols, buffers=1):
        return pl.BlockSpec((None, None, rows, cols), lambda t, te, nt, tok: (layer, te[t], 0, 0),
                            pipeline_mode=pl.Buffered(buffers))

    y_sorted = pl.pallas_call(
        functools.partial(_moe_ffn_body, tm=tm, f_chunk=f_chunk),
        grid_spec=pltpu.PrefetchScalarGridSpec(
            num_scalar_prefetch=3,
            grid=(max_tiles,),
            in_specs=[
                pl.BlockSpec(memory_space=pl.ANY),
                pl.BlockSpec((1, D), lambda t, te, nt, tok: (0, 0)),
                expert_spec(D, d_ff),
                expert_spec(D, d_ff),
                expert_spec(d_ff, D, buffers=2),
            ],
            out_specs=pl.BlockSpec((tm, D), lambda t, te, nt, tok: (t, 0)),
            scratch_shapes=[pltpu.VMEM((2, tm, D), F32), pltpu.SemaphoreType.DMA((2,))],
        ),
        out_shape=jax.ShapeDtypeStruct((n_slots, D), F32),
        compiler_params=_cparams(("arbitrary",)),
        name="moe_ffn",
    )(tile_expert, n_tiles_arr, tok_of_slot, X, g2, w_gate, w_up, w_down)

    def combine_rows(tm_c, first_block, n_steps, in_place):
        out_first = first_block if in_place else 0
        return pl.pallas_call(
            functools.partial(_moe_combine_body, tm=tm_c, row0=first_block * tm_c, n_steps=n_steps,
                              final_norm=final_norm),
            grid_spec=pltpu.PrefetchScalarGridSpec(
                num_scalar_prefetch=2,
                grid=(n_steps,),
                in_specs=[
                    pl.BlockSpec((tm_c, D), lambda i, p1, p2: (i + first_block, 0)),
                    pl.BlockSpec((tm_c, L), lambda i, p1, p2: (i + first_block, 0)),
                    pl.BlockSpec(memory_space=pl.ANY),
                    pl.BlockSpec((1, D), lambda i, p1, p2: (0, 0)),
                ],
                out_specs=pl.BlockSpec((tm_c, D), lambda i, p1, p2: (i + out_first, 0)),
                scratch_shapes=[pltpu.VMEM((2, tm_c, D), F32), pltpu.VMEM((2, tm_c, D), F32),
                                pltpu.SemaphoreType.DMA((2, 2))],
            ),
            out_shape=jax.ShapeDtypeStruct((M if in_place else n_steps * tm_c, D), F32),
            input_output_aliases={2: 0} if in_place else {},
            compiler_params=_cparams(("arbitrary",)),
            name="moe_combine",
        )(pos1, pos2, X, wts, y_sorted, g_final.reshape(1, D))

    if not final_norm:
        return combine_rows(tm_tok, 0, M // tm_tok, True)
    n_sample = M - n_prompt
    y_p = combine_rows(tm_out, 0, n_prompt // tm_out, False)
    y_s = combine_rows(n_sample, n_prompt // n_sample, 1, False)
    return y_p, y_s


def _ssd_post(y, xs, z, dsk, ng, wout_ref, ynorm_ref, inner):
    y = (y + dsk * xs) * _silu(z)
    gw = inner // SSM_GROUPS
    for g in range(SSM_GROUPS):
        cols = slice(g * gw, (g + 1) * gw)
        yg = y[:, cols]
        yg = yg * lax.rsqrt(jnp.mean(yg * yg, axis=-1, keepdims=True) + EPS) * ng[:, cols]
        ynorm_ref[:, cols] = yg.astype(BF16)
    return _dot(ynorm_ref[...], wout_ref[...])


def _mamba_prompt_body(x_ref, g_ref, win_ref, wdt_ref, cw_ref, cb_ref, dtb_ref, a_ref, dsk_ref, ng_ref,
                       wout_ref, r_ref, o_ref, conv_ref, ssm_ref, ctx_ref, s_ref, y_ref, ynorm_ref,
                       *, tm, tiles_per_seq, inner, bc):
    i = pl.program_id(0)

    @pl.when(i % tiles_per_seq == 0)
    def _():
        ctx_ref[0:SSM_PAD, :] = jnp.zeros((SSM_PAD, ctx_ref.shape[1]), F32)
        s_ref[...] = jnp.zeros_like(s_ref)

    x = x_ref[...]
    h = _rms(x, g_ref[...]).astype(BF16)
    proj = _dot(h, win_ref[...])
    z = proj[:, :inner]
    xbc = proj[:, inner:]
    ctx_ref[SSM_PAD:SSM_PAD + tm, :] = xbc
    conv_ref[0] = ctx_ref[SSM_PAD + tm - (SSM_CONV - 1):SSM_PAD + tm, :]
    conv = cb_ref[...] + xbc * cw_ref[SSM_CONV - 1:SSM_CONV, :]
    for k in range(SSM_CONV - 1):
        off = SSM_PAD - (SSM_CONV - 1) + k
        conv = conv + ctx_ref[off:off + tm, :] * cw_ref[k:k + 1, :]
    ctx_ref[0:SSM_PAD, :] = ctx_ref[tm:tm + SSM_PAD, :]
    act = _silu(conv)
    xs = act[:, :inner]
    bm = act[:, inner:inner + bc].astype(BF16)
    cm = act[:, inner + bc:].astype(BF16)
    dt = _softplus(_dot(h, wdt_ref[...]) + dtb_ref[...])
    da = dt * a_ref[...]

    rows_i = lax.broadcasted_iota(jnp.int32, (CHUNK, CHUNK), 0)
    cols_i = lax.broadcasted_iota(jnp.int32, (CHUNK, CHUNK), 1)
    causal = rows_i >= cols_i
    tril = causal.astype(F32)
    gw = inner // SSM_GROUPS
    hpg = gw // SSM_HEAD_DIM
    lane_head = lax.broadcasted_iota(jnp.int32, (CHUNK, gw), 1) // SSM_HEAD_DIM
    r_mat = r_ref[...]

    for c in range(tm // CHUNK):
        rows = slice(c * CHUNK, (c + 1) * CHUNK)
        dt_c = dt[rows, :]
        cum = _dot_hi(tril, da[rows, :])
        cum_t = cum.T
        dt_t = dt_c.T
        ecum = jnp.exp(cum)
        e_full = _dot_split_lhs(ecum, r_mat)
        w_in = jnp.exp(cum[CHUNK - 1:CHUNK, :] - cum) * dt_c
        xs_c = xs[rows, :]
        xw = (xs_c * _dot_split_lhs(w_in, r_mat)).astype(BF16)
        xs_b = xs_c.astype(BF16)
        for g in range(SSM_GROUPS):
            gcols = slice(g * gw, (g + 1) * gw)
            ncols = slice(g * SSM_STATE, (g + 1) * SSM_STATE)
            b_g = bm[rows, ncols]
            c_g = cm[rows, ncols]
            cb = _dot_nt(c_g, b_g)
            x_g = xs_b[:, gcols]
            s_g = s_ref[g]
            wts, x_heads = [], []
            for r in range(hpg):
                hd = g * hpg + r
                seg = cum[:, hd:hd + 1] - cum_t[hd:hd + 1, :]
                decay = jnp.where(causal, jnp.exp(jnp.where(causal, seg, 0.0)), 0.0)
                wts.append((cb * decay * dt_t[hd:hd + 1, :]).astype(BF16))
                x_heads.append(jnp.where(lane_head == r, x_g, jnp.zeros_like(x_g)))
            y_intra = _dot(jnp.concatenate(wts, axis=1), jnp.concatenate(x_heads, axis=0))
            y_ref[rows, gcols] = y_intra + _dot_nt(c_g, s_g.astype(BF16)) * e_full[:, gcols]
            upd = _dot_tn(xw[:, gcols], b_g)
            for r in range(hpg):
                hd = g * hpg + r
                hrows = slice(r * SSM_HEAD_DIM, (r + 1) * SSM_HEAD_DIM)
                s_ref[g, hrows, :] = s_g[hrows, :] * ecum[CHUNK - 1:CHUNK, hd:hd + 1] + upd[hrows, :]

    out = _ssd_post(y_ref[...], xs, z, dsk_ref[...], ng_ref[...], wout_ref, ynorm_ref, inner)
    o_ref[...] = x + out
    ssm_ref[0] = s_ref[...]


def _mamba_sample_pre_body(x_ref, g_ref, win_ref, wdt_ref, cw_ref, cb_ref, dtb_ref, a_ref, r_ref, cs_ref,
                           z_ref, xs_ref, b_ref, c_ref, xdt_ref, e_ref, newconv_ref, *, inner, bc):
    h = _rms(x_ref[...], g_ref[...]).astype(BF16)
    proj = _dot(h, win_ref[...])
    z_ref[...] = proj[:, :inner]
    xbc = proj[:, inner:]
    conv = cb_ref[...] + xbc * cw_ref[SSM_CONV - 1:SSM_CONV, :]
    for k in range(SSM_CONV - 1):
        conv = conv + cs_ref[k] * cw_ref[k:k + 1, :]
    for k in range(SSM_CONV - 2):
        newconv_ref[k] = cs_ref[k + 1]
    newconv_ref[SSM_CONV - 2] = xbc
    act = _silu(conv)
    xs = act[:, :inner]
    xs_ref[...] = xs
    b_ref[...] = act[:, inner:inner + bc]
    c_ref[...] = act[:, inner + bc:]
    dt = _softplus(_dot(h, wdt_ref[...]) + dtb_ref[...])
    xdt_ref[...] = xs * _dot_split_lhs(dt, r_ref[...])
    e_ref[...] = jnp.exp(dt * a_ref[...])


def _mamba_sample_state_body(e_ref, h0_ref, xdt_t_ref, b_ref, c_ref, hn_ref, y_ref, *, bt, inner, heads):
    i = pl.program_id(0)
    gw = inner // SSM_GROUPS
    hpg = gw // SSM_HEAD_DIM
    row_id = lax.broadcasted_iota(jnp.int32, (bt, bt * SSM_STATE), 0)
    lane_tok = lax.broadcasted_iota(jnp.int32, (bt, bt * SSM_STATE), 1) // SSM_STATE
    for g in range(SSM_GROUPS):
        rows = slice(g * gw, (g + 1) * gw)
        ncols = slice(g * SSM_STATE, (g + 1) * SSM_STATE)
        states = []
        for j in range(bt):
            b_row = b_ref[0, j:j + 1, ncols]
            for r in range(hpg):
                hrows = slice(g * gw + r * SSM_HEAD_DIM, g * gw + (r + 1) * SSM_HEAD_DIM)
                e = e_ref[(i * bt + j) * heads + g * hpg + r]
                hn_ref[j, hrows, :] = h0_ref[j, hrows, :] * e + xdt_t_ref[0, hrows, j:j + 1] * b_row
            states.append(hn_ref[j, rows, :].astype(BF16))
        c_g = c_ref[0, :, ncols]
        c_diag = jnp.where(row_id == lane_tok, jnp.concatenate([c_g] * bt, axis=1), 0.0).astype(BF16)
        y_ref[:, rows] = _dot_nt(c_diag, jnp.concatenate(states, axis=1))


def _mamba_sample_post_body(x_ref, y_ref, xs_ref, z_ref, dsk_ref, ng_ref, wout_ref, o_ref, ynorm_ref, *, inner):
    out = _ssd_post(y_ref[...], xs_ref[...], z_ref[...], dsk_ref[...], ng_ref[...], wout_ref, ynorm_ref, inner)
    o_ref[...] = x_ref[...] + out


def _mamba(X, n_prompt, n_batch, state_ssm_all, layer, state_conv, g, w_in, conv_w, conv_b, dt_bias, a_log, d_skip,
           norm_g, w_out, *, tm, bt):
    M, D = X.shape
    inner = w_out.shape[0]
    heads = a_log.shape[0]
    conv_dim = conv_w.shape[1]
    bc = (conv_dim - inner) // 2
    n_sample = M - n_prompt
    seq = n_prompt // n_batch
    tiles_per_seq = seq // tm
    gw = inner // SSM_GROUPS
    L = LANES_V7X

    g2 = g.reshape(1, D)
    w_main = w_in[:, :inner + conv_dim].astype(BF16)
    w_dt = jnp.zeros((D, L), F32).at[:, :heads].set(w_in[:, inner + conv_dim:]).astype(BF16)
    cb2 = conv_b.reshape(1, conv_dim)
    dtb = jnp.zeros((1, L), F32).at[0, :heads].set(dt_bias)
    a_neg = jnp.zeros((1, L), F32).at[0, :heads].set(-jnp.exp(a_log))
    dsk = jnp.repeat(d_skip, SSM_HEAD_DIM).reshape(1, inner)
    ng2 = norm_g.reshape(1, inner)
    w_out_b = w_out.astype(BF16)
    r_mat = (jnp.arange(L, dtype=jnp.int32)[:, None]
             == (jnp.arange(inner, dtype=jnp.int32) // SSM_HEAD_DIM)[None, :]).astype(BF16)

    X, conv_p, ssm_p = pl.pallas_call(
        functools.partial(_mamba_prompt_body, tm=tm, tiles_per_seq=tiles_per_seq, inner=inner, bc=bc),
        grid=(n_prompt // tm,),
        in_specs=[
            pl.BlockSpec((tm, D), lambda i: (i, 0)),
            _const_spec((1, D)),
            _const_spec((D, inner + conv_dim)),
            _const_spec((D, L)),
            _const_spec((SSM_CONV, conv_dim)),
            _const_spec((1, conv_dim)),
            _const_spec((1, L)),
            _const_spec((1, L)),
            _const_spec((1, inner)),
            _const_spec((1, inner)),
            _const_spec((inner, D)),
            _const_spec((L, inner)),
        ],
        out_specs=[
            pl.BlockSpec((tm, D), lambda i: (i, 0)),
            pl.BlockSpec((1, SSM_CONV - 1, conv_dim), lambda i: (i // tiles_per_seq, 0, 0)),
            pl.BlockSpec((1, SSM_GROUPS, gw, SSM_STATE), lambda i: (i // tiles_per_seq, 0, 0, 0)),
        ],
        out_shape=[
            jax.ShapeDtypeStruct((M, D), F32),
            jax.ShapeDtypeStruct((n_batch, SSM_CONV - 1, conv_dim), F32),
            jax.ShapeDtypeStruct((n_batch, SSM_GROUPS, gw, SSM_STATE), F32),
        ],
        scratch_shapes=[
            pltpu.VMEM((SSM_PAD + tm, conv_dim), F32),
            pltpu.VMEM((SSM_GROUPS, gw, SSM_STATE), F32),
            pltpu.VMEM((tm, inner), F32),
            pltpu.VMEM((tm, inner), BF16),
        ],
        input_output_aliases={0: 0},
        compiler_params=_cparams(("arbitrary",)),
        name="mamba_prompt",
    )(X, g2, w_main, w_dt, conv_w, cb2, dtb, a_neg, dsk, ng2, w_out_b, r_mat)

    sblk = n_prompt // n_sample
    cs_t = jnp.transpose(state_conv, (1, 0, 2))
    z, xs, b_m, c_m, xdt, e_tok, newconv_t = pl.pallas_call(
        functools.partial(_mamba_sample_pre_body, inner=inner, bc=bc),
        grid=(1,),
        in_specs=[
            pl.BlockSpec((n_sample, D), lambda i: (sblk, 0)),
            _const_spec((1, D)),
            _const_spec((D, inner + conv_dim)),
            _const_spec((D, L)),
            _const_spec((SSM_CONV, conv_dim)),
            _const_spec((1, conv_dim)),
            _const_spec((1, L)),
            _const_spec((1, L)),
            _const_spec((L, inner)),
            _const_spec((SSM_CONV - 1, n_sample, conv_dim)),
        ],
        out_specs=[
            pl.BlockSpec((n_sample, inner), lambda i: (0, 0)),
            pl.BlockSpec((n_sample, inner), lambda i: (0, 0)),
            pl.BlockSpec((n_sample, bc), lambda i: (0, 0)),
            pl.BlockSpec((n_sample, bc), lambda i: (0, 0)),
            pl.BlockSpec((n_sample, inner), lambda i: (0, 0)),
            pl.BlockSpec((n_sample, L), lambda i: (0, 0)),
            pl.BlockSpec((SSM_CONV - 1, n_sample, conv_dim), lambda i: (0, 0, 0)),
        ],
        out_shape=[
            jax.ShapeDtypeStruct((n_sample, inner), F32),
            jax.ShapeDtypeStruct((n_sample, inner), F32),
            jax.ShapeDtypeStruct((n_sample, bc), F32),
            jax.ShapeDtypeStruct((n_sample, bc), F32),
            jax.ShapeDtypeStruct((n_sample, inner), F32),
            jax.ShapeDtypeStruct((n_sample, L), F32),
            jax.ShapeDtypeStruct((SSM_CONV - 1, n_sample, conv_dim), F32),
        ],
        compiler_params=_cparams(("arbitrary",)),
        name="mamba_sample_pre",
    )(X, g2, w_main, w_dt, conv_w, cb2, dtb, a_neg, r_mat, cs_t)
    conv_s = jnp.transpose(newconv_t, (1, 0, 2))

    nblk = n_sample // bt

    def to_cols(arr):
        return jnp.transpose(arr.reshape(nblk, bt, inner), (0, 2, 1))

    h0 = state_ssm_all.reshape(state_ssm_all.shape[0], n_sample, inner, SSM_STATE)
    h_new, y_s = pl.pallas_call(
        functools.partial(_mamba_sample_state_body, bt=bt, inner=inner, heads=heads),
        grid_spec=pltpu.PrefetchScalarGridSpec(
            num_scalar_prefetch=1,
            grid=(nblk,),
            in_specs=[
                pl.BlockSpec((None, bt, inner, SSM_STATE), lambda i, e: (layer, i, 0, 0)),
                pl.BlockSpec((1, inner, bt), lambda i, e: (i, 0, 0)),
                pl.BlockSpec((1, bt, bc), lambda i, e: (i, 0, 0)),
                pl.BlockSpec((1, bt, bc), lambda i, e: (i, 0, 0)),
            ],
            out_specs=[
                pl.BlockSpec((bt, inner, SSM_STATE), lambda i, e: (i, 0, 0)),
                pl.BlockSpec((bt, inner), lambda i, e: (i, 0)),
            ],
        ),
        out_shape=[
            jax.ShapeDtypeStruct((n_sample, inner, SSM_STATE), F32),
            jax.ShapeDtypeStruct((n_sample, inner), F32),
        ],
        compiler_params=_cparams(("arbitrary",)),
        name="mamba_sample_state",
    )(e_tok[:, :heads].reshape(n_sample * heads), h0, to_cols(xdt), b_m.reshape(nblk, bt, bc),
      c_m.reshape(nblk, bt, bc))

    X = pl.pallas_call(
        functools.partial(_mamba_sample_post_body, inner=inner),
        grid=(1,),
        in_specs=[
            pl.BlockSpec((n_sample, D), lambda i: (sblk, 0)),
            pl.BlockSpec((n_sample, inner), lambda i: (0, 0)),
            pl.BlockSpec((n_sample, inner), lambda i: (0, 0)),
            pl.BlockSpec((n_sample, inner), lambda i: (0, 0)),
            _const_spec((1, inner)),
            _const_spec((1, inner)),
            _const_spec((inner, D)),
        ],
        out_specs=pl.BlockSpec((n_sample, D), lambda i: (sblk, 0)),
        out_shape=jax.ShapeDtypeStruct((M, D), F32),
        scratch_shapes=[pltpu.VMEM((n_sample, inner), BF16)],
        input_output_aliases={0: 0},
        compiler_params=_cparams(("arbitrary",)),
        name="mamba_sample_post",
    )(X, y_s, xs, z, dsk, ng2, w_out_b)

    ssm_p = ssm_p.reshape(n_batch, heads, SSM_HEAD_DIM, SSM_STATE)
    ssm_s = h_new.reshape(n_sample, heads, SSM_HEAD_DIM, SSM_STATE)
    return X, conv_p, conv_s, ssm_p, ssm_s


def _causal_dwconv_tile(ctx_ref, dw_ref, acc_ref, tm, d):
    S = SUBLANES_V7X
    first = CFM_PAD - (CFM_KERNEL - 1)
    rb, lb = CONV_ROW_BLOCK, CONV_LANE_BLOCK
    for r0 in range(0, tm, rb):
        for l0 in range(0, d, lb):
            lanes = slice(l0, l0 + lb)
            y = None
            for b in range(S):
                pb = None
                for a in range((first + CFM_KERNEL - 1) // S + 1):
                    k = S * a + b - first
                    if 0 <= k < CFM_KERNEL:
                        term = ctx_ref[r0 + S * a:r0 + S * a + rb + S, lanes] * dw_ref[k:k + 1, lanes]
                        pb = term if pb is None else pb + term
                part = pb[b:b + rb, :]
                y = part if y is None else y + part
            acc_ref[r0:r0 + rb, lanes] = y


def _cfm_prompt_body(x_ref, g_ref, w1_ref, b1_ref, dw_ref, dwb_ref, lng_ref, lnb_ref, w2_ref, b2_ref,
                     o_ref, buf_ref, ctx_ref, acc_ref, *, tm, tiles_per_seq, d):
    i = pl.program_id(0)

    @pl.when(i % tiles_per_seq == 0)
    def _():
        ctx_ref[0:CFM_PAD, :] = jnp.zeros((CFM_PAD, d), F32)
        ctx_ref[CFM_PAD + tm:CFM_PAD + tm + SUBLANES_V7X, :] = jnp.zeros((SUBLANES_V7X, d), F32)

    x = x_ref[...]
    h = _rms(x, g_ref[...]).astype(BF16)
    a = _dot(h, w1_ref[...]) + b1_ref[...]
    ctx_ref[CFM_PAD:CFM_PAD + tm, :] = a[:, :d] * _sigmoid(a[:, d:])
    _causal_dwconv_tile(ctx_ref, dw_ref, acc_ref, tm, d)
    buf_ref[0] = ctx_ref[CFM_PAD + tm - (CFM_KERNEL - 1):CFM_PAD + tm, :]
    ctx_ref[0:CFM_PAD, :] = ctx_ref[tm:tm + CFM_PAD, :]
    acc = acc_ref[...] + dwb_ref[...]
    hc = _silu(_layernorm(acc, lng_ref[...], lnb_ref[...])).astype(BF16)
    o_ref[...] = x + _dot(hc, w2_ref[...]) + b2_ref[...]


def _cfm_sample_body(x_ref, g_ref, w1_ref, b1_ref, dw_ref, dwb_ref, lng_ref, lnb_ref, w2_ref, b2_ref, cs_ref,
                     o_ref, new_ref, *, d):
    x = x_ref[...]
    h = _rms(x, g_ref[...]).astype(BF16)
    a = _dot(h, w1_ref[...]) + b1_ref[...]
    glu = a[:, :d] * _sigmoid(a[:, d:])
    acc = dwb_ref[...] + glu * dw_ref[CFM_KERNEL - 1:CFM_KERNEL, :]
    for k in range(CFM_KERNEL - 1):
        acc = acc + cs_ref[k] * dw_ref[k:k + 1, :]
    for k in range(CFM_KERNEL - 2):
        new_ref[k] = cs_ref[k + 1]
    new_ref[CFM_KERNEL - 2] = glu
    hc = _silu(_layernorm(acc, lng_ref[...], lnb_ref[...])).astype(BF16)
    o_ref[...] = x + _dot(hc, w2_ref[...]) + b2_ref[...]


def _conformer(X, n_prompt, n_batch, state_conv, g, w_pw1, b_pw1, dw_w, dw_b, ln_g, ln_b, w_pw2, b_pw2,
               *, tm, bt):
    M, D = X.shape
    n_sample = M - n_prompt
    seq = n_prompt // n_batch
    tiles_per_seq = seq // tm
    K = CFM_KERNEL
    g2 = g.reshape(1, D)
    w1, w2 = w_pw1.astype(BF16), w_pw2.astype(BF16)
    b1, b2 = b_pw1.reshape(1, 2 * D), b_pw2.reshape(1, D)
    dwb, lng, lnb = dw_b.reshape(1, D), ln_g.reshape(1, D), ln_b.reshape(1, D)
    weight_specs = [
        _const_spec((1, D)), _const_spec((D, 2 * D)), _const_spec((1, 2 * D)), _const_spec((K, D)),
        _const_spec((1, D)), _const_spec((1, D)), _const_spec((1, D)), _const_spec((D, D)), _const_spec((1, D)),
    ]
    weights = (g2, w1, b1, dw_w, dwb, lng, lnb, w2, b2)

    X, buf_p = pl.pallas_call(
        functools.partial(_cfm_prompt_body, tm=tm, tiles_per_seq=tiles_per_seq, d=D),
        grid=(n_prompt // tm,),
        in_specs=[pl.BlockSpec((tm, D), lambda i: (i, 0))] + weight_specs,
        out_specs=[
            pl.BlockSpec((tm, D), lambda i: (i, 0)),
            pl.BlockSpec((1, K - 1, D), lambda i: (i // tiles_per_seq, 0, 0)),
        ],
        out_shape=[jax.ShapeDtypeStruct((M, D), F32), jax.ShapeDtypeStruct((n_batch, K - 1, D), F32)],
        scratch_shapes=[pltpu.VMEM((CFM_PAD + tm + SUBLANES_V7X, D), F32), pltpu.VMEM((tm, D), F32)],
        input_output_aliases={0: 0},
        compiler_params=_cparams(("arbitrary",)),
        name="conformer_prompt",
    )(X, *weights)

    first = n_prompt // bt
    cs_t = jnp.transpose(state_conv, (1, 0, 2))
    X, new_t = pl.pallas_call(
        functools.partial(_cfm_sample_body, d=D),
        grid=(n_sample // bt,),
        in_specs=[pl.BlockSpec((bt, D), lambda i: (i + first, 0))] + weight_specs
        + [pl.BlockSpec((K - 1, bt, D), lambda i: (0, i, 0))],
        out_specs=[
            pl.BlockSpec((bt, D), lambda i: (i + first, 0)),
            pl.BlockSpec((K - 1, bt, D), lambda i: (0, i, 0)),
        ],
        out_shape=[jax.ShapeDtypeStruct((M, D), F32), jax.ShapeDtypeStruct((K - 1, n_sample, D), F32)],
        input_output_aliases={0: 0},
        compiler_params=_cparams(("arbitrary",)),
        name="conformer_sample",
    )(X, *weights, cs_t)
    return X, buf_p, jnp.transpose(new_t, (1, 0, 2))


def _stack(parts):
    return parts[0][None] if len(parts) == 1 else jnp.stack(parts)


def kernel(x_prompt, x_sample, state_ssm, state_conv_ssm, state_conv_cfm, norm_mix_g, norm_ffn_g, norm_final_g, a_w_in, a_b_in, a_ln_g, a_ln_b, a_w_s, a_b_s, a_w_out, b_w_in, b_conv_w, b_conv_b, b_dt_bias, b_a_log, b_d, b_norm_g, b_w_out, c_w_pw1, c_b_pw1, c_dw_w, c_dw_b, c_ln_g, c_ln_b, c_w_pw2, c_b_pw2, f_w_gate, f_w_up, f_w_down, e_w_router, e_w_gate, e_w_up, e_w_down):
    n_batch, seq, D = x_prompt.shape
    n_sample = x_sample.shape[0]
    n_prompt = n_batch * seq
    depth = norm_mix_g.shape[0]
    d_ff = f_w_gate.shape[2]
    assert x_sample.shape[1] == 1 and n_prompt % n_sample == 0 and seq % CHUNK == 0

    X = jnp.concatenate([x_prompt.reshape(n_prompt, D), x_sample.reshape(n_sample, D)], axis=0)
    v_p, v_s, ssm_p, ssm_s, cs_p, cs_s, cc_p, cc_s = [], [], [], [], [], [], [], []
    for i in range(depth):
        kind, j = i % 3, i // 3
        if kind == 0:
            X, vp, vs = _gmlp(X, n_prompt, n_batch, norm_mix_g[i], a_w_in[j], a_b_in[j], a_ln_g[j], a_ln_b[j],
                              a_w_s[j], a_b_s[j], a_w_out[j], tm=512)
            v_p.append(vp)
            v_s.append(vs)
        elif kind == 1:
            X, cbp, cbs, hlp, hls = _mamba(X, n_prompt, n_batch, state_ssm, j, state_conv_ssm[j], norm_mix_g[i],
                                           b_w_in[j], b_conv_w[j], b_conv_b[j], b_dt_bias[j], b_a_log[j], b_d[j],
                                           b_norm_g[j], b_w_out[j], tm=256, bt=8)
            cs_p.append(cbp)
            cs_s.append(cbs)
            ssm_p.append(hlp)
            ssm_s.append(hls)
        else:
            X, cbp, cbs = _conformer(X, n_prompt, n_batch, state_conv_cfm[j], norm_mix_g[i], c_w_pw1[j], c_b_pw1[j],
                                     c_dw_w[j], c_dw_b[j], c_ln_g[j], c_ln_b[j], c_w_pw2[j], c_b_pw2[j],
                                     tm=512, bt=32)
            cc_p.append(cbp)
            cc_s.append(cbs)
        k = i // 2
        if i % 2 == 0:
            X = _ffn_dense(X, n_prompt, norm_ffn_g[i], f_w_gate, f_w_up, f_w_down, k,
                           tm=512, f_chunk=FFN_CHUNK)
        else:
            X = _moe(X, n_prompt, norm_ffn_g[i], e_w_router[k], e_w_gate, e_w_up, e_w_down, k, norm_final_g,
                     tm_tok=384, tm_out=512, tm=512, f_chunk=FFN_CHUNK, final_norm=(i == depth - 1))
    assert depth % 2 == 0
    y_prompt, y_sample = X
    y_prompt = y_prompt.reshape(n_batch, seq, D)
    y_sample = y_sample.reshape(n_sample, 1, D)
    return (y_prompt, y_sample, _stack(v_p), _stack(v_s), _stack(ssm_p), _stack(ssm_s),
            _stack(cs_p), _stack(cs_s), _stack(cc_p), _stack(cc_s))
```

```python
import functools

import jax
import jax.numpy as jnp
from jax import lax
from jax.experimental import pallas as pl
from jax.experimental.pallas import tpu as pltpu

F32 = jnp.float32
BF16 = jnp.bfloat16
EPS = 1e-6
HIGHEST = lax.Precision.HIGHEST

LANES_V7X = 128
SUBLANES_V7X = 8
VMEM_LIMIT_V7X = 60 * 1024 * 1024

CHUNK = 128
GM_GROUPS = 8
SSM_GROUPS = 8
SSM_HEAD_DIM = 64
SSM_STATE = 128
SSM_CONV = 4
CFM_KERNEL = 31
N_EXPERTS = 8
CFM_PAD = 32
SSM_PAD = 8
FFN_CHUNK = 256
CONV_ROW_BLOCK = 128
CONV_LANE_BLOCK = 128


def _cparams(sem):
    return pltpu.CompilerParams(dimension_semantics=sem, vmem_limit_bytes=VMEM_LIMIT_V7X)


def _const_spec(shape):
    nd = len(shape)
    return pl.BlockSpec(shape, lambda *_: (0,) * nd, pipeline_mode=pl.Buffered(1))


def _dot(a, b):
    return jnp.dot(a, b, preferred_element_type=F32)


def _dot_nt(a, b):
    return lax.dot_general(a, b, (((1,), (1,)), ((), ())), preferred_element_type=F32)


def _dot_tn(a, b):
    return lax.dot_general(a, b, (((0,), (0,)), ((), ())), preferred_element_type=F32)


def _dot_hi(a, b):
    return jnp.dot(a, b, precision=HIGHEST, preferred_element_type=F32)


def _split_bf16(a):
    hi = a.astype(BF16)
    return hi, (a - hi.astype(F32)).astype(BF16)


def _dot_split_lhs(a, b_bf16):
    hi, lo = _split_bf16(a)
    return _dot(hi, b_bf16) + _dot(lo, b_bf16)


def _rms(x, g):
    return x * lax.rsqrt(jnp.mean(x * x, axis=-1, keepdims=True) + EPS) * g


def _layernorm(x, g, b):
    xc = x - jnp.mean(x, axis=-1, keepdims=True)
    return xc * lax.rsqrt(jnp.mean(xc * xc, axis=-1, keepdims=True) + EPS) * g + b


def _sigmoid(x):
    return 1.0 / (1.0 + jnp.exp(-x))


def _silu(x):
    return x * _sigmoid(x)


def _gelu(x):
    return 0.5 * x * (1.0 + lax.erf(x * (2.0 ** -0.5)))


def _softplus(x):
    return jnp.maximum(x, 0.0) + jnp.log(1.0 + jnp.exp(-jnp.abs(x)))


def _gmlp_prompt_body(x_ref, g_ref, win_ref, bin_ref, lng_ref, lnb_ref, ws_ref, bs_ref, wout_ref,
                      o_ref, v_ref, gated_ref, *, tm, width):
    x = x_ref[...]
    h = _rms(x, g_ref[...]).astype(BF16)
    u = _gelu(_dot(h, win_ref[:, :width]) + bin_ref[:, :width])
    hv = _gelu(_dot(h, win_ref[:, width:]) + bin_ref[:, width:])
    v = _layernorm(hv, lng_ref[...], lnb_ref[...])
    v_ref[0] = v[tm - CHUNK:, :]
    vb = v.astype(BF16)
    gdim = width // GM_GROUPS
    causal = (lax.broadcasted_iota(jnp.int32, (CHUNK, CHUNK), 0)
              >= lax.broadcasted_iota(jnp.int32, (CHUNK, CHUNK), 1))
    for g in range(GM_GROUPS):
        wc = jnp.where(causal, ws_ref[g], 0.0).astype(BF16)
        bias = bs_ref[:, g:g + 1]
        for c in range(tm // CHUNK):
            rows = slice(c * CHUNK, (c + 1) * CHUNK)
            cols = slice(g * gdim, (g + 1) * gdim)
            s = _dot(wc, vb[rows, cols]) + bias
            gated_ref[rows, cols] = (u[rows, cols] * s).astype(BF16)
    o_ref[...] = x + _dot(gated_ref[...], wout_ref[...])


def _gmlp_sample_body(x_ref, g_ref, win_ref, bin_ref, lng_ref, lnb_ref, ws0_ref, bs0_ref, wout_ref,
                      o_ref, v_ref, *, width):
    x = x_ref[...]
    h = _rms(x, g_ref[...]).astype(BF16)
    u = _gelu(_dot(h, win_ref[:, :width]) + bin_ref[:, :width])
    hv = _gelu(_dot(h, win_ref[:, width:]) + bin_ref[:, width:])
    v = _layernorm(hv, lng_ref[...], lnb_ref[...])
    v_ref[...] = v
    s = v * ws0_ref[...] + bs0_ref[...]
    o_ref[...] = x + _dot((u * s).astype(BF16), wout_ref[...])


def _gmlp(X, n_prompt, n_batch, g, w_in, b_in, ln_g, ln_b, w_s, b_s, w_out, *, tm):
    M, D = X.shape
    width = w_out.shape[0]
    seq = n_prompt // n_batch
    tiles_per_seq = seq // tm
    n_sample = M - n_prompt
    g2 = g.reshape(1, D)
    b_in2 = b_in.reshape(1, 2 * width)
    ln_g2, ln_b2 = ln_g.reshape(1, width), ln_b.reshape(1, width)
    w_in_b, w_out_b = w_in.astype(BF16), w_out.astype(BF16)

    X, v_p = pl.pallas_call(
        functools.partial(_gmlp_prompt_body, tm=tm, width=width),
        grid=(n_prompt // tm,),
        in_specs=[
            pl.BlockSpec((tm, D), lambda i: (i, 0)),
            _const_spec((1, D)),
            _const_spec((D, 2 * width)),
            _const_spec((1, 2 * width)),
            _const_spec((1, width)),
            _const_spec((1, width)),
            _const_spec((GM_GROUPS, CHUNK, CHUNK)),
            _const_spec((CHUNK, GM_GROUPS)),
            _const_spec((width, D)),
        ],
        out_specs=[
            pl.BlockSpec((tm, D), lambda i: (i, 0)),
            pl.BlockSpec((1, CHUNK, width), lambda i: (i // tiles_per_seq, 0, 0)),
        ],
        out_shape=[jax.ShapeDtypeStruct((M, D), F32),
                   jax.ShapeDtypeStruct((n_batch, CHUNK, width), F32)],
        scratch_shapes=[pltpu.VMEM((tm, width), BF16)],
        input_output_aliases={0: 0},
        compiler_params=_cparams(("arbitrary",)),
        name="gmlp_prompt",
    )(X, g2, w_in_b, b_in2, ln_g2, ln_b2, w_s, b_s.T, w_out_b)

    gdim = width // GM_GROUPS
    ws0 = jnp.repeat(w_s[:, 0, 0], gdim).reshape(1, width)
    bs0 = jnp.repeat(b_s[:, 0], gdim).reshape(1, width)
    sblk = n_prompt // n_sample
    X, v_s = pl.pallas_call(
        functools.partial(_gmlp_sample_body, width=width),
        grid=(1,),
        in_specs=[
            pl.BlockSpec((n_sample, D), lambda i: (sblk, 0)),
            _const_spec((1, D)),
            _const_spec((D, 2 * width)),
            _const_spec((1, 2 * width)),
            _const_spec((1, width)),
            _const_spec((1, width)),
            _const_spec((1, width)),
            _const_spec((1, width)),
            _const_spec((width, D)),
        ],
        out_specs=[
            pl.BlockSpec((n_sample, D), lambda i: (sblk, 0)),
            pl.BlockSpec((n_sample, width), lambda i: (0, 0)),
        ],
        out_shape=[jax.ShapeDtypeStruct((M, D), F32),
                   jax.ShapeDtypeStruct((n_sample, width), F32)],
        input_output_aliases={0: 0},
        compiler_params=_cparams(("arbitrary",)),
        name="gmlp_sample",
    )(X, g2, w_in_b, b_in2, ln_g2, ln_b2, ws0, bs0, w_out_b)
    return X, v_p, v_s.reshape(n_sample, 1, width)


def _swiglu_tile(h, wg_ref, wu_ref, wd_ref, f_chunk):
    d_ff = wd_ref.shape[0]
    acc = None
    for f in range(d_ff // f_chunk):
        cols = slice(f * f_chunk, (f + 1) * f_chunk)
        a = _dot(h, wg_ref[:, cols].astype(BF16))
        b = _dot(h, wu_ref[:, cols].astype(BF16))
        part = _dot((_silu(a) * b).astype(BF16), wd_ref[cols, :].astype(BF16))
        acc = part if acc is None else acc + part
    return acc


def _ffn_dense_body(x_ref, g_ref, wg_ref, wu_ref, wd_ref, o_ref, *, f_chunk):
    x = x_ref[...]
    h = _rms(x, g_ref[...]).astype(BF16)
    o_ref[...] = x + _swiglu_tile(h, wg_ref, wu_ref, wd_ref, f_chunk)


def _ffn_dense_rows(X, g2, wg, wu, wd, layer, *, tm, first_block, n_tiles, f_chunk):
    M, D = X.shape
    d_ff = wd.shape[1]

    def layer_spec(rows, cols):
        return pl.BlockSpec((None, rows, cols), lambda i: (layer, 0, 0), pipeline_mode=pl.Buffered(1))

    return pl.pallas_call(
        functools.partial(_ffn_dense_body, f_chunk=f_chunk),
        grid=(n_tiles,),
        in_specs=[
            pl.BlockSpec((tm, D), lambda i: (i + first_block, 0)),
            _const_spec((1, D)),
            layer_spec(D, d_ff),
            layer_spec(D, d_ff),
            layer_spec(d_ff, D),
        ],
        out_specs=pl.BlockSpec((tm, D), lambda i: (i + first_block, 0)),
        out_shape=jax.ShapeDtypeStruct((M, D), F32),
        input_output_aliases={0: 0},
        compiler_params=_cparams(("arbitrary",)),
        name="ffn_dense",
    )(X, g2, wg, wu, wd)


def _ffn_dense(X, n_prompt, g, w_gate, w_up, w_down, layer, *, tm, f_chunk):
    M, D = X.shape
    n_sample = M - n_prompt
    g2 = g.reshape(1, D)
    X = _ffn_dense_rows(X, g2, w_gate, w_up, w_down, layer, tm=tm, first_block=0, n_tiles=n_prompt // tm,
                        f_chunk=f_chunk)
    X = _ffn_dense_rows(X, g2, w_gate, w_up, w_down, layer, tm=n_sample, first_block=n_prompt // n_sample,
                        n_tiles=1, f_chunk=f_chunk)
    return X


def _router_body(x_ref, g_ref, wr_ref, idx_ref, wts_ref):
    L = LANES_V7X
    h_hi, h_lo = _split_bf16(_rms(x_ref[...], g_ref[...]))
    hi_part = _dot(h_hi, wr_ref[...])
    logits = hi_part[:, :L] + (hi_part[:, L:] + _dot(h_lo, wr_ref[:, :L]))
    lane = lax.broadcasted_iota(jnp.int32, logits.shape, 1)
    neg = jnp.float32(-jnp.inf)
    logits = jnp.where(lane < N_EXPERTS, logits, neg)
    m1 = jnp.max(logits, axis=-1, keepdims=True)
    i1 = jnp.min(jnp.where(logits == m1, lane, LANES_V7X), axis=-1, keepdims=True)
    rest = jnp.where(lane == i1, neg, logits)
    m2 = jnp.max(rest, axis=-1, keepdims=True)
    i2 = jnp.min(jnp.where(rest == m2, lane, LANES_V7X), axis=-1, keepdims=True)
    e = jnp.exp(m2 - m1)
    w1 = 1.0 / (1.0 + e)
    w2 = e / (1.0 + e)
    idx_ref[...] = jnp.where(lane == 0, i1, jnp.where(lane == 1, i2, 0))
    wts_ref[...] = jnp.where(lane == 0, w1, jnp.where(lane == 1, w2, 0.0))


def _slot_table_body(pos1_ref, pos2_ref, pad_lo_ref, pad_hi_ref, tok_ref, *, n_tokens, n_ranges):
    def pad(s, carry):
        tok_ref[s] = 0
        return carry

    for r in range(n_ranges):
        lax.fori_loop(pad_lo_ref[r], pad_hi_ref[r], pad, 0)

    def place(t, carry):
        tok_ref[pos1_ref[t]] = t
        tok_ref[pos2_ref[t]] = t
        return carry

    lax.fori_loop(0, n_tokens, place, 0, unroll=8)


def _issue_row_gather(src_hbm, idx_ref, base, buf, sem, n_rows):
    def body(r, carry):
        pltpu.make_async_copy(src_hbm.at[pl.ds(idx_ref[base + r], 1), :], buf.at[pl.ds(r, 1), :], sem).start()
        return carry
    lax.fori_loop(0, n_rows, body, 0, unroll=8)


def _wait_row_gather(src_hbm, buf, sem, n_rows):
    pltpu.make_async_copy(src_hbm.at[pl.ds(0, n_rows), :], buf, sem).wait()


def _moe_ffn_body(tile_expert_ref, n_tiles_ref, tok_ref, x_hbm, g_ref, wg_ref, wu_ref, wd_ref,
                  y_ref, buf_ref, sem_ref, *, tm, f_chunk):
    t = pl.program_id(0)
    n_tiles = n_tiles_ref[0]
    slot = t % 2

    @pl.when(t == 0)
    def _():
        _issue_row_gather(x_hbm, tok_ref, 0, buf_ref.at[0], sem_ref.at[0], tm)

    @pl.when(t + 1 < n_tiles)
    def _():
        _issue_row_gather(x_hbm, tok_ref, (t + 1) * tm, buf_ref.at[1 - slot], sem_ref.at[1 - slot], tm)

    @pl.when(t < n_tiles)
    def _():
        _wait_row_gather(x_hbm, buf_ref.at[slot], sem_ref.at[slot], tm)
        h = _rms(buf_ref[slot], g_ref[...]).astype(BF16)
        y_ref[...] = _swiglu_tile(h, wg_ref, wu_ref, wd_ref, f_chunk)

    @pl.when(t >= n_tiles)
    def _():
        y_ref[...] = jnp.zeros_like(y_ref)


def _moe_combine_body(p1_ref, p2_ref, x_ref, wts_ref, y_hbm, gf_ref, o_ref, buf1_ref, buf2_ref, sem_ref,
                      *, tm, row0, n_steps, final_norm):
    i = pl.program_id(0)
    slot = i % 2

    def issue(step, s):
        base = row0 + step * tm

        def body(r, carry):
            for p_ref, buf, k in ((p1_ref, buf1_ref, 0), (p2_ref, buf2_ref, 1)):
                pltpu.make_async_copy(y_hbm.at[pl.ds(p_ref[base + r], 1), :], buf.at[s, pl.ds(r, 1), :],
                                      sem_ref.at[k, s]).start()
            return carry
        lax.fori_loop(0, tm, body, 0, unroll=8)

    @pl.when(i == 0)
    def _():
        issue(0, 0)

    @pl.when(i + 1 < n_steps)
    def _():
        issue(i + 1, 1 - slot)

    _wait_row_gather(y_hbm, buf1_ref.at[slot], sem_ref.at[0, slot], tm)
    _wait_row_gather(y_hbm, buf2_ref.at[slot], sem_ref.at[1, slot], tm)
    out = x_ref[...] + (wts_ref[:, 0:1] * buf1_ref[slot] + wts_ref[:, 1:2] * buf2_ref[slot])
    if final_norm:
        out = _rms(out, gf_ref[...])
    o_ref[...] = out


def _moe(X, n_prompt, g, w_router, w_gate, w_up, w_down, layer, g_final, *, tm_tok, tm_out, tm, f_chunk,
         final_norm):
    M, D = X.shape
    E = N_EXPERTS
    d_ff = w_down.shape[2]
    g2 = g.reshape(1, D)
    L = LANES_V7X
    wr = jnp.zeros((D, L), F32).at[:, :E].set(w_router)
    wr_hi = wr.astype(BF16)
    wr_split = jnp.concatenate([wr_hi, (wr - wr_hi.astype(F32)).astype(BF16)], axis=1)

    idx, wts = pl.pallas_call(
        _router_body,
        grid=(M // tm_tok,),
        in_specs=[pl.BlockSpec((tm_tok, D), lambda i: (i, 0)), _const_spec((1, D)), _const_spec((D, 2 * L))],
        out_specs=[pl.BlockSpec((tm_tok, L), lambda i: (i, 0)), pl.BlockSpec((tm_tok, L), lambda i: (i, 0))],
        out_shape=[jax.ShapeDtypeStruct((M, L), jnp.int32), jax.ShapeDtypeStruct((M, L), F32)],
        compiler_params=_cparams(("arbitrary",)),
        name="moe_router",
    )(X, g2, wr_split)

    i1, i2 = idx[:, 0], idx[:, 1]
    eids = jnp.arange(E, dtype=jnp.int32)
    sel = ((i1[:, None] == eids) | (i2[:, None] == eids)).astype(jnp.int32)
    count = jnp.sum(sel, axis=0)
    tiles_e = (count + tm - 1) // tm
    tiles_end = jnp.cumsum(tiles_e)
    n_tiles = tiles_end[-1]
    start_e = (tiles_end - tiles_e) * tm
    pos = start_e[None, :] + jnp.cumsum(sel, axis=0) - sel
    pos1 = jnp.sum(jnp.where(i1[:, None] == eids, pos, 0), axis=1).astype(jnp.int32)
    pos2 = jnp.sum(jnp.where(i2[:, None] == eids, pos, 0), axis=1).astype(jnp.int32)
    max_tiles = (2 * M + E * (tm - 1)) // tm
    n_slots = max_tiles * tm
    pad_lo = jnp.concatenate([start_e + count, (n_tiles * tm)[None]]).astype(jnp.int32)
    pad_hi = jnp.concatenate([tiles_end * tm, jnp.array([n_slots], jnp.int32)]).astype(jnp.int32)
    tile_ids = jnp.arange(max_tiles, dtype=jnp.int32)
    tile_expert = jnp.sum((jnp.minimum(tile_ids, n_tiles - 1)[:, None] >= tiles_end[None, :]).astype(jnp.int32),
                          axis=1)
    tile_expert = jnp.minimum(tile_expert, E - 1).astype(jnp.int32)
    n_tiles_arr = n_tiles.astype(jnp.int32).reshape(1)

    smem = pl.BlockSpec(memory_space=pltpu.SMEM)
    tok_of_slot = pl.pallas_call(
        functools.partial(_slot_table_body, n_tokens=M, n_ranges=E + 1),
        in_specs=[smem, smem, smem, smem],
        out_specs=smem,
        out_shape=jax.ShapeDtypeStruct((n_slots,), jnp.int32),
        name="moe_slot_table",
    )(pos1, pos2, pad_lo, pad_hi)

    def expert_spec(rows, cols, buffers=1):
        return pl.BlockSpec((None, None, rows, cols), lambda t, te, nt, tok: (layer, te[t], 0, 0),
                            pipeline_mode=pl.Buffered(buffers))

    y_sorted = pl.pallas_call(
        functools.partial(_moe_ffn_body, tm=tm, f_chunk=f_chunk),
        grid_spec=pltpu.PrefetchScalarGridSpec(
            num_scalar_prefetch=3,
            grid=(max_tiles,),
            in_specs=[
                pl.BlockSpec(memory_space=pl.ANY),
                pl.BlockSpec((1, D), lambda t, te, nt, tok: (0, 0)),
                expert_spec(D, d_ff),
                expert_spec(D, d_ff),
                expert_spec(d_ff, D, buffers=2),
            ],
            out_specs=pl.BlockSpec((tm, D), lambda t, te, nt, tok: (t, 0)),
            scratch_shapes=[pltpu.VMEM((2, tm, D), F32), pltpu.SemaphoreType.DMA((2,))],
        ),
        out_shape=jax.ShapeDtypeStruct((n_slots, D), F32),
        compiler_params=_cparams(("arbitrary",)),
        name="moe_ffn",
    )(tile_expert, n_tiles_arr, tok_of_slot, X, g2, w_gate, w_up, w_down)

    def combine_rows(tm_c, first_block, n_steps, in_place):
        out_first = first_block if in_place else 0
        return pl.pallas_call(
            functools.partial(_moe_combine_body, tm=tm_c, row0=first_block * tm_c, n_steps=n_steps,
                              final_norm=final_norm),
            grid_spec=pltpu.PrefetchScalarGridSpec(
                num_scalar_prefetch=2,
                grid=(n_steps,),
                in_specs=[
                    pl.BlockSpec((tm_c, D), lambda i, p1, p2: (i + first_block, 0)),
                    pl.BlockSpec((tm_c, L), lambda i, p1, p2: (i + first_block, 0)),
                    pl.BlockSpec(memory_space=pl.ANY),
                    pl.BlockSpec((1, D), lambda i, p1, p2: (0, 0)),
                ],
                out_specs=pl.BlockSpec((tm_c, D), lambda i, p1, p2: (i + out_first, 0)),
                scratch_shapes=[pltpu.VMEM((2, tm_c, D), F32), pltpu.VMEM((2, tm_c, D), F32),
                                pltpu.SemaphoreType.DMA((2, 2))],
            ),
            out_shape=jax.ShapeDtypeStruct((M if in_place else n_steps * tm_c, D), F32),
            input_output_aliases={2: 0} if in_place else {},
            compiler_params=_cparams(("arbitrary",)),
            name="moe_combine",
        )(pos1, pos2, X, wts, y_sorted, g_final.reshape(1, D))

    if not final_norm:
        return combine_rows(tm_tok, 0, M // tm_tok, True)
    n_sample = M - n_prompt
    y_p = combine_rows(tm_out, 0, n_prompt // tm_out, False)
    y_s = combine_rows(n_sample, n_prompt // n_sample, 1, False)
    return y_p, y_s


def _ssd_post(y, xs, z, dsk, ng, wout_ref, ynorm_ref, inner):
    y = (y + dsk * xs) * _silu(z)
    gw = inner // SSM_GROUPS
    for g in range(SSM_GROUPS):
        cols = slice(g * gw, (g + 1) * gw)
        yg = y[:, cols]
        yg = yg * lax.rsqrt(jnp.mean(yg * yg, axis=-1, keepdims=True) + EPS) * ng[:, cols]
        ynorm_ref[:, cols] = yg.astype(BF16)
    return _dot(ynorm_ref[...], wout_ref[...])


def _mamba_prompt_body(x_ref, g_ref, win_ref, wdt_ref, cw_ref, cb_ref, dtb_ref, a_ref, dsk_ref, ng_ref,
                       wout_ref, r_ref, o_ref, conv_ref, ssm_ref, ctx_ref, s_ref, y_ref, ynorm_ref,
                       *, tm, tiles_per_seq, inner, bc):
    i = pl.program_id(0)

    @pl.when(i % tiles_per_seq == 0)
    def _():
        ctx_ref[0:SSM_PAD, :] = jnp.zeros((SSM_PAD, ctx_ref.shape[1]), F32)
        s_ref[...] = jnp.zeros_like(s_ref)

    x = x_ref[...]
    h = _rms(x, g_ref[...]).astype(BF16)
    proj = _dot(h, win_ref[...])
    z = proj[:, :inner]
    xbc = proj[:, inner:]
    ctx_ref[SSM_PAD:SSM_PAD + tm, :] = xbc
    conv_ref[0] = ctx_ref[SSM_PAD + tm - (SSM_CONV - 1):SSM_PAD + tm, :]
    conv = cb_ref[...] + xbc * cw_ref[SSM_CONV - 1:SSM_CONV, :]
    for k in range(SSM_CONV - 1):
        off = SSM_PAD - (SSM_CONV - 1) + k
        conv = conv + ctx_ref[off:off + tm, :] * cw_ref[k:k + 1, :]
    ctx_ref[0:SSM_PAD, :] = ctx_ref[tm:tm + SSM_PAD, :]
    act = _silu(conv)
    xs = act[:, :inner]
    bm = act[:, inner:inner + bc].astype(BF16)
    cm = act[:, inner + bc:].astype(BF16)
    dt = _softplus(_dot(h, wdt_ref[...]) + dtb_ref[...])
    da = dt * a_ref[...]

    rows_i = lax.broadcasted_iota(jnp.int32, (CHUNK, CHUNK), 0)
    cols_i = lax.broadcasted_iota(jnp.int32, (CHUNK, CHUNK), 1)
    causal = rows_i >= cols_i
    tril = causal.astype(F32)
    gw = inner // SSM_GROUPS
    hpg = gw // SSM_HEAD_DIM
    lane_head = lax.broadcasted_iota(jnp.int32, (CHUNK, gw), 1) // SSM_HEAD_DIM
    r_mat = r_ref[...]

    for c in range(tm // CHUNK):
        rows = slice(c * CHUNK, (c + 1) * CHUNK)
        dt_c = dt[rows, :]
        cum = _dot_hi(tril, da[rows, :])
        cum_t = cum.T
        dt_t = dt_c.T
        ecum = jnp.exp(cum)
        e_full = _dot_split_lhs(ecum, r_mat)
        w_in = jnp.exp(cum[CHUNK - 1:CHUNK, :] - cum) * dt_c
        xs_c = xs[rows, :]
        xw = (xs_c * _dot_split_lhs(w_in, r_mat)).astype(BF16)
        xs_b = xs_c.astype(BF16)
        for g in range(SSM_GROUPS):
            gcols = slice(g * gw, (g + 1) * gw)
            ncols = slice(g * SSM_STATE, (g + 1) * SSM_STATE)
            b_g = bm[rows, ncols]
            c_g = cm[rows, ncols]
            cb = _dot_nt(c_g, b_g)
            x_g = xs_b[:, gcols]
            s_g = s_ref[g]
            wts, x_heads = [], []
            for r in range(hpg):
                hd = g * hpg + r
                seg = cum[:, hd:hd + 1] - cum_t[hd:hd + 1, :]
                decay = jnp.where(causal, jnp.exp(jnp.where(causal, seg, 0.0)), 0.0)
                wts.append((cb * decay * dt_t[hd:hd + 1, :]).astype(BF16))
                x_heads.append(jnp.where(lane_head == r, x_g, jnp.zeros_like(x_g)))
            y_intra = _dot(jnp.concatenate(wts, axis=1), jnp.concatenate(x_heads, axis=0))
            y_ref[rows, gcols] = y_intra + _dot_nt(c_g, s_g.astype(BF16)) * e_full[:, gcols]
            upd = _dot_tn(xw[:, gcols], b_g)
            for r in range(hpg):
                hd = g * hpg + r
                hrows = slice(r * SSM_HEAD_DIM, (r + 1) * SSM_HEAD_DIM)
                s_ref[g, hrows, :] = s_g[hrows, :] * ecum[CHUNK - 1:CHUNK, hd:hd + 1] + upd[hrows, :]

    out = _ssd_post(y_ref[...], xs, z, dsk_ref[...], ng_ref[...], wout_ref, ynorm_ref, inner)
    o_ref[...] = x + out
    ssm_ref[0] = s_ref[...]


def _mamba_sample_pre_body(x_ref, g_ref, win_ref, wdt_ref, cw_ref, cb_ref, dtb_ref, a_ref, r_ref, cs_ref,
                           z_ref, xs_ref, b_ref, c_ref, xdt_ref, e_ref, newconv_ref, *, inner, bc):
    h = _rms(x_ref[...], g_ref[...]).astype(BF16)
    proj = _dot(h, win_ref[...])
    z_ref[...] = proj[:, :inner]
    xbc = proj[:, inner:]
    conv = cb_ref[...] + xbc * cw_ref[SSM_CONV - 1:SSM_CONV, :]
    for k in range(SSM_CONV - 1):
        conv = conv + cs_ref[k] * cw_ref[k:k + 1, :]
    for k in range(SSM_CONV - 2):
        newconv_ref[k] = cs_ref[k + 1]
    newconv_ref[SSM_CONV - 2] = xbc
    act = _silu(conv)
    xs = act[:, :inner]
    xs_ref[...] = xs
    b_ref[...] = act[:, inner:inner + bc]
    c_ref[...] = act[:, inner + bc:]
    dt = _softplus(_dot(h, wdt_ref[...]) + dtb_ref[...])
    xdt_ref[...] = xs * _dot_split_lhs(dt, r_ref[...])
    e_ref[...] = jnp.exp(dt * a_ref[...])


def _mamba_sample_state_body(e_ref, h0_ref, xdt_t_ref, b_ref, c_ref, hn_ref, y_ref, *, bt, inner, heads):
    i = pl.program_id(0)
    gw = inner // SSM_GROUPS
    hpg = gw // SSM_HEAD_DIM
    row_id = lax.broadcasted_iota(jnp.int32, (bt, bt * SSM_STATE), 0)
    lane_tok = lax.broadcasted_iota(jnp.int32, (bt, bt * SSM_STATE), 1) // SSM_STATE
    for g in range(SSM_GROUPS):
        rows = slice(g * gw, (g + 1) * gw)
        ncols = slice(g * SSM_STATE, (g + 1) * SSM_STATE)
        states = []
        for j in range(bt):
            b_row = b_ref[0, j:j + 1, ncols]
            for r in range(hpg):
                hrows = slice(g * gw + r * SSM_HEAD_DIM, g * gw + (r + 1) * SSM_HEAD_DIM)
                e = e_ref[(i * bt + j) * heads + g * hpg + r]
                hn_ref[j, hrows, :] = h0_ref[j, hrows, :] * e + xdt_t_ref[0, hrows, j:j + 1] * b_row
            states.append(hn_ref[j, rows, :].astype(BF16))
        c_g = c_ref[0, :, ncols]
        c_diag = jnp.where(row_id == lane_tok, jnp.concatenate([c_g] * bt, axis=1), 0.0).astype(BF16)
        y_ref[:, rows] = _dot_nt(c_diag, jnp.concatenate(states, axis=1))


def _mamba_sample_post_body(x_ref, y_ref, xs_ref, z_ref, dsk_ref, ng_ref, wout_ref, o_ref, ynorm_ref, *, inner):
    out = _ssd_post(y_ref[...], xs_ref[...], z_ref[...], dsk_ref[...], ng_ref[...], wout_ref, ynorm_ref, inner)
    o_ref[...] = x_ref[...] + out


def _mamba(X, n_prompt, n_batch, state_ssm_all, layer, state_conv, g, w_in, conv_w, conv_b, dt_bias, a_log, d_skip,
           norm_g, w_out, *, tm, bt):
    M, D = X.shape
    inner = w_out.shape[0]
    heads = a_log.shape[0]
    conv_dim = conv_w.shape[1]
    bc = (conv_dim - inner) // 2
    n_sample = M - n_prompt
    seq = n_prompt // n_batch
    tiles_per_seq = seq // tm
    gw = inner // SSM_GROUPS
    L = LANES_V7X

    g2 = g.reshape(1, D)
    w_main = w_in[:, :inner + conv_dim].astype(BF16)
    w_dt = jnp.zeros((D, L), F32).at[:, :heads].set(w_in[:, inner + conv_dim:]).astype(BF16)
    cb2 = conv_b.reshape(1, conv_dim)
    dtb = jnp.zeros((1, L), F32).at[0, :heads].set(dt_bias)
    a_neg = jnp.zeros((1, L), F32).at[0, :heads].set(-jnp.exp(a_log))
    dsk = jnp.repeat(d_skip, SSM_HEAD_DIM).reshape(1, inner)
    ng2 = norm_g.reshape(1, inner)
    w_out_b = w_out.astype(BF16)
    r_mat = (jnp.arange(L, dtype=jnp.int32)[:, None]
             == (jnp.arange(inner, dtype=jnp.int32) // SSM_HEAD_DIM)[None, :]).astype(BF16)

    X, conv_p, ssm_p = pl.pallas_call(
        functools.partial(_mamba_prompt_body, tm=tm, tiles_per_seq=tiles_per_seq, inner=inner, bc=bc),
        grid=(n_prompt // tm,),
        in_specs=[
            pl.BlockSpec((tm, D), lambda i: (i, 0)),
            _const_spec((1, D)),
            _const_spec((D, inner + conv_dim)),
            _const_spec((D, L)),
            _const_spec((SSM_CONV, conv_dim)),
            _const_spec((1, conv_dim)),
            _const_spec((1, L)),
            _const_spec((1, L)),
            _const_spec((1, inner)),
            _const_spec((1, inner)),
            _const_spec((inner, D)),
            _const_spec((L, inner)),
        ],
        out_specs=[
            pl.BlockSpec((tm, D), lambda i: (i, 0)),
            pl.BlockSpec((1, SSM_CONV - 1, conv_dim), lambda i: (i // tiles_per_seq, 0, 0)),
            pl.BlockSpec((1, SSM_GROUPS, gw, SSM_STATE), lambda i: (i // tiles_per_seq, 0, 0, 0)),
        ],
        out_shape=[
            jax.ShapeDtypeStruct((M, D), F32),
            jax.ShapeDtypeStruct((n_batch, SSM_CONV - 1, conv_dim), F32),
            jax.ShapeDtypeStruct((n_batch, SSM_GROUPS, gw, SSM_STATE), F32),
        ],
        scratch_shapes=[
            pltpu.VMEM((SSM_PAD + tm, conv_dim), F32),
            pltpu.VMEM((SSM_GROUPS, gw, SSM_STATE), F32),
            pltpu.VMEM((tm, inner), F32),
            pltpu.VMEM((tm, inner), BF16),
        ],
        input_output_aliases={0: 0},
        compiler_params=_cparams(("arbitrary",)),
        name="mamba_prompt",
    )(X, g2, w_main, w_dt, conv_w, cb2, dtb, a_neg, dsk, ng2, w_out_b, r_mat)

    sblk = n_prompt // n_sample
    cs_t = jnp.transpose(state_conv, (1, 0, 2))
    z, xs, b_m, c_m, xdt, e_tok, newconv_t = pl.pallas_call(
        functools.partial(_mamba_sample_pre_body, inner=inner, bc=bc),
        grid=(1,),
        in_specs=[
            pl.BlockSpec((n_sample, D), lambda i: (sblk, 0)),
            _const_spec((1, D)),
            _const_spec((D, inner + conv_dim)),
            _const_spec((D, L)),
            _const_spec((SSM_CONV, conv_dim)),
            _const_spec((1, conv_dim)),
            _const_spec((1, L)),
            _const_spec((1, L)),
            _const_spec((L, inner)),
            _const_spec((SSM_CONV - 1, n_sample, conv_dim)),
        ],
        out_specs=[
            pl.BlockSpec((n_sample, inner), lambda i: (0, 0)),
            pl.BlockSpec((n_sample, inner), lambda i: (0, 0)),
            pl.BlockSpec((n_sample, bc), lambda i: (0, 0)),
            pl.BlockSpec((n_sample, bc), lambda i: (0, 0)),
            pl.BlockSpec((n_sample, inner), lambda i: (0, 0)),
            pl.BlockSpec((n_sample, L), lambda i: (0, 0)),
            pl.BlockSpec((SSM_CONV - 1, n_sample, conv_dim), lambda i: (0, 0, 0)),
        ],
        out_shape=[
            jax.ShapeDtypeStruct((n_sample, inner), F32),
            jax.ShapeDtypeStruct((n_sample, inner), F32),
            jax.ShapeDtypeStruct((n_sample, bc), F32),
            jax.ShapeDtypeStruct((n_sample, bc), F32),
            jax.ShapeDtypeStruct((n_sample, inner), F32),
            jax.ShapeDtypeStruct((n_sample, L), F32),
            jax.ShapeDtypeStruct((SSM_CONV - 1, n_sample, conv_dim), F32),
        ],
        compiler_params=_cparams(("arbitrary",)),
        name="mamba_sample_pre",
    )(X, g2, w_main, w_dt, conv_w, cb2, dtb, a_neg, r_mat, cs_t)
    conv_s = jnp.transpose(newconv_t, (1, 0, 2))

    nblk = n_sample // bt

    def to_cols(arr):
        return jnp.transpose(arr.reshape(nblk, bt, inner), (0, 2, 1))

    h0 = state_ssm_all.reshape(state_ssm_all.shape[0], n_sample, inner, SSM_STATE)
    h_new, y_s = pl.pallas_call(
        functools.partial(_mamba_sample_state_body, bt=bt, inner=inner, heads=heads),
        grid_spec=pltpu.PrefetchScalarGridSpec(
            num_scalar_prefetch=1,
            grid=(nblk,),
            in_specs=[
                pl.BlockSpec((None, bt, inner, SSM_STATE), lambda i, e: (layer, i, 0, 0)),
                pl.BlockSpec((1, inner, bt), lambda i, e: (i, 0, 0)),
                pl.BlockSpec((1, bt, bc), lambda i, e: (i, 0, 0)),
                pl.BlockSpec((1, bt, bc), lambda i, e: (i, 0, 0)),
            ],
            out_specs=[
                pl.BlockSpec((bt, inner, SSM_STATE), lambda i, e: (i, 0, 0)),
                pl.BlockSpec((bt, inner), lambda i, e: (i, 0)),
            ],
        ),
        out_shape=[
            jax.ShapeDtypeStruct((n_sample, inner, SSM_STATE), F32),
            jax.ShapeDtypeStruct((n_sample, inner), F32),
        ],
        compiler_params=_cparams(("arbitrary",)),
        name="mamba_sample_state",
    )(e_tok[:, :heads].reshape(n_sample * heads), h0, to_cols(xdt), b_m.reshape(nblk, bt, bc),
      c_m.reshape(nblk, bt, bc))

    X = pl.pallas_call(
        functools.partial(_mamba_sample_post_body, inner=inner),
        grid=(1,),
        in_specs=[
            pl.BlockSpec((n_sample, D), lambda i: (sblk, 0)),
            pl.BlockSpec((n_sample, inner), lambda i: (0, 0)),
            pl.BlockSpec((n_sample, inner), lambda i: (0, 0)),
            pl.BlockSpec((n_sample, inner), lambda i: (0, 0)),
            _const_spec((1, inner)),
            _const_spec((1, inner)),
            _const_spec((inner, D)),
        ],
        out_specs=pl.BlockSpec((n_sample, D), lambda i: (sblk, 0)),
        out_shape=jax.ShapeDtypeStruct((M, D), F32),
        scratch_shapes=[pltpu.VMEM((n_sample, inner), BF16)],
        input_output_aliases={0: 0},
        compiler_params=_cparams(("arbitrary",)),
        name="mamba_sample_post",
    )(X, y_s, xs, z, dsk, ng2, w_out_b)

    ssm_p = ssm_p.reshape(n_batch, heads, SSM_HEAD_DIM, SSM_STATE)
    ssm_s = h_new.reshape(n_sample, heads, SSM_HEAD_DIM, SSM_STATE)
    return X, conv_p, conv_s, ssm_p, ssm_s


def _causal_dwconv_tile(ctx_ref, dw_ref, acc_ref, tm, d):
    S = SUBLANES_V7X
    first = CFM_PAD - (CFM_KERNEL - 1)
    rb, lb = CONV_ROW_BLOCK, CONV_LANE_BLOCK
    for r0 in range(0, tm, rb):
        for l0 in range(0, d, lb):
            lanes = slice(l0, l0 + lb)
            y = None
            for b in range(S):
                pb = None
                for a in range((first + CFM_KERNEL - 1) // S + 1):
                    k = S * a + b - first
                    if 0 <= k < CFM_KERNEL:
                        term = ctx_ref[r0 + S * a:r0 + S * a + rb + S, lanes] * dw_ref[k:k + 1, lanes]
                        pb = term if pb is None else pb + term
                part = pb[b:b + rb, :]
                y = part if y is None else y + part
            acc_ref[r0:r0 + rb, lanes] = y


def _cfm_prompt_body(x_ref, g_ref, w1_ref, b1_ref, dw_ref, dwb_ref, lng_ref, lnb_ref, w2_ref, b2_ref,
                     o_ref, buf_ref, ctx_ref, acc_ref, *, tm, tiles_per_seq, d):
    i = pl.program_id(0)

    @pl.when(i % tiles_per_seq == 0)
    def _():
        ctx_ref[0:CFM_PAD, :] = jnp.zeros((CFM_PAD, d), F32)
        ctx_ref[CFM_PAD + tm:CFM_PAD + tm + SUBLANES_V7X, :] = jnp.zeros((SUBLANES_V7X, d), F32)

    x = x_ref[...]
    h = _rms(x, g_ref[...]).astype(BF16)
    a = _dot(h, w1_ref[...]) + b1_ref[...]
    ctx_ref[CFM_PAD:CFM_PAD + tm, :] = a[:, :d] * _sigmoid(a[:, d:])
    _causal_dwconv_tile(ctx_ref, dw_ref, acc_ref, tm, d)
    buf_ref[0] = ctx_ref[CFM_PAD + tm - (CFM_KERNEL - 1):CFM_PAD + tm, :]
    ctx_ref[0:CFM_PAD, :] = ctx_ref[tm:tm + CFM_PAD, :]
    acc = acc_ref[...] + dwb_ref[...]
    hc = _silu(_layernorm(acc, lng_ref[...], lnb_ref[...])).astype(BF16)
    o_ref[...] = x + _dot(hc, w2_ref[...]) + b2_ref[...]


def _cfm_sample_body(x_ref, g_ref, w1_ref, b1_ref, dw_ref, dwb_ref, lng_ref, lnb_ref, w2_ref, b2_ref, cs_ref,
                     o_ref, new_ref, *, d):
    x = x_ref[...]
    h = _rms(x, g_ref[...]).astype(BF16)
    a = _dot(h, w1_ref[...]) + b1_ref[...]
    glu = a[:, :d] * _sigmoid(a[:, d:])
    acc = dwb_ref[...] + glu * dw_ref[CFM_KERNEL - 1:CFM_KERNEL, :]
    for k in range(CFM_KERNEL - 1):
        acc = acc + cs_ref[k] * dw_ref[k:k + 1, :]
    for k in range(CFM_KERNEL - 2):
        new_ref[k] = cs_ref[k + 1]
    new_ref[CFM_KERNEL - 2] = glu
    hc = _silu(_layernorm(acc, lng_ref[...], lnb_ref[...])).astype(BF16)
    o_ref[...] = x + _dot(hc, w2_ref[...]) + b2_ref[...]


def _conformer(X, n_prompt, n_batch, state_conv, g, w_pw1, b_pw1, dw_w, dw_b, ln_g, ln_b, w_pw2, b_pw2,
               *, tm, bt):
    M, D = X.shape
    n_sample = M - n_prompt
    seq = n_prompt // n_batch
    tiles_per_seq = seq // tm
    K = CFM_KERNEL
    g2 = g.reshape(1, D)
    w1, w2 = w_pw1.astype(BF16), w_pw2.astype(BF16)
    b1, b2 = b_pw1.reshape(1, 2 * D), b_pw2.reshape(1, D)
    dwb, lng, lnb = dw_b.reshape(1, D), ln_g.reshape(1, D), ln_b.reshape(1, D)
    weight_specs = [
        _const_spec((1, D)), _const_spec((D, 2 * D)), _const_spec((1, 2 * D)), _const_spec((K, D)),
        _const_spec((1, D)), _const_spec((1, D)), _const_spec((1, D)), _const_spec((D, D)), _const_spec((1, D)),
    ]
    weights = (g2, w1, b1, dw_w, dwb, lng, lnb, w2, b2)

    X, buf_p = pl.pallas_call(
        functools.partial(_cfm_prompt_body, tm=tm, tiles_per_seq=tiles_per_seq, d=D),
        grid=(n_prompt // tm,),
        in_specs=[pl.BlockSpec((tm, D), lambda i: (i, 0))] + weight_specs,
        out_specs=[
            pl.BlockSpec((tm, D), lambda i: (i, 0)),
            pl.BlockSpec((1, K - 1, D), lambda i: (i // tiles_per_seq, 0, 0)),
        ],
        out_shape=[jax.ShapeDtypeStruct((M, D), F32), jax.ShapeDtypeStruct((n_batch, K - 1, D), F32)],
        scratch_shapes=[pltpu.VMEM((CFM_PAD + tm + SUBLANES_V7X, D), F32), pltpu.VMEM((tm, D), F32)],
        input_output_aliases={0: 0},
        compiler_params=_cparams(("arbitrary",)),
        name="conformer_prompt",
    )(X, *weights)

    first = n_prompt // bt
    cs_t = jnp.transpose(state_conv, (1, 0, 2))
    X, new_t = pl.pallas_call(
        functools.partial(_cfm_sample_body, d=D),
        grid=(n_sample // bt,),
        in_specs=[pl.BlockSpec((bt, D), lambda i: (i + first, 0))] + weight_specs
        + [pl.BlockSpec((K - 1, bt, D), lambda i: (0, i, 0))],
        out_specs=[
            pl.BlockSpec((bt, D), lambda i: (i + first, 0)),
            pl.BlockSpec((K - 1, bt, D), lambda i: (0, i, 0)),
        ],
        out_shape=[jax.ShapeDtypeStruct((M, D), F32), jax.ShapeDtypeStruct((K - 1, n_sample, D), F32)],
        input_output_aliases={0: 0},
        compiler_params=_cparams(("arbitrary",)),
        name="conformer_sample",
    )(X, *weights, cs_t)
    return X, buf_p, jnp.transpose(new_t, (1, 0, 2))


def _stack(parts):
    return parts[0][None] if len(parts) == 1 else jnp.stack(parts)


def kernel(x_prompt, x_sample, state_ssm, state_conv_ssm, state_conv_cfm, norm_mix_g, norm_ffn_g, norm_final_g, a_w_in, a_b_in, a_ln_g, a_ln_b, a_w_s, a_b_s, a_w_out, b_w_in, b_conv_w, b_conv_b, b_dt_bias, b_a_log, b_d, b_norm_g, b_w_out, c_w_pw1, c_b_pw1, c_dw_w, c_dw_b, c_ln_g, c_ln_b, c_w_pw2, c_b_pw2, f_w_gate, f_w_up, f_w_down, e_w_router, e_w_gate, e_w_up, e_w_down):
    n_batch, seq, D = x_prompt.shape
    n_sample = x_sample.shape[0]
    n_prompt = n_batch * seq
    depth = norm_mix_g.shape[0]
    d_ff = f_w_gate.shape[2]
    assert x_sample.shape[1] == 1 and n_prompt % n_sample == 0 and seq % CHUNK == 0

    X = jnp.concatenate([x_prompt.reshape(n_prompt, D), x_sample.reshape(n_sample, D)], axis=0)
    v_p, v_s, ssm_p, ssm_s, cs_p, cs_s, cc_p, cc_s = [], [], [], [], [], [], [], []
    for i in range(depth):
        kind, j = i % 3, i // 3
        if kind == 0:
            X, vp, vs = _gmlp(X, n_prompt, n_batch, norm_mix_g[i], a_w_in[j], a_b_in[j], a_ln_g[j], a_ln_b[j],
                              a_w_s[j], a_b_s[j], a_w_out[j], tm=512)
            v_p.append(vp)
            v_s.append(vs)
        elif kind == 1:
            X, cbp, cbs, hlp, hls = _mamba(X, n_prompt, n_batch, state_ssm, j, state_conv_ssm[j], norm_mix_g[i],
                                           b_w_in[j], b_conv_w[j], b_conv_b[j], b_dt_bias[j], b_a_log[j], b_d[j],
                                           b_norm_g[j], b_w_out[j], tm=256, bt=8)
            cs_p.append(cbp)
            cs_s.append(cbs)
            ssm_p.append(hlp)
            ssm_s.append(hls)
        else:
            X, cbp, cbs = _conformer(X, n_prompt, n_batch, state_conv_cfm[j], norm_mix_g[i], c_w_pw1[j], c_b_pw1[j],
                                     c_dw_w[j], c_dw_b[j], c_ln_g[j], c_ln_b[j], c_w_pw2[j], c_b_pw2[j],
                                     tm=512, bt=32)
            cc_p.append(cbp)
            cc_s.append(cbs)
        k = i // 2
        if i % 2 == 0:
            X = _ffn_dense(X, n_prompt, norm_ffn_g[i], f_w_gate, f_w_up, f_w_down, k,
                           tm=512, f_chunk=FFN_CHUNK)
        else:
            X = _moe(X, n_prompt, norm_ffn_g[i], e_w_router[k], e_w_gate, e_w_up, e_w_down, k, norm_final_g,
                     tm_tok=384, tm_out=512, tm=512, f_chunk=FFN_CHUNK, final_norm=(i == depth - 1))
    assert depth % 2 == 0
    y_prompt, y_sample = X
    y_prompt = y_prompt.reshape(n_batch, seq, D)
    y_sample = y_sample.reshape(n_sample, 1, D)
    return (y_prompt, y_sample, _stack(v_p), _stack(v_s), _stack(ssm_p), _stack(ssm_s),
            _stack(cs_p), _stack(cs_s), _stack(cc_p), _stack(cc_s))
```

```python
import functools

import jax
import jax.numpy as jnp
from jax import lax
from jax.experimental import pallas as pl
from jax.experimental.pallas import tpu as pltpu

F32 = jnp.float32
BF16 = jnp.bfloat16
EPS = 1e-6
HIGHEST = lax.Precision.HIGHEST

LANES_V7X = 128
SUBLANES_V7X = 8
VMEM_LIMIT_V7X = 60 * 1024 * 1024

CHUNK = 128
GM_GROUPS = 8
SSM_GROUPS = 8
SSM_HEAD_DIM = 64
SSM_STATE = 128
SSM_CONV = 4
CFM_KERNEL = 31
N_EXPERTS = 8
CFM_PAD = 32
SSM_PAD = 8
FFN_CHUNK = 256
CONV_ROW_BLOCK = 128
CONV_LANE_BLOCK = 128


def _cparams(sem):
    return pltpu.CompilerParams(dimension_semantics=sem, vmem_limit_bytes=VMEM_LIMIT_V7X)


def _const_spec(shape):
    nd = len(shape)
    return pl.BlockSpec(shape, lambda *_: (0,) * nd, pipeline_mode=pl.Buffered(1))


def _dot(a, b):
    return jnp.dot(a, b, preferred_element_type=F32)


def _dot_nt(a, b):
    return lax.dot_general(a, b, (((1,), (1,)), ((), ())), preferred_element_type=F32)


def _dot_tn(a, b):
    return lax.dot_general(a, b, (((0,), (0,)), ((), ())), preferred_element_type=F32)


def _dot_hi(a, b):
    return jnp.dot(a, b, precision=HIGHEST, preferred_element_type=F32)


def _split_bf16(a):
    hi = a.astype(BF16)
    return hi, (a - hi.astype(F32)).astype(BF16)


def _dot_split_lhs(a, b_bf16):
    hi, lo = _split_bf16(a)
    return _dot(hi, b_bf16) + _dot(lo, b_bf16)


def _rms(x, g):
    return x * lax.rsqrt(jnp.mean(x * x, axis=-1, keepdims=True) + EPS) * g


def _layernorm(x, g, b):
    xc = x - jnp.mean(x, axis=-1, keepdims=True)
    return xc * lax.rsqrt(jnp.mean(xc * xc, axis=-1, keepdims=True) + EPS) * g + b


def _sigmoid(x):
    return 1.0 / (1.0 + jnp.exp(-x))


def _silu(x):
    return x * _sigmoid(x)


def _gelu(x):
    return 0.5 * x * (1.0 + lax.erf(x * (2.0 ** -0.5)))


def _softplus(x):
    return jnp.maximum(x, 0.0) + jnp.log(1.0 + jnp.exp(-jnp.abs(x)))


def _gmlp_prompt_body(x_ref, g_ref, win_ref, bin_ref, lng_ref, lnb_ref, ws_ref, bs_ref, wout_ref,
                      o_ref, v_ref, gated_ref, *, tm, width):
    x = x_ref[...]
    h = _rms(x, g_ref[...]).astype(BF16)
    u = _gelu(_dot(h, win_ref[:, :width]) + bin_ref[:, :width])
    hv = _gelu(_dot(h, win_ref[:, width:]) + bin_ref[:, width:])
    v = _layernorm(hv, lng_ref[...], lnb_ref[...])
    v_ref[0] = v[tm - CHUNK:, :]
    vb = v.astype(BF16)
    gdim = width // GM_GROUPS
    causal = (lax.broadcasted_iota(jnp.int32, (CHUNK, CHUNK), 0)
              >= lax.broadcasted_iota(jnp.int32, (CHUNK, CHUNK), 1))
    for g in range(GM_GROUPS):
        wc = jnp.where(causal, ws_ref[g], 0.0).astype(BF16)
        bias = bs_ref[:, g:g + 1]
        for c in range(tm // CHUNK):
            rows = slice(c * CHUNK, (c + 1) * CHUNK)
            cols = slice(g * gdim, (g + 1) * gdim)
            s = _dot(wc, vb[rows, cols]) + bias
            gated_ref[rows, cols] = (u[rows, cols] * s).astype(BF16)
    o_ref[...] = x + _dot(gated_ref[...], wout_ref[...])


def _gmlp_sample_body(x_ref, g_ref, win_ref, bin_ref, lng_ref, lnb_ref, ws0_ref, bs0_ref, wout_ref,
                      o_ref, v_ref, *, width):
    x = x_ref[...]
    h = _rms(x, g_ref[...]).astype(BF16)
    u = _gelu(_dot(h, win_ref[:, :width]) + bin_ref[:, :width])
    hv = _gelu(_dot(h, win_ref[:, width:]) + bin_ref[:, width:])
    v = _layernorm(hv, lng_ref[...], lnb_ref[...])
    v_ref[...] = v
    s = v * ws0_ref[...] + bs0_ref[...]
    o_ref[...] = x + _dot((u * s).astype(BF16), wout_ref[...])


def _gmlp(X, n_prompt, n_batch, g, w_in, b_in, ln_g, ln_b, w_s, b_s, w_out, *, tm):
    M, D = X.shape
    width = w_out.shape[0]
    seq = n_prompt // n_batch
    tiles_per_seq = seq // tm
    n_sample = M - n_prompt
    g2 = g.reshape(1, D)
    b_in2 = b_in.reshape(1, 2 * width)
    ln_g2, ln_b2 = ln_g.reshape(1, width), ln_b.reshape(1, width)
    w_in_b, w_out_b = w_in.astype(BF16), w_out.astype(BF16)

    X, v_p = pl.pallas_call(
        functools.partial(_gmlp_prompt_body, tm=tm, width=width),
        grid=(n_prompt // tm,),
        in_specs=[
            pl.BlockSpec((tm, D), lambda i: (i, 0)),
            _const_spec((1, D)),
            _const_spec((D, 2 * width)),
            _const_spec((1, 2 * width)),
            _const_spec((1, width)),
            _const_spec((1, width)),
            _const_spec((GM_GROUPS, CHUNK, CHUNK)),
            _const_spec((CHUNK, GM_GROUPS)),
            _const_spec((width, D)),
        ],
        out_specs=[
            pl.BlockSpec((tm, D), lambda i: (i, 0)),
            pl.BlockSpec((1, CHUNK, width), lambda i: (i // tiles_per_seq, 0, 0)),
        ],
        out_shape=[jax.ShapeDtypeStruct((M, D), F32),
                   jax.ShapeDtypeStruct((n_batch, CHUNK, width), F32)],
        scratch_shapes=[pltpu.VMEM((tm, width), BF16)],
        input_output_aliases={0: 0},
        compiler_params=_cparams(("arbitrary",)),
        name="gmlp_prompt",
    )(X, g2, w_in_b, b_in2, ln_g2, ln_b2, w_s, b_s.T, w_out_b)

    gdim = width // GM_GROUPS
    ws0 = jnp.repeat(w_s[:, 0, 0], gdim).reshape(1, width)
    bs0 = jnp.repeat(b_s[:, 0], gdim).reshape(1, width)
    sblk = n_prompt // n_sample
    X, v_s = pl.pallas_call(
        functools.partial(_gmlp_sample_body, width=width),
        grid=(1,),
        in_specs=[
            pl.BlockSpec((n_sample, D), lambda i: (sblk, 0)),
            _const_spec((1, D)),
            _const_spec((D, 2 * width)),
            _const_spec((1, 2 * width)),
            _const_spec((1, width)),
            _const_spec((1, width)),
            _const_spec((1, width)),
            _const_spec((1, width)),
            _const_spec((width, D)),
        ],
        out_specs=[
            pl.BlockSpec((n_sample, D), lambda i: (sblk, 0)),
            pl.BlockSpec((n_sample, width), lambda i: (0, 0)),
        ],
        out_shape=[jax.ShapeDtypeStruct((M, D), F32),
                   jax.ShapeDtypeStruct((n_sample, width), F32)],
        input_output_aliases={0: 0},
        compiler_params=_cparams(("arbitrary",)),
        name="gmlp_sample",
    )(X, g2, w_in_b, b_in2, ln_g2, ln_b2, ws0, bs0, w_out_b)
    return X, v_p, v_s.reshape(n_sample, 1, width)


def _swiglu_tile(h, wg_ref, wu_ref, wd_ref, f_chunk):
    d_ff = wd_ref.shape[0]
    acc = None
    for f in range(d_ff // f_chunk):
        cols = slice(f * f_chunk, (f + 1) * f_chunk)
        a = _dot(h, wg_ref[:, cols].astype(BF16))
        b = _dot(h, wu_ref[:, cols].astype(BF16))
        part = _dot((_silu(a) * b).astype(BF16), wd_ref[cols, :].astype(BF16))
        acc = part if acc is None else acc + part
    return acc


def _ffn_dense_body(x_ref, g_ref, wg_ref, wu_ref, wd_ref, o_ref, *, f_chunk):
    x = x_ref[...]
    h = _rms(x, g_ref[...]).astype(BF16)
    o_ref[...] = x + _swiglu_tile(h, wg_ref, wu_ref, wd_ref, f_chunk)


def _ffn_dense_rows(X, g2, wg, wu, wd, layer, *, tm, first_block, n_tiles, f_chunk):
    M, D = X.shape
    d_ff = wd.shape[1]

    def layer_spec(rows, cols):
        return pl.BlockSpec((None, rows, cols), lambda i: (layer, 0, 0), pipeline_mode=pl.Buffered(1))

    return pl.pallas_call(
        functools.partial(_ffn_dense_body, f_chunk=f_chunk),
        grid=(n_tiles,),
        in_specs=[
            pl.BlockSpec((tm, D), lambda i: (i + first_block, 0)),
            _const_spec((1, D)),
            layer_spec(D, d_ff),
            layer_spec(D, d_ff),
            layer_spec(d_ff, D),
        ],
        out_specs=pl.BlockSpec((tm, D), lambda i: (i + first_block, 0)),
        out_shape=jax.ShapeDtypeStruct((M, D), F32),
        input_output_aliases={0: 0},
        compiler_params=_cparams(("arbitrary",)),
        name="ffn_dense",
    )(X, g2, wg, wu, wd)


def _ffn_dense(X, n_prompt, g, w_gate, w_up, w_down, layer, *, tm, f_chunk):
    M, D = X.shape
    n_sample = M - n_prompt
    g2 = g.reshape(1, D)
    X = _ffn_dense_rows(X, g2, w_gate, w_up, w_down, layer, tm=tm, first_block=0, n_tiles=n_prompt // tm,
                        f_chunk=f_chunk)
    X = _ffn_dense_rows(X, g2, w_gate, w_up, w_down, layer, tm=n_sample, first_block=n_prompt // n_sample,
                        n_tiles=1, f_chunk=f_chunk)
    return X


def _router_body(x_ref, g_ref, wr_ref, idx_ref, wts_ref):
    L = LANES_V7X
    h_hi, h_lo = _split_bf16(_rms(x_ref[...], g_ref[...]))
    hi_part = _dot(h_hi, wr_ref[...])
    logits = hi_part[:, :L] + (hi_part[:, L:] + _dot(h_lo, wr_ref[:, :L]))
    lane = lax.broadcasted_iota(jnp.int32, logits.shape, 1)
    neg = jnp.float32(-jnp.inf)
    logits = jnp.where(lane < N_EXPERTS, logits, neg)
    m1 = jnp.max(logits, axis=-1, keepdims=True)
    i1 = jnp.min(jnp.where(logits == m1, lane, LANES_V7X), axis=-1, keepdims=True)
    rest = jnp.where(lane == i1, neg, logits)
    m2 = jnp.max(rest, axis=-1, keepdims=True)
    i2 = jnp.min(jnp.where(rest == m2, lane, LANES_V7X), axis=-1, keepdims=True)
    e = jnp.exp(m2 - m1)
    w1 = 1.0 / (1.0 + e)
    w2 = e / (1.0 + e)
    idx_ref[...] = jnp.where(lane == 0, i1, jnp.where(lane == 1, i2, 0))
    wts_ref[...] = jnp.where(lane == 0, w1, jnp.where(lane == 1, w2, 0.0))


def _slot_table_body(pos1_ref, pos2_ref, pad_lo_ref, pad_hi_ref, tok_ref, *, n_tokens, n_ranges):
    def pad(s, carry):
        tok_ref[s] = 0
        return carry

    for r in range(n_ranges):
        lax.fori_loop(pad_lo_ref[r], pad_hi_ref[r], pad, 0)

    def place(t, carry):
        tok_ref[pos1_ref[t]] = t
        tok_ref[pos2_ref[t]] = t
        return carry

    lax.fori_loop(0, n_tokens, place, 0, unroll=8)


def _issue_row_gather(src_hbm, idx_ref, base, buf, sem, n_rows):
    def body(i, carry):
        for p in range(2):
            r = 2 * i + p
            pltpu.make_async_copy(src_hbm.at[pl.ds(idx_ref[base + r], 1), :], buf.at[pl.ds(r, 1), :],
                                  sem).start(priority=p)
        return carry
    lax.fori_loop(0, n_rows // 2, body, 0, unroll=4)


def _wait_row_gather(src_hbm, buf, sem, n_rows):
    pltpu.make_async_copy(src_hbm.at[pl.ds(0, n_rows), :], buf, sem).wait()


def _moe_ffn_body(tile_expert_ref, n_tiles_ref, tok_ref, x_hbm, g_ref, wg_ref, wu_ref, wd_ref,
                  y_ref, buf_ref, sem_ref, *, tm, f_chunk):
    t = pl.program_id(0)
    n_tiles = n_tiles_ref[0]
    slot = t % 2

    @pl.when(t == 0)
    def _():
        _issue_row_gather(x_hbm, tok_ref, 0, buf_ref.at[0], sem_ref.at[0], tm)

    @pl.when(t + 1 < n_tiles)
    def _():
        _issue_row_gather(x_hbm, tok_ref, (t + 1) * tm, buf_ref.at[1 - slot], sem_ref.at[1 - slot], tm)

    @pl.when(t < n_tiles)
    def _():
        _wait_row_gather(x_hbm, buf_ref.at[slot], sem_ref.at[slot], tm)
        h = _rms(buf_ref[slot], g_ref[...]).astype(BF16)
        y_ref[...] = _swiglu_tile(h, wg_ref, wu_ref, wd_ref, f_chunk)

    @pl.when(t >= n_tiles)
    def _():
        y_ref[...] = jnp.zeros_like(y_ref)


def _moe_combine_body(p1_ref, p2_ref, x_ref, wts_ref, y_hbm, gf_ref, o_ref, buf1_ref, buf2_ref, sem_ref,
                      *, tm, row0, n_steps, final_norm):
    i = pl.program_id(0)
    slot = i % 2

    def issue(step, s):
        base = row0 + step * tm

        def body(r, carry):
            for p_ref, buf, k in ((p1_ref, buf1_ref, 0), (p2_ref, buf2_ref, 1)):
                pltpu.make_async_copy(y_hbm.at[pl.ds(p_ref[base + r], 1), :], buf.at[s, pl.ds(r, 1), :],
                                      sem_ref.at[k, s]).start(priority=k)
            return carry
        lax.fori_loop(0, tm, body, 0, unroll=8)

    @pl.when(i == 0)
    def _():
        issue(0, 0)

    @pl.when(i + 1 < n_steps)
    def _():
        issue(i + 1, 1 - slot)

    _wait_row_gather(y_hbm, buf1_ref.at[slot], sem_ref.at[0, slot], tm)
    _wait_row_gather(y_hbm, buf2_ref.at[slot], sem_ref.at[1, slot], tm)
    out = x_ref[...] + (wts_ref[:, 0:1] * buf1_ref[slot] + wts_ref[:, 1:2] * buf2_ref[slot])
    if final_norm:
        out = _rms(out, gf_ref[...])
    o_ref[...] = out


def _moe(X, n_prompt, g, w_router, w_gate, w_up, w_down, layer, g_final, *, tm_tok, tm_out, tm, f_chunk,
         final_norm):
    M, D = X.shape
    E = N_EXPERTS
    d_ff = w_down.shape[2]
    g2 = g.reshape(1, D)
    L = LANES_V7X
    wr = jnp.zeros((D, L), F32).at[:, :E].set(w_router)
    wr_hi = wr.astype(BF16)
    wr_split = jnp.concatenate([wr_hi, (wr - wr_hi.astype(F32)).astype(BF16)], axis=1)

    idx, wts = pl.pallas_call(
        _router_body,
        grid=(M // tm_tok,),
        in_specs=[pl.BlockSpec((tm_tok, D), lambda i: (i, 0)), _const_spec((1, D)), _const_spec((D, 2 * L))],
        out_specs=[pl.BlockSpec((tm_tok, L), lambda i: (i, 0)), pl.BlockSpec((tm_tok, L), lambda i: (i, 0))],
        out_shape=[jax.ShapeDtypeStruct((M, L), jnp.int32), jax.ShapeDtypeStruct((M, L), F32)],
        compiler_params=_cparams(("arbitrary",)),
        name="moe_router",
    )(X, g2, wr_split)

    i1, i2 = idx[:, 0], idx[:, 1]
    eids = jnp.arange(E, dtype=jnp.int32)
    sel = ((i1[:, None] == eids) | (i2[:, None] == eids)).astype(jnp.int32)
    count = jnp.sum(sel, axis=0)
    tiles_e = (count + tm - 1) // tm
    tiles_end = jnp.cumsum(tiles_e)
    n_tiles = tiles_end[-1]
    start_e = (tiles_end - tiles_e) * tm
    pos = start_e[None, :] + jnp.cumsum(sel, axis=0) - sel
    pos1 = jnp.sum(jnp.where(i1[:, None] == eids, pos, 0), axis=1).astype(jnp.int32)
    pos2 = jnp.sum(jnp.where(i2[:, None] == eids, pos, 0), axis=1).astype(jnp.int32)
    max_tiles = (2 * M + E * (tm - 1)) // tm
    n_slots = max_tiles * tm
    pad_lo = jnp.concatenate([start_e + count, (n_tiles * tm)[None]]).astype(jnp.int32)
    pad_hi = jnp.concatenate([tiles_end * tm, jnp.array([n_slots], jnp.int32)]).astype(jnp.int32)
    tile_ids = jnp.arange(max_tiles, dtype=jnp.int32)
    tile_expert = jnp.sum((jnp.minimum(tile_ids, n_tiles - 1)[:, None] >= tiles_end[None, :]).astype(jnp.int32),
                          axis=1)
    tile_expert = jnp.minimum(tile_expert, E - 1).astype(jnp.int32)
    n_tiles_arr = n_tiles.astype(jnp.int32).reshape(1)

    smem = pl.BlockSpec(memory_space=pltpu.SMEM)
    tok_of_slot = pl.pallas_call(
        functools.partial(_slot_table_body, n_tokens=M, n_ranges=E + 1),
        in_specs=[smem, smem, smem, smem],
        out_specs=smem,
        out_shape=jax.ShapeDtypeStruct((n_slots,), jnp.int32),
        name="moe_slot_table",
    )(pos1, pos2, pad_lo, pad_hi)

    def expert_spec(rows, cols, buffers=1):
        return pl.BlockSpec((None, None, rows, cols), lambda t, te, nt, tok: (layer, te[t], 0, 0),
                            pipeline_mode=pl.Buffered(buffers))

    y_sorted = pl.pallas_call(
        functools.partial(_moe_ffn_body, tm=tm, f_chunk=f_chunk),
        grid_spec=pltpu.PrefetchScalarGridSpec(
            num_scalar_prefetch=3,
            grid=(max_tiles,),
            in_specs=[
                pl.BlockSpec(memory_space=pl.ANY),
                pl.BlockSpec((1, D), lambda t, te, nt, tok: (0, 0)),
                expert_spec(D, d_ff),
                expert_spec(D, d_ff),
                expert_spec(d_ff, D, buffers=2),
            ],
            out_specs=pl.BlockSpec((tm, D), lambda t, te, nt, tok: (t, 0)),
            scratch_shapes=[pltpu.VMEM((2, tm, D), F32), pltpu.SemaphoreType.DMA((2,))],
        ),
        out_shape=jax.ShapeDtypeStruct((n_slots, D), F32),
        compiler_params=_cparams(("arbitrary",)),
        name="moe_ffn",
    )(tile_expert, n_tiles_arr, tok_of_slot, X, g2, w_gate, w_up, w_down)

    def combine_rows(tm_c, first_block, n_steps, in_place):
        out_first = first_block if in_place else 0
        return pl.pallas_call(
            functools.partial(_moe_combine_body, tm=tm_c, row0=first_block * tm_c, n_steps=n_steps,
                              final_norm=final_norm),
            grid_spec=pltpu.PrefetchScalarGridSpec(
                num_scalar_prefetch=2,
                grid=(n_steps,),
                in_specs=[
                    pl.BlockSpec((tm_c, D), lambda i, p1, p2: (i + first_block, 0)),
                    pl.BlockSpec((tm_c, L), lambda i, p1, p2: (i + first_block, 0)),
                    pl.BlockSpec(memory_space=pl.ANY),
                    pl.BlockSpec((1, D), lambda i, p1, p2: (0, 0)),
                ],
                out_specs=pl.BlockSpec((tm_c, D), lambda i, p1, p2: (i + out_first, 0)),
                scratch_shapes=[pltpu.VMEM((2, tm_c, D), F32), pltpu.VMEM((2, tm_c, D), F32),
                                pltpu.SemaphoreType.DMA((2, 2))],
            ),
            out_shape=jax.ShapeDtypeStruct((M if in_place else n_steps * tm_c, D), F32),
            input_output_aliases={2: 0} if in_place else {},
            compiler_params=_cparams(("arbitrary",)),
            name="moe_combine",
        )(pos1, pos2, X, wts, y_sorted, g_final.reshape(1, D))

    if not final_norm:
        return combine_rows(tm_tok, 0, M // tm_tok, True)
    n_sample = M - n_prompt
    y_p = combine_rows(tm_out, 0, n_prompt // tm_out, False)
    y_s = combine_rows(n_sample, n_prompt // n_sample, 1, False)
    return y_p, y_s


def _ssd_post(y, xs, z, dsk, ng, wout_ref, ynorm_ref, inner):
    y = (y + dsk * xs) * _silu(z)
    gw = inner // SSM_GROUPS
    for g in range(SSM_GROUPS):
        cols = slice(g * gw, (g + 1) * gw)
        yg = y[:, cols]
        yg = yg * lax.rsqrt(jnp.mean(yg * yg, axis=-1, keepdims=True) + EPS) * ng[:, cols]
        ynorm_ref[:, cols] = yg.astype(BF16)
    return _dot(ynorm_ref[...], wout_ref[...])


def _mamba_prompt_body(x_ref, g_ref, win_ref, wdt_ref, cw_ref, cb_ref, dtb_ref, a_ref, dsk_ref, ng_ref,
                       wout_ref, r_ref, o_ref, conv_ref, ssm_ref, ctx_ref, s_ref, y_ref, ynorm_ref,
                       *, tm, tiles_per_seq, inner, bc):
    i = pl.program_id(0)

    @pl.when(i % tiles_per_seq == 0)
    def _():
        ctx_ref[0:SSM_PAD, :] = jnp.zeros((SSM_PAD, ctx_ref.shape[1]), F32)
        s_ref[...] = jnp.zeros_like(s_ref)

    x = x_ref[...]
    h = _rms(x, g_ref[...]).astype(BF16)
    proj = _dot(h, win_ref[...])
    z = proj[:, :inner]
    xbc = proj[:, inner:]
    ctx_ref[SSM_PAD:SSM_PAD + tm, :] = xbc
    conv_ref[0] = ctx_ref[SSM_PAD + tm - (SSM_CONV - 1):SSM_PAD + tm, :]
    conv = cb_ref[...] + xbc * cw_ref[SSM_CONV - 1:SSM_CONV, :]
    for k in range(SSM_CONV - 1):
        off = SSM_PAD - (SSM_CONV - 1) + k
        conv = conv + ctx_ref[off:off + tm, :] * cw_ref[k:k + 1, :]
    ctx_ref[0:SSM_PAD, :] = ctx_ref[tm:tm + SSM_PAD, :]
    act = _silu(conv)
    xs = act[:, :inner]
    bm = act[:, inner:inner + bc].astype(BF16)
    cm = act[:, inner + bc:].astype(BF16)
    dt = _softplus(_dot(h, wdt_ref[...]) + dtb_ref[...])
    da = dt * a_ref[...]

    rows_i = lax.broadcasted_iota(jnp.int32, (CHUNK, CHUNK), 0)
    cols_i = lax.broadcasted_iota(jnp.int32, (CHUNK, CHUNK), 1)
    causal = rows_i >= cols_i
    tril = causal.astype(F32)
    gw = inner // SSM_GROUPS
    hpg = gw // SSM_HEAD_DIM
    lane_head = lax.broadcasted_iota(jnp.int32, (CHUNK, gw), 1) // SSM_HEAD_DIM
    r_mat = r_ref[...]

    for c in range(tm // CHUNK):
        rows = slice(c * CHUNK, (c + 1) * CHUNK)
        dt_c = dt[rows, :]
        cum = _dot_hi(tril, da[rows, :])
        cum_t = cum.T
        dt_t = dt_c.T
        ecum = jnp.exp(cum)
        e_full = _dot_split_lhs(ecum, r_mat)
        w_in = jnp.exp(cum[CHUNK - 1:CHUNK, :] - cum) * dt_c
        xs_c = xs[rows, :]
        xw = (xs_c * _dot_split_lhs(w_in, r_mat)).astype(BF16)
        xs_b = xs_c.astype(BF16)
        for g in range(SSM_GROUPS):
            gcols = slice(g * gw, (g + 1) * gw)
            ncols = slice(g * SSM_STATE, (g + 1) * SSM_STATE)
            b_g = bm[rows, ncols]
            c_g = cm[rows, ncols]
            cb = _dot_nt(c_g, b_g)
            x_g = xs_b[:, gcols]
            s_g = s_ref[g]
            wts, x_heads = [], []
            for r in range(hpg):
                hd = g * hpg + r
                seg = cum[:, hd:hd + 1] - cum_t[hd:hd + 1, :]
                decay = jnp.where(causal, jnp.exp(jnp.where(causal, seg, 0.0)), 0.0)
                wts.append((cb * decay * dt_t[hd:hd + 1, :]).astype(BF16))
                x_heads.append(jnp.where(lane_head == r, x_g, jnp.zeros_like(x_g)))
            y_intra = _dot(jnp.concatenate(wts, axis=1), jnp.concatenate(x_heads, axis=0))
            y_ref[rows, gcols] = y_intra + _dot_nt(c_g, s_g.astype(BF16)) * e_full[:, gcols]
            upd = _dot_tn(xw[:, gcols], b_g)
            for r in range(hpg):
                hd = g * hpg + r
                hrows = slice(r * SSM_HEAD_DIM, (r + 1) * SSM_HEAD_DIM)
                s_ref[g, hrows, :] = s_g[hrows, :] * ecum[CHUNK - 1:CHUNK, hd:hd + 1] + upd[hrows, :]

    out = _ssd_post(y_ref[...], xs, z, dsk_ref[...], ng_ref[...], wout_ref, ynorm_ref, inner)
    o_ref[...] = x + out
    ssm_ref[0] = s_ref[...]


def _mamba_sample_pre_body(x_ref, g_ref, win_ref, wdt_ref, cw_ref, cb_ref, dtb_ref, a_ref, r_ref, cs_ref,
                           z_ref, xs_ref, b_ref, c_ref, xdt_ref, e_ref, newconv_ref, *, inner, bc):
    h = _rms(x_ref[...], g_ref[...]).astype(BF16)
    proj = _dot(h, win_ref[...])
    z_ref[...] = proj[:, :inner]
    xbc = proj[:, inner:]
    conv = cb_ref[...] + xbc * cw_ref[SSM_CONV - 1:SSM_CONV, :]
    for k in range(SSM_CONV - 1):
        conv = conv + cs_ref[k] * cw_ref[k:k + 1, :]
    for k in range(SSM_CONV - 2):
        newconv_ref[k] = cs_ref[k + 1]
    newconv_ref[SSM_CONV - 2] = xbc
    act = _silu(conv)
    xs = act[:, :inner]
    xs_ref[...] = xs
    b_ref[...] = act[:, inner:inner + bc]
    c_ref[...] = act[:, inner + bc:]
    dt = _softplus(_dot(h, wdt_ref[...]) + dtb_ref[...])
    xdt_ref[...] = xs * _dot_split_lhs(dt, r_ref[...])
    e_ref[...] = jnp.exp(dt * a_ref[...])


def _mamba_sample_state_body(e_ref, h0_ref, xdt_t_ref, b_ref, c_ref, hn_ref, y_ref, *, bt, inner, heads):
    i = pl.program_id(0)
    gw = inner // SSM_GROUPS
    hpg = gw // SSM_HEAD_DIM
    row_id = lax.broadcasted_iota(jnp.int32, (bt, bt * SSM_STATE), 0)
    lane_tok = lax.broadcasted_iota(jnp.int32, (bt, bt * SSM_STATE), 1) // SSM_STATE
    for g in range(SSM_GROUPS):
        rows = slice(g * gw, (g + 1) * gw)
        ncols = slice(g * SSM_STATE, (g + 1) * SSM_STATE)
        states = []
        for j in range(bt):
            b_row = b_ref[0, j:j + 1, ncols]
            for r in range(hpg):
                hrows = slice(g * gw + r * SSM_HEAD_DIM, g * gw + (r + 1) * SSM_HEAD_DIM)
                e = e_ref[(i * bt + j) * heads + g * hpg + r]
                hn_ref[j, hrows, :] = h0_ref[j, hrows, :] * e + xdt_t_ref[0, hrows, j:j + 1] * b_row
            states.append(hn_ref[j, rows, :].astype(BF16))
        c_g = c_ref[0, :, ncols]
        c_diag = jnp.where(row_id == lane_tok, jnp.concatenate([c_g] * bt, axis=1), 0.0).astype(BF16)
        y_ref[:, rows] = _dot_nt(c_diag, jnp.concatenate(states, axis=1))


def _mamba_sample_post_body(x_ref, y_ref, xs_ref, z_ref, dsk_ref, ng_ref, wout_ref, o_ref, ynorm_ref, *, inner):
    out = _ssd_post(y_ref[...], xs_ref[...], z_ref[...], dsk_ref[...], ng_ref[...], wout_ref, ynorm_ref, inner)
    o_ref[...] = x_ref[...] + out


def _mamba(X, n_prompt, n_batch, state_ssm_all, layer, state_conv, g, w_in, conv_w, conv_b, dt_bias, a_log, d_skip,
           norm_g, w_out, *, tm, bt):
    M, D = X.shape
    inner = w_out.shape[0]
    heads = a_log.shape[0]
    conv_dim = conv_w.shape[1]
    bc = (conv_dim - inner) // 2
    n_sample = M - n_prompt
    seq = n_prompt // n_batch
    tiles_per_seq = seq // tm
    gw = inner // SSM_GROUPS
    L = LANES_V7X

    g2 = g.reshape(1, D)
    w_main = w_in[:, :inner + conv_dim].astype(BF16)
    w_dt = jnp.zeros((D, L), F32).at[:, :heads].set(w_in[:, inner + conv_dim:]).astype(BF16)
    cb2 = conv_b.reshape(1, conv_dim)
    dtb = jnp.zeros((1, L), F32).at[0, :heads].set(dt_bias)
    a_neg = jnp.zeros((1, L), F32).at[0, :heads].set(-jnp.exp(a_log))
    dsk = jnp.repeat(d_skip, SSM_HEAD_DIM).reshape(1, inner)
    ng2 = norm_g.reshape(1, inner)
    w_out_b = w_out.astype(BF16)
    r_mat = (jnp.arange(L, dtype=jnp.int32)[:, None]
             == (jnp.arange(inner, dtype=jnp.int32) // SSM_HEAD_DIM)[None, :]).astype(BF16)

    X, conv_p, ssm_p = pl.pallas_call(
        functools.partial(_mamba_prompt_body, tm=tm, tiles_per_seq=tiles_per_seq, inner=inner, bc=bc),
        grid=(n_prompt // tm,),
        in_specs=[
            pl.BlockSpec((tm, D), lambda i: (i, 0)),
            _const_spec((1, D)),
            _const_spec((D, inner + conv_dim)),
            _const_spec((D, L)),
            _const_spec((SSM_CONV, conv_dim)),
            _const_spec((1, conv_dim)),
            _const_spec((1, L)),
            _const_spec((1, L)),
            _const_spec((1, inner)),
            _const_spec((1, inner)),
            _const_spec((inner, D)),
            _const_spec((L, inner)),
        ],
        out_specs=[
            pl.BlockSpec((tm, D), lambda i: (i, 0)),
            pl.BlockSpec((1, SSM_CONV - 1, conv_dim), lambda i: (i // tiles_per_seq, 0, 0)),
            pl.BlockSpec((1, SSM_GROUPS, gw, SSM_STATE), lambda i: (i // tiles_per_seq, 0, 0, 0)),
        ],
        out_shape=[
            jax.ShapeDtypeStruct((M, D), F32),
            jax.ShapeDtypeStruct((n_batch, SSM_CONV - 1, conv_dim), F32),
            jax.ShapeDtypeStruct((n_batch, SSM_GROUPS, gw, SSM_STATE), F32),
        ],
        scratch_shapes=[
            pltpu.VMEM((SSM_PAD + tm, conv_dim), F32),
            pltpu.VMEM((SSM_GROUPS, gw, SSM_STATE), F32),
            pltpu.VMEM((tm, inner), F32),
            pltpu.VMEM((tm, inner), BF16),
        ],
        input_output_aliases={0: 0},
        compiler_params=_cparams(("arbitrary",)),
        name="mamba_prompt",
    )(X, g2, w_main, w_dt, conv_w, cb2, dtb, a_neg, dsk, ng2, w_out_b, r_mat)

    sblk = n_prompt // n_sample
    cs_t = jnp.transpose(state_conv, (1, 0, 2))
    z, xs, b_m, c_m, xdt, e_tok, newconv_t = pl.pallas_call(
        functools.partial(_mamba_sample_pre_body, inner=inner, bc=bc),
        grid=(1,),
        in_specs=[
            pl.BlockSpec((n_sample, D), lambda i: (sblk, 0)),
            _const_spec((1, D)),
            _const_spec((D, inner + conv_dim)),
            _const_spec((D, L)),
            _const_spec((SSM_CONV, conv_dim)),
            _const_spec((1, conv_dim)),
            _const_spec((1, L)),
            _const_spec((1, L)),
            _const_spec((L, inner)),
            _const_spec((SSM_CONV - 1, n_sample, conv_dim)),
        ],
        out_specs=[
            pl.BlockSpec((n_sample, inner), lambda i: (0, 0)),
            pl.BlockSpec((n_sample, inner), lambda i: (0, 0)),
            pl.BlockSpec((n_sample, bc), lambda i: (0, 0)),
            pl.BlockSpec((n_sample, bc), lambda i: (0, 0)),
            pl.BlockSpec((n_sample, inner), lambda i: (0, 0)),
            pl.BlockSpec((n_sample, L), lambda i: (0, 0)),
            pl.BlockSpec((SSM_CONV - 1, n_sample, conv_dim), lambda i: (0, 0, 0)),
        ],
        out_shape=[
            jax.ShapeDtypeStruct((n_sample, inner), F32),
            jax.ShapeDtypeStruct((n_sample, inner), F32),
            jax.ShapeDtypeStruct((n_sample, bc), F32),
            jax.ShapeDtypeStruct((n_sample, bc), F32),
            jax.ShapeDtypeStruct((n_sample, inner), F32),
            jax.ShapeDtypeStruct((n_sample, L), F32),
            jax.ShapeDtypeStruct((SSM_CONV - 1, n_sample, conv_dim), F32),
        ],
        compiler_params=_cparams(("arbitrary",)),
        name="mamba_sample_pre",
    )(X, g2, w_main, w_dt, conv_w, cb2, dtb, a_neg, r_mat, cs_t)
    conv_s = jnp.transpose(newconv_t, (1, 0, 2))

    nblk = n_sample // bt

    def to_cols(arr):
        return jnp.transpose(arr.reshape(nblk, bt, inner), (0, 2, 1))

    h0 = state_ssm_all.reshape(state_ssm_all.shape[0], n_sample, inner, SSM_STATE)
    h_new, y_s = pl.pallas_call(
        functools.partial(_mamba_sample_state_body, bt=bt, inner=inner, heads=heads),
        grid_spec=pltpu.PrefetchScalarGridSpec(
            num_scalar_prefetch=1,
            grid=(nblk,),
            in_specs=[
                pl.BlockSpec((None, bt, inner, SSM_STATE), lambda i, e: (layer, i, 0, 0)),
                pl.BlockSpec((1, inner, bt), lambda i, e: (i, 0, 0)),
                pl.BlockSpec((1, bt, bc), lambda i, e: (i, 0, 0)),
                pl.BlockSpec((1, bt, bc), lambda i, e: (i, 0, 0)),
            ],
            out_specs=[
                pl.BlockSpec((bt, inner, SSM_STATE), lambda i, e: (i, 0, 0)),
                pl.BlockSpec((bt, inner), lambda i, e: (i, 0)),
            ],
        ),
        out_shape=[
            jax.ShapeDtypeStruct((n_sample, inner, SSM_STATE), F32),
            jax.ShapeDtypeStruct((n_sample, inner), F32),
        ],
        compiler_params=_cparams(("arbitrary",)),
        name="mamba_sample_state",
    )(e_tok[:, :heads].reshape(n_sample * heads), h0, to_cols(xdt), b_m.reshape(nblk, bt, bc),
      c_m.reshape(nblk, bt, bc))

    X = pl.pallas_call(
        functools.partial(_mamba_sample_post_body, inner=inner),
        grid=(1,),
        in_specs=[
            pl.BlockSpec((n_sample, D), lambda i: (sblk, 0)),
            pl.BlockSpec((n_sample, inner), lambda i: (0, 0)),
            pl.BlockSpec((n_sample, inner), lambda i: (0, 0)),
            pl.BlockSpec((n_sample, inner), lambda i: (0, 0)),
            _const_spec((1, inner)),
            _const_spec((1, inner)),
            _const_spec((inner, D)),
        ],
        out_specs=pl.BlockSpec((n_sample, D), lambda i: (sblk, 0)),
        out_shape=jax.ShapeDtypeStruct((M, D), F32),
        scratch_shapes=[pltpu.VMEM((n_sample, inner), BF16)],
        input_output_aliases={0: 0},
        compiler_params=_cparams(("arbitrary",)),
        name="mamba_sample_post",
    )(X, y_s, xs, z, dsk, ng2, w_out_b)

    ssm_p = ssm_p.reshape(n_batch, heads, SSM_HEAD_DIM, SSM_STATE)
    ssm_s = h_new.reshape(n_sample, heads, SSM_HEAD_DIM, SSM_STATE)
    return X, conv_p, conv_s, ssm_p, ssm_s


def _causal_dwconv_tile(ctx_ref, dw_ref, acc_ref, tm, d):
    S = SUBLANES_V7X
    first = CFM_PAD - (CFM_KERNEL - 1)
    rb, lb = CONV_ROW_BLOCK, CONV_LANE_BLOCK
    for r0 in range(0, tm, rb):
        for l0 in range(0, d, lb):
            lanes = slice(l0, l0 + lb)
            y = None
            for b in range(S):
                pb = None
                for a in range((first + CFM_KERNEL - 1) // S + 1):
                    k = S * a + b - first
                    if 0 <= k < CFM_KERNEL:
                        term = ctx_ref[r0 + S * a:r0 + S * a + rb + S, lanes] * dw_ref[k:k + 1, lanes]
                        pb = term if pb is None else pb + term
                part = pb[b:b + rb, :]
                y = part if y is None else y + part
            acc_ref[r0:r0 + rb, lanes] = y


def _cfm_prompt_body(x_ref, g_ref, w1_ref, b1_ref, dw_ref, dwb_ref, lng_ref, lnb_ref, w2_ref, b2_ref,
                     o_ref, buf_ref, ctx_ref, acc_ref, *, tm, tiles_per_seq, d):
    i = pl.program_id(0)

    @pl.when(i % tiles_per_seq == 0)
    def _():
        ctx_ref[0:CFM_PAD, :] = jnp.zeros((CFM_PAD, d), F32)
        ctx_ref[CFM_PAD + tm:CFM_PAD + tm + SUBLANES_V7X, :] = jnp.zeros((SUBLANES_V7X, d), F32)

    x = x_ref[...]
    h = _rms(x, g_ref[...]).astype(BF16)
    a = _dot(h, w1_ref[...]) + b1_ref[...]
    ctx_ref[CFM_PAD:CFM_PAD + tm, :] = a[:, :d] * _sigmoid(a[:, d:])
    _causal_dwconv_tile(ctx_ref, dw_ref, acc_ref, tm, d)
    buf_ref[0] = ctx_ref[CFM_PAD + tm - (CFM_KERNEL - 1):CFM_PAD + tm, :]
    ctx_ref[0:CFM_PAD, :] = ctx_ref[tm:tm + CFM_PAD, :]
    acc = acc_ref[...] + dwb_ref[...]
    hc = _silu(_layernorm(acc, lng_ref[...], lnb_ref[...])).astype(BF16)
    o_ref[...] = x + _dot(hc, w2_ref[...]) + b2_ref[...]


def _cfm_sample_body(x_ref, g_ref, w1_ref, b1_ref, dw_ref, dwb_ref, lng_ref, lnb_ref, w2_ref, b2_ref, cs_ref,
                     o_ref, new_ref, *, d):
    x = x_ref[...]
    h = _rms(x, g_ref[...]).astype(BF16)
    a = _dot(h, w1_ref[...]) + b1_ref[...]
    glu = a[:, :d] * _sigmoid(a[:, d:])
    acc = dwb_ref[...] + glu * dw_ref[CFM_KERNEL - 1:CFM_KERNEL, :]
    for k in range(CFM_KERNEL - 1):
        acc = acc + cs_ref[k] * dw_ref[k:k + 1, :]
    for k in range(CFM_KERNEL - 2):
        new_ref[k] = cs_ref[k + 1]
    new_ref[CFM_KERNEL - 2] = glu
    hc = _silu(_layernorm(acc, lng_ref[...], lnb_ref[...])).astype(BF16)
    o_ref[...] = x + _dot(hc, w2_ref[...]) + b2_ref[...]


def _conformer(X, n_prompt, n_batch, state_conv, g, w_pw1, b_pw1, dw_w, dw_b, ln_g, ln_b, w_pw2, b_pw2,
               *, tm, bt):
    M, D = X.shape
    n_sample = M - n_prompt
    seq = n_prompt // n_batch
    tiles_per_seq = seq // tm
    K = CFM_KERNEL
    g2 = g.reshape(1, D)
    w1, w2 = w_pw1.astype(BF16), w_pw2.astype(BF16)
    b1, b2 = b_pw1.reshape(1, 2 * D), b_pw2.reshape(1, D)
    dwb, lng, lnb = dw_b.reshape(1, D), ln_g.reshape(1, D), ln_b.reshape(1, D)
    weight_specs = [
        _const_spec((1, D)), _const_spec((D, 2 * D)), _const_spec((1, 2 * D)), _const_spec((K, D)),
        _const_spec((1, D)), _const_spec((1, D)), _const_spec((1, D)), _const_spec((D, D)), _const_spec((1, D)),
    ]
    weights = (g2, w1, b1, dw_w, dwb, lng, lnb, w2, b2)

    X, buf_p = pl.pallas_call(
        functools.partial(_cfm_prompt_body, tm=tm, tiles_per_seq=tiles_per_seq, d=D),
        grid=(n_prompt // tm,),
        in_specs=[pl.BlockSpec((tm, D), lambda i: (i, 0))] + weight_specs,
        out_specs=[
            pl.BlockSpec((tm, D), lambda i: (i, 0)),
            pl.BlockSpec((1, K - 1, D), lambda i: (i // tiles_per_seq, 0, 0)),
        ],
        out_shape=[jax.ShapeDtypeStruct((M, D), F32), jax.ShapeDtypeStruct((n_batch, K - 1, D), F32)],
        scratch_shapes=[pltpu.VMEM((CFM_PAD + tm + SUBLANES_V7X, D), F32), pltpu.VMEM((tm, D), F32)],
        input_output_aliases={0: 0},
        compiler_params=_cparams(("arbitrary",)),
        name="conformer_prompt",
    )(X, *weights)

    first = n_prompt // bt
    cs_t = jnp.transpose(state_conv, (1, 0, 2))
    X, new_t = pl.pallas_call(
        functools.partial(_cfm_sample_body, d=D),
        grid=(n_sample // bt,),
        in_specs=[pl.BlockSpec((bt, D), lambda i: (i + first, 0))] + weight_specs
        + [pl.BlockSpec((K - 1, bt, D), lambda i: (0, i, 0))],
        out_specs=[
            pl.BlockSpec((bt, D), lambda i: (i + first, 0)),
            pl.BlockSpec((K - 1, bt, D), lambda i: (0, i, 0)),
        ],
        out_shape=[jax.ShapeDtypeStruct((M, D), F32), jax.ShapeDtypeStruct((K - 1, n_sample, D), F32)],
        input_output_aliases={0: 0},
        compiler_params=_cparams(("arbitrary",)),
        name="conformer_sample",
    )(X, *weights, cs_t)
    return X, buf_p, jnp.transpose(new_t, (1, 0, 2))


def _stack(parts):
    return parts[0][None] if len(parts) == 1 else jnp.stack(parts)


def kernel(x_prompt, x_sample, state_ssm, state_conv_ssm, state_conv_cfm, norm_mix_g, norm_ffn_g, norm_final_g, a_w_in, a_b_in, a_ln_g, a_ln_b, a_w_s, a_b_s, a_w_out, b_w_in, b_conv_w, b_conv_b, b_dt_bias, b_a_log, b_d, b_norm_g, b_w_out, c_w_pw1, c_b_pw1, c_dw_w, c_dw_b, c_ln_g, c_ln_b, c_w_pw2, c_b_pw2, f_w_gate, f_w_up, f_w_down, e_w_router, e_w_gate, e_w_up, e_w_down):
    n_batch, seq, D = x_prompt.shape
    n_sample = x_sample.shape[0]
    n_prompt = n_batch * seq
    depth = norm_mix_g.shape[0]
    d_ff = f_w_gate.shape[2]
    assert x_sample.shape[1] == 1 and n_prompt % n_sample == 0 and seq % CHUNK == 0

    X = jnp.concatenate([x_prompt.reshape(n_prompt, D), x_sample.reshape(n_sample, D)], axis=0)
    v_p, v_s, ssm_p, ssm_s, cs_p, cs_s, cc_p, cc_s = [], [], [], [], [], [], [], []
    for i in range(depth):
        kind, j = i % 3, i // 3
        if kind == 0:
            X, vp, vs = _gmlp(X, n_prompt, n_batch, norm_mix_g[i], a_w_in[j], a_b_in[j], a_ln_g[j], a_ln_b[j],
                              a_w_s[j], a_b_s[j], a_w_out[j], tm=512)
            v_p.append(vp)
            v_s.append(vs)
        elif kind == 1:
            X, cbp, cbs, hlp, hls = _mamba(X, n_prompt, n_batch, state_ssm, j, state_conv_ssm[j], norm_mix_g[i],
                                           b_w_in[j], b_conv_w[j], b_conv_b[j], b_dt_bias[j], b_a_log[j], b_d[j],
                                           b_norm_g[j], b_w_out[j], tm=256, bt=8)
            cs_p.append(cbp)
            cs_s.append(cbs)
            ssm_p.append(hlp)
            ssm_s.append(hls)
        else:
            X, cbp, cbs = _conformer(X, n_prompt, n_batch, state_conv_cfm[j], norm_mix_g[i], c_w_pw1[j], c_b_pw1[j],
                                     c_dw_w[j], c_dw_b[j], c_ln_g[j], c_ln_b[j], c_w_pw2[j], c_b_pw2[j],
                                     tm=512, bt=32)
            cc_p.append(cbp)
            cc_s.append(cbs)
        k = i // 2
        if i % 2 == 0:
            X = _ffn_dense(X, n_prompt, norm_ffn_g[i], f_w_gate, f_w_up, f_w_down, k,
                           tm=512, f_chunk=FFN_CHUNK)
        else:
            X = _moe(X, n_prompt, norm_ffn_g[i], e_w_router[k], e_w_gate, e_w_up, e_w_down, k, norm_final_g,
                     tm_tok=384, tm_out=512, tm=512, f_chunk=FFN_CHUNK, final_norm=(i == depth - 1))
    assert depth % 2 == 0
    y_prompt, y_sample = X
    y_prompt = y_prompt.reshape(n_batch, seq, D)
    y_sample = y_sample.reshape(n_sample, 1, D)
    return (y_prompt, y_sample, _stack(v_p), _stack(v_s), _stack(ssm_p), _stack(ssm_s),
            _stack(cs_p), _stack(cs_s), _stack(cc_p), _stack(cc_s))
```
